```python
import jax, jax.numpy as jnp
from jax import lax
import numpy as np

D_MODEL = 1024
BATCH = 4
SEQ = 4096
DEPTH = 4

GRID_W = 64
CTX_LEN = 256
N_MIXERS = 2
CHUNK = 128
A_WIDTH = 2 * D_MODEL
A_GROUPS = 8
A_GW = A_WIDTH // A_GROUPS
HEAD_DIM = 64
N_HEADS = D_MODEL // HEAD_DIM
N_KV_HEADS = N_HEADS // 4
GQA_GROUP = N_HEADS // N_KV_HEADS
B_WIDTH = N_HEADS * HEAD_DIM
KV_WIDTH = N_KV_HEADS * HEAD_DIM
WINDOW = 128
ATT_BLOCK = 128
BAND = ATT_BLOCK + 2 * WINDOW
ROPE_AXIS_DIM = HEAD_DIM // 2
ROPE_BASE = 10000.0
LN_EPS = 1e-5
NEG_INF = -1e30

kernel_name = "hybrid_gmlp_swa_prefix_dit"


def layer_norm(x, g, b):
    xf = x.astype(jnp.float32)
    mu = jnp.mean(xf, axis=-1, keepdims=True)
    var = jnp.mean(jnp.square(xf - mu), axis=-1, keepdims=True)
    y = (xf - mu) * lax.rsqrt(var + LN_EPS) * g.astype(jnp.float32) + b.astype(jnp.float32)
    return y.astype(x.dtype)


def chunk_spatial_mix(v, w_s, b_s):
    bsz, length, _ = v.shape
    vc = v.reshape(bsz, length // CHUNK, CHUNK, A_GROUPS, A_GW)
    s = jnp.einsum('gpq,bnqgc->bnpgc', w_s, vc) + b_s.T[None, None, :, :, None]
    return s.reshape(bsz, length, A_WIDTH)


def mixer_a(h, w_in, ln_g, ln_b, w_s, b_s, w_out):
    u, v, z = jnp.split(h @ w_in, 3, axis=-1)
    u = jax.nn.gelu(u)
    v = layer_norm(jax.nn.gelu(v), ln_g, ln_b)
    y = u * chunk_spatial_mix(v, w_s, b_s)
    return (y * jax.nn.silu(z)) @ w_out


def axial_rope_angles(length):
    rows = length // GRID_W
    row = jnp.repeat(jnp.arange(rows), GRID_W).astype(jnp.float32)
    col = jnp.tile(jnp.arange(GRID_W), rows).astype(jnp.float32)
    n_freq = ROPE_AXIS_DIM // 2
    inv = ROPE_BASE ** (-jnp.arange(n_freq, dtype=jnp.float32) / n_freq)
    return row[:, None] * inv[None, :], col[:, None] * inv[None, :]


def rotate_axis(x, ang):
    x1, x2 = jnp.split(x, 2, axis=-1)
    cos = jnp.cos(ang)[None, :, None, :].astype(x.dtype)
    sin = jnp.sin(ang)[None, :, None, :].astype(x.dtype)
    return jnp.concatenate([x1 * cos - x2 * sin, x2 * cos + x1 * sin], axis=-1)


def apply_axial_rope(x, ang_row, ang_col):
    xr, xc = jnp.split(x, 2, axis=-1)
    return jnp.concatenate([rotate_axis(xr, ang_row), rotate_axis(xc, ang_col)], axis=-1)


def project_qkvz(h, w_in):
    bsz, length, _ = h.shape
    q, k, v, z = jnp.split(h @ w_in, [B_WIDTH, B_WIDTH + KV_WIDTH, B_WIDTH + 2 * KV_WIDTH], axis=-1)
    return (q.reshape(bsz, length, N_HEADS, HEAD_DIM),
            k.reshape(bsz, length, N_KV_HEADS, HEAD_DIM),
            v.reshape(bsz, length, N_KV_HEADS, HEAD_DIM), z)


def project_kv(h, w_in):
    bsz, length, _ = h.shape
    k, v = jnp.split(h @ w_in[:, B_WIDTH:B_WIDTH + 2 * KV_WIDTH], 2, axis=-1)
    return (k.reshape(bsz, length, N_KV_HEADS, HEAD_DIM),
            v.reshape(bsz, length, N_KV_HEADS, HEAD_DIM))


def windowed_attention_with_context(q, k, v, k_ctx, v_ctx, sink):
    bsz, length = q.shape[:2]
    n_ctx = k_ctx.shape[1]
    nb = length // ATT_BLOCK
    scale = HEAD_DIM ** -0.5
    qg = q.reshape(bsz, length, N_KV_HEADS, GQA_GROUP, HEAD_DIM)
    pad = ((0, 0), (WINDOW, WINDOW), (0, 0), (0, 0))
    k_pad = jnp.pad(k, pad)
    v_pad = jnp.pad(v, pad)
    sink_l = sink.astype(jnp.float32).reshape(1, N_KV_HEADS, GQA_GROUP, 1, 1)
    qi = jnp.arange(ATT_BLOCK)[:, None]
    kj = jnp.arange(BAND)[None, :]
    rel = kj - WINDOW - qi

    def block(i):
        start = i * ATT_BLOCK
        qb = lax.dynamic_slice_in_dim(qg, start, ATT_BLOCK, axis=1)
        kb = lax.dynamic_slice_in_dim(k_pad, start, BAND, axis=1)
        vb = lax.dynamic_slice_in_dim(v_pad, start, BAND, axis=1)
        key_pos = start - WINDOW + kj
        valid = (jnp.abs(rel) <= WINDOW) & (key_pos >= 0) & (key_pos < length)
        s_loc = jnp.einsum('bqhgd,bkhd->bhgqk', qb, kb).astype(jnp.float32) * scale
        s_loc = jnp.where(valid, s_loc, NEG_INF)
        s_ctx = jnp.einsum('bqhgd,bkhd->bhgqk', qb, k_ctx).astype(jnp.float32) * scale
        s_sink = jnp.broadcast_to(sink_l, s_loc.shape[:-1] + (1,))
        p = jax.nn.softmax(jnp.concatenate([s_loc, s_ctx, s_sink], axis=-1), axis=-1)
        p_loc = p[..., :BAND].astype(v.dtype)
        p_ctx = p[..., BAND:BAND + n_ctx].astype(v.dtype)
        return (jnp.einsum('bhgqk,bkhd->bqhgd', p_loc, vb)
                + jnp.einsum('bhgqk,bkhd->bqhgd', p_ctx, v_ctx))

    out = lax.map(block, jnp.arange(nb))
    return jnp.moveaxis(out, 0, 1).reshape(bsz, length, B_WIDTH)


def context_attention(q_ctx, k_ctx, v_ctx, sink):
    bsz, n_ctx = q_ctx.shape[:2]
    scale = HEAD_DIM ** -0.5
    qg = q_ctx.reshape(bsz, n_ctx, N_KV_HEADS, GQA_GROUP, HEAD_DIM)
    s = jnp.einsum('bqhgd,bkhd->bhgqk', qg, k_ctx).astype(jnp.float32) * scale
    s_sink = jnp.broadcast_to(sink.astype(jnp.float32).reshape(1, N_KV_HEADS, GQA_GROUP, 1, 1),
                              s.shape[:-1] + (1,))
    p = jax.nn.softmax(jnp.concatenate([s, s_sink], axis=-1), axis=-1)
    o = jnp.einsum('bhgqk,bkhd->bqhgd', p[..., :n_ctx].astype(v_ctx.dtype), v_ctx)
    return o.reshape(bsz, n_ctx, B_WIDTH)


def mixer_b(h, hc, w_in, sink, w_out, ang_row, ang_col, need_ctx_out):
    q, k, v, z = project_qkvz(h, w_in)
    q = apply_axial_rope(q, ang_row, ang_col)
    k = apply_axial_rope(k, ang_row, ang_col)
    if need_ctx_out:
        qc, kc, vc, zc = project_qkvz(hc, w_in)
    else:
        kc, vc = project_kv(hc, w_in)
    o = windowed_attention_with_context(q, k, v, kc, vc, sink)
    out = (o * jax.nn.silu(z)) @ w_out
    out_c = None
    if need_ctx_out:
        oc = context_attention(qc, kc, vc, sink)
        out_c = (oc * jax.nn.silu(zc)) @ w_out
    return out, out_c


def setup_inputs(seed: int = 0) -> dict:
    key = jax.random.key(seed)
    ks = jax.random.split(key, 20)
    n_a = (DEPTH + N_MIXERS - 1) // N_MIXERS
    n_b = DEPTH // N_MIXERS
    beta = (8.0 * DEPTH) ** -0.25
    nrm = jax.random.normal
    f32 = jnp.float32
    return {
        "x": nrm(ks[0], (BATCH, SEQ, D_MODEL), f32),
        "c": nrm(ks[1], (BATCH, D_MODEL), f32),
        "ctx": nrm(ks[2], (BATCH, CTX_LEN, D_MODEL), f32),
        "c_ctx": nrm(ks[3], (D_MODEL,), f32),
        "ada_w": nrm(ks[4], (DEPTH, D_MODEL, 3 * D_MODEL), f32) * (0.5 * D_MODEL ** -0.5),
        "ada_b": nrm(ks[5], (DEPTH, 3 * D_MODEL), f32) * 0.01,
        "ln_g": 1.0 + 0.05 * nrm(ks[6], (DEPTH, D_MODEL), f32),
        "ln_b": 0.01 * nrm(ks[7], (DEPTH, D_MODEL), f32),
        "a_w_in": nrm(ks[8], (n_a, D_MODEL, 3 * A_WIDTH), f32) * D_MODEL ** -0.5,
        "a_ln_g": 1.0 + 0.05 * nrm(ks[9], (n_a, A_WIDTH), f32),
        "a_ln_b": 0.01 * nrm(ks[10], (n_a, A_WIDTH), f32),
        "a_w_s": nrm(ks[11], (n_a, A_GROUPS, CHUNK, CHUNK), f32) * CHUNK ** -0.5,
        "a_b_s": 1.0 + 0.05 * nrm(ks[12], (n_a, A_GROUPS, CHUNK), f32),
        "a_w_out": nrm(ks[13], (n_a, A_WIDTH, D_MODEL), f32) * (A_WIDTH ** -0.5 * beta),
        "b_w_in": nrm(ks[14], (n_b, D_MODEL, 2 * B_WIDTH + 2 * KV_WIDTH), f32) * D_MODEL ** -0.5,
        "b_sink": 0.5 * nrm(ks[15], (n_b, N_HEADS), f32),
        "b_w_out": nrm(ks[16], (n_b, B_WIDTH, D_MODEL), f32) * (B_WIDTH ** -0.5 * beta),
    }


def reference(x, c, ctx, c_ctx, ada_w, ada_b, ln_g, ln_b,
              a_w_in, a_ln_g, a_ln_b, a_w_s, a_b_s, a_w_out,
              b_w_in, b_sink, b_w_out):
    alpha = (2.0 * DEPTH) ** 0.25
    length = x.shape[1]
    ang_row, ang_col = axial_rope_angles(length)
    cond = jax.nn.silu(c)
    cond_ctx = jax.nn.silu(c_ctx)
    for i in range(DEPTH):
        j = i // N_MIXERS
        need_ctx_out = i < DEPTH - 1
        shift, scale, gate = jnp.split(cond @ ada_w[i] + ada_b[i], 3, axis=-1)
        shift_c, scale_c, gate_c = jnp.split(cond_ctx @ ada_w[i] + ada_b[i], 3, axis=-1)
        h = x * (1.0 + scale[:, None, :]) + shift[:, None, :]
        hc = ctx * (1.0 + scale_c) + shift_c
        if i % N_MIXERS == 0:
            out = mixer_a(h, a_w_in[j], a_ln_g[j], a_ln_b[j], a_w_s[j], a_b_s[j], a_w_out[j])
            out_c = (mixer_a(hc, a_w_in[j], a_ln_g[j], a_ln_b[j], a_w_s[j], a_b_s[j], a_w_out[j])
                     if need_ctx_out else None)
        else:
            out, out_c = mixer_b(h, hc, b_w_in[j], b_sink[j], b_w_out[j], ang_row, ang_col, need_ctx_out)
        x = layer_norm(alpha * x + gate[:, None, :] * out, ln_g[i], ln_b[i])
        if need_ctx_out:
            ctx = layer_norm(alpha * ctx + gate_c * out_c, ln_g[i], ln_b[i])
    return x
```

```python
import functools
import math

import jax
import jax.numpy as jnp
from jax import lax
from jax.experimental import pallas as pl
from jax.experimental.pallas import tpu as pltpu

D_MODEL = 1024
DEPTH = 4
GRID_W = 64
N_MIXERS = 2
CHUNK = 128
A_WIDTH = 2 * D_MODEL
A_GROUPS = 8
A_GW = A_WIDTH // A_GROUPS
HEAD_DIM = 64
N_HEADS = D_MODEL // HEAD_DIM
N_KV_HEADS = N_HEADS // 4
B_WIDTH = N_HEADS * HEAD_DIM
KV_WIDTH = N_KV_HEADS * HEAD_DIM
WINDOW = 128
ATT_BLOCK = 128
ROPE_AXIS_DIM = HEAD_DIM // 2
ROPE_BASE = 10000.0
LN_EPS = 1e-5
NEG_INF = -1e30
ALPHA = (2.0 * DEPTH) ** 0.25

LANES = 128
MOD_ROWS = 8
VMEM_LIMIT = 48 * 1024 * 1024

BF16 = jnp.bfloat16
F32 = jnp.float32

_GELU_C0 = math.sqrt(2.0 / math.pi)
_GELU_C1 = _GELU_C0 * 0.044715


def _dot(a, b):
    return jnp.dot(a, b, preferred_element_type=F32)


def _dot_nt(a, b):
    return lax.dot_general(a, b, (((1,), (1,)), ((), ())), preferred_element_type=F32)


def _gelu(x):
    inner = x * (_GELU_C1 * (x * x) + _GELU_C0)
    hx = 0.5 * x
    return hx + hx * jnp.tanh(inner)


def _silu(z):
    hz = 0.5 * z
    return hz + hz * jnp.tanh(hz)


def _layer_norm(x, g, b):
    mu = jnp.mean(x, axis=-1, keepdims=True)
    xc = x - mu
    var = jnp.mean(xc * xc, axis=-1, keepdims=True)
    return xc * lax.rsqrt(var + LN_EPS) * g + b


def _const_spec(shape):
    nd = len(shape)
    return pl.BlockSpec(shape, lambda *_: (0,) * nd, pipeline_mode=pl.Buffered(1))


def _modulation_kernel(c_ref, w_ref, b_ref, o_ref):
    c = c_ref[...]
    cond = _silu(c).astype(BF16)
    o_ref[0] = _dot(cond, w_ref[0].astype(BF16)) + b_ref[0]


def _modulation(c_rows, ada_w, ada_b):
    tn = D_MODEL
    return pl.pallas_call(
        _modulation_kernel,
        out_shape=jax.ShapeDtypeStruct((DEPTH, MOD_ROWS, 3 * D_MODEL), F32),
        grid=(DEPTH, 3 * D_MODEL // tn),
        in_specs=[
            pl.BlockSpec((MOD_ROWS, D_MODEL), lambda i, j: (0, 0)),
            pl.BlockSpec((1, D_MODEL, tn), lambda i, j: (i, 0, j)),
            pl.BlockSpec((1, 1, tn), lambda i, j: (i, 0, j)),
        ],
        out_specs=pl.BlockSpec((1, MOD_ROWS, tn), lambda i, j: (i, 0, j)),
        compiler_params=pltpu.CompilerParams(
            dimension_semantics=("parallel", "parallel"), vmem_limit_bytes=VMEM_LIMIT),
        name="modulation",
    )(c_rows, ada_w, ada_b.reshape(DEPTH, 1, 3 * D_MODEL))


def _modulate(x, mod_ref):
    shift = mod_ref[0, 0:1, :]
    scale = mod_ref[0, 1:2, :]
    return x * (1.0 + scale) + shift


def _post_norm(x, out, mod_ref, g_ref, b_ref):
    gate = mod_ref[0, 2:3, :]
    return _layer_norm(ALPHA * x + gate * out, g_ref[...], b_ref[...])


def _layer_a_kernel(x_ref, mod_ref, w_in_ref, lng_ref, lnb_ref, ws_ref, bs_ref, w_out_ref,
                    g_ref, b_ref, o_ref, vn_ref, y_ref, *, tm):
    x = x_ref[0]
    h = _modulate(x, mod_ref).astype(BF16)
    v = _gelu(_dot(h, w_in_ref[:, A_WIDTH:2 * A_WIDTH]))
    vn_ref[...] = _layer_norm(v, lng_ref[...], lnb_ref[...]).astype(BF16)
    for g in range(A_GROUPS):
        cols = slice(g * A_GW, (g + 1) * A_GW)
        u = _gelu(_dot(h, w_in_ref[:, g * A_GW:(g + 1) * A_GW]))
        z = _dot(h, w_in_ref[:, 2 * A_WIDTH + g * A_GW:2 * A_WIDTH + (g + 1) * A_GW])
        uz = u * _silu(z)
        for c in range(tm // CHUNK):
            rows = slice(c * CHUNK, (c + 1) * CHUNK)
            s = _dot(ws_ref[g], vn_ref[rows, cols]) + bs_ref[:, cols]
            y_ref[rows, cols] = (uz[rows] * s).astype(BF16)
    out = _dot(y_ref[...], w_out_ref[...])
    o_ref[0] = _post_norm(x, out, mod_ref, g_ref, b_ref)


def _layer_a(x, mod, w_in, a_ln_g, a_ln_b, w_s, b_s_full, w_out, ln_g, ln_b, *, tm):
    bsz, length, _ = x.shape
    kernel = functools.partial(_layer_a_kernel, tm=tm)
    return pl.pallas_call(
        kernel,
        out_shape=jax.ShapeDtypeStruct(x.shape, F32),
        grid=(bsz, length // tm),
        in_specs=[
            pl.BlockSpec((1, tm, D_MODEL), lambda b, t: (b, t, 0)),
            pl.BlockSpec((1, 3, D_MODEL), lambda b, t: (b, 0, 0)),
            _const_spec((D_MODEL, 3 * A_WIDTH)),
            _const_spec((1, A_WIDTH)),
            _const_spec((1, A_WIDTH)),
            _const_spec((A_GROUPS, CHUNK, CHUNK)),
            _const_spec((CHUNK, A_WIDTH)),
            _const_spec((A_WIDTH, D_MODEL)),
            _const_spec((1, D_MODEL)),
            _const_spec((1, D_MODEL)),
        ],
        out_specs=pl.BlockSpec((1, tm, D_MODEL), lambda b, t: (b, t, 0)),
        scratch_shapes=[pltpu.VMEM((tm, A_WIDTH), BF16), pltpu.VMEM((tm, A_WIDTH), BF16)],
        compiler_params=pltpu.CompilerParams(
            dimension_semantics=("parallel", "parallel"), vmem_limit_bytes=VMEM_LIMIT),
        name="layer_a",
    )(x, mod, w_in, a_ln_g, a_ln_b, w_s, b_s_full, w_out, ln_g, ln_b)


def _swap_head_pairs(t):
    n = t.shape[1] // LANES
    return jnp.concatenate(
        [pltpu.roll(t[:, j * LANES:(j + 1) * LANES], HEAD_DIM, 1) for j in range(n)], axis=1)


def _rope(t, cos, sin_a, sin_b):
    n = t.shape[1] // LANES
    half = ROPE_AXIS_DIM // 2
    outs = []
    for j in range(n):
        blk = t[:, j * LANES:(j + 1) * LANES]
        up = pltpu.roll(blk, LANES - half, 1)
        dn = pltpu.roll(blk, half, 1)
        outs.append(blk * cos + up * sin_a + dn * sin_b)
    return jnp.concatenate(outs, axis=1)


def _project_b_kernel(x_ref, mod_ref, w_ref, cos_ref, sa_ref, sb_ref,
                      q_ref, k_ref, ksw_ref, v_ref, vsw_ref, gz_ref, *, rope):
    x = x_ref[0]
    h = _modulate(x, mod_ref).astype(BF16)
    q = _dot(h, w_ref[:, 0:B_WIDTH])
    k = _dot(h, w_ref[:, B_WIDTH:B_WIDTH + KV_WIDTH])
    v = _dot(h, w_ref[:, B_WIDTH + KV_WIDTH:B_WIDTH + 2 * KV_WIDTH])
    z = _dot(h, w_ref[:, B_WIDTH + 2 * KV_WIDTH:])
    if rope:
        cos, sa, sb = cos_ref[...], sa_ref[...], sb_ref[...]
        q = _rope(q, cos, sa, sb)
        k = _rope(k, cos, sa, sb)
    q_ref[0] = (q * (HEAD_DIM ** -0.5)).astype(BF16)
    k_ref[0] = k.astype(BF16)
    ksw_ref[0] = _swap_head_pairs(k).astype(BF16)
    v_ref[0] = v.astype(BF16)
    vsw_ref[0] = _swap_head_pairs(v).astype(BF16)
    gz_ref[0] = _silu(z).astype(BF16)


def _project_b(x, mod, w_in, tables, *, tm, rope):
    bsz, length, _ = x.shape
    cos, sa, sb = tables
    row_spec = lambda width: pl.BlockSpec((1, tm, width), lambda b, t: (b, t, 0))
    tab_spec = pl.BlockSpec((tm, LANES), lambda b, t: (t, 0))
    sds = lambda width: jax.ShapeDtypeStruct((bsz, length, width), BF16)
    kernel = functools.partial(_project_b_kernel, rope=rope)
    return pl.pallas_call(
        kernel,
        out_shape=(sds(B_WIDTH), sds(KV_WIDTH), sds(KV_WIDTH), sds(KV_WIDTH), sds(KV_WIDTH),
                   sds(B_WIDTH)),
        grid=(bsz, length // tm),
        in_specs=[
            row_spec(D_MODEL),
            pl.BlockSpec((1, 3, D_MODEL), lambda b, t: (b, 0, 0)),
            _const_spec((D_MODEL, 2 * B_WIDTH + 2 * KV_WIDTH)),
            tab_spec, tab_spec, tab_spec,
        ],
        out_specs=(row_spec(B_WIDTH), row_spec(KV_WIDTH), row_spec(KV_WIDTH), row_spec(KV_WIDTH),
                   row_spec(KV_WIDTH), row_spec(B_WIDTH)),
        compiler_params=pltpu.CompilerParams(
            dimension_semantics=("parallel", "parallel"), vmem_limit_bytes=VMEM_LIMIT),
        name="project_b_rope" if rope else "project_b",
    )(x, mod, w_in, cos, sa, sb)


def _attend_b_kernel(*refs, local, n_blocks):
    if local:
        (sink_ref, q_ref,
         kp_ref, kc_ref, kn_ref, kswp_ref, kswc_ref, kswn_ref,
         vp_ref, vc_ref, vn_ref, vswp_ref, vswc_ref, vswn_ref,
         kx_ref, kswx_ref, vx_ref, vswx_ref,
         gz_ref, x_ref, mod_ref, w_out_ref, g_ref, b_ref, mask_ref, o_ref, og_ref) = refs
        k_tiles = (kp_ref, kc_ref, kn_ref, kx_ref)
        ksw_tiles = (kswp_ref, kswc_ref, kswn_ref, kswx_ref)
        v_tiles = (vp_ref, vc_ref, vn_ref, vx_ref)
        vsw_tiles = (vswp_ref, vswc_ref, vswn_ref, vswx_ref)
    else:
        (sink_ref, q_ref, kx_ref, kswx_ref, vx_ref, vswx_ref,
         gz_ref, x_ref, mod_ref, w_out_ref, g_ref, b_ref, o_ref, og_ref) = refs
        k_tiles, ksw_tiles, v_tiles, vsw_tiles = (kx_ref,), (kswx_ref,), (vx_ref,), (vswx_ref,)

    tq = q_ref.shape[1]
    if local:
        i = pl.program_id(1)
        pen_prev = jnp.where(i == 0, NEG_INF, 0.0).astype(F32)
        pen_next = jnp.where(i == n_blocks - 1, NEG_INF, 0.0).astype(F32)
        bias_prev = mask_ref[:, 0:WINDOW] + pen_prev
        bias_next = mask_ref[:, WINDOW:2 * WINDOW] + pen_next
        bias_prev = jnp.concatenate([bias_prev, bias_prev], axis=0)
        bias_next = jnp.concatenate([bias_next, bias_next], axis=0)

    lane = lax.broadcasted_iota(jnp.int32, (1, LANES), 1)
    low = lane < HEAD_DIM
    row = lax.broadcasted_iota(jnp.int32, (2 * tq, 1), 0)
    top = row < tq

    for blk in range(KV_WIDTH // LANES):
        cols = slice(blk * LANES, (blk + 1) * LANES)
        cat = lambda tiles: jnp.concatenate([t[0, :, cols] for t in tiles], axis=0)
        k_cat, ksw_cat, v_cat, vsw_cat = cat(k_tiles), cat(ksw_tiles), cat(v_tiles), cat(vsw_tiles)
        zero = jnp.zeros_like(k_cat)
        for e in range(2):
            g = 2 * blk + e
            own_k, other_k = (k_cat, ksw_cat) if e == 0 else (ksw_cat, k_cat)
            own_v, other_v = (v_cat, vsw_cat) if e == 0 else (vsw_cat, v_cat)
            k_lo, k_hi = jnp.where(low, own_k, zero), jnp.where(low, zero, other_k)
            v_lo, v_hi = jnp.where(low, own_v, zero), jnp.where(low, zero, other_v)
            qp = jnp.concatenate([q_ref[0, :, (2 * g) * LANES:(2 * g + 1) * LANES],
                                  q_ref[0, :, (2 * g + 1) * LANES:(2 * g + 2) * LANES]], axis=0)
            acc = None
            inv = []
            for half, (k_h, v_h) in enumerate(((k_lo, v_lo), (k_hi, v_hi))):
                s = _dot_nt(qp, k_h)
                if local:
                    s = jnp.concatenate(
                        [s[:, 0:WINDOW] + bias_prev, s[:, WINDOW:2 * WINDOW],
                         s[:, 2 * WINDOW:3 * WINDOW] + bias_next, s[:, 3 * WINDOW:]], axis=1)
                sink = jnp.where(top, sink_ref[4 * g + half], sink_ref[4 * g + 2 + half])
                m = jnp.maximum(jnp.max(s, axis=-1, keepdims=True), sink)
                p = jnp.exp(s - m)
                denom = jnp.sum(p, axis=-1, keepdims=True) + jnp.exp(sink - m)
                inv.append(1.0 / denom)
                pv = _dot(p.astype(BF16), v_h)
                acc = pv if acc is None else acc + pv
            o2 = acc * jnp.where(low, inv[0], inv[1])
            for r in range(2):
                oc = slice((2 * g + r) * LANES, (2 * g + r + 1) * LANES)
                og_ref[:, oc] = (o2[r * tq:(r + 1) * tq] * gz_ref[0, :, oc].astype(F32)).astype(BF16)

    out = _dot(og_ref[...], w_out_ref[...])
    o_ref[0] = _post_norm(x_ref[0], out, mod_ref, g_ref, b_ref)


def _attend_b(q, k, ksw, v, vsw, kx, kswx, vx, vswx, gz, x, mod, w_out, sink, ln_g, ln_b, mask,
              *, local):
    bsz, length, _ = q.shape
    n_ctx = kx.shape[1]
    tq = ATT_BLOCK
    nb = length // tq
    q_spec = pl.BlockSpec((1, tq, B_WIDTH), lambda b, i: (b, i, 0))
    x_spec = pl.BlockSpec((1, tq, D_MODEL), lambda b, i: (b, i, 0))
    ctx_spec = pl.BlockSpec((1, n_ctx, KV_WIDTH), lambda b, i: (b, 0, 0))
    prev_spec = pl.BlockSpec((1, tq, KV_WIDTH), lambda b, i: (b, jnp.maximum(i - 1, 0), 0))
    cur_spec = pl.BlockSpec((1, tq, KV_WIDTH), lambda b, i: (b, i, 0))
    next_spec = pl.BlockSpec((1, tq, KV_WIDTH), lambda b, i: (b, jnp.minimum(i + 1, nb - 1), 0))
    in_specs = [pl.BlockSpec(memory_space=pltpu.SMEM), q_spec]
    args = [sink, q]
    if local:
        for t in (k, ksw, v, vsw):
            in_specs += [prev_spec, cur_spec, next_spec]
            args += [t, t, t]
    in_specs += [ctx_spec] * 4
    args += [kx, kswx, vx, vswx]
    in_specs += [q_spec, x_spec, pl.BlockSpec((1, 3, D_MODEL), lambda b, i: (b, 0, 0)),
                 _const_spec((B_WIDTH, D_MODEL)), _const_spec((1, D_MODEL)),
                 _const_spec((1, D_MODEL))]
    args += [gz, x, mod, w_out, ln_g, ln_b]
    if local:
        in_specs.append(_const_spec((ATT_BLOCK, 2 * WINDOW)))
        args.append(mask)
    kernel = functools.partial(_attend_b_kernel, local=local, n_blocks=nb)
    return pl.pallas_call(
        kernel,
        out_shape=jax.ShapeDtypeStruct(x.shape, F32),
        grid=(bsz, nb),
        in_specs=in_specs,
        out_specs=x_spec,
        scratch_shapes=[pltpu.VMEM((tq, B_WIDTH), BF16)],
        compiler_params=pltpu.CompilerParams(
            dimension_semantics=("parallel", "parallel"), vmem_limit_bytes=VMEM_LIMIT),
        name="attend_b_local" if local else "attend_b_ctx",
    )(*args)


def _rope_tables(length):
    rows = length // GRID_W
    row = jnp.repeat(jnp.arange(rows), GRID_W).astype(F32)
    col = jnp.tile(jnp.arange(GRID_W), rows).astype(F32)
    n_freq = ROPE_AXIS_DIM // 2
    inv = ROPE_BASE ** (-jnp.arange(n_freq, dtype=F32) / n_freq)
    ang_r, ang_c = row[:, None] * inv[None, :], col[:, None] * inv[None, :]
    cr, sr, cc, sc = jnp.cos(ang_r), jnp.sin(ang_r), jnp.cos(ang_c), jnp.sin(ang_c)
    zero = jnp.zeros_like(sr)
    reps = LANES // HEAD_DIM
    cos = jnp.tile(jnp.concatenate([cr, cr, cc, cc], axis=1), (1, reps))
    sin_a = jnp.tile(jnp.concatenate([-sr, zero, -sc, zero], axis=1), (1, reps))
    sin_b = jnp.tile(jnp.concatenate([zero, sr, zero, sc], axis=1), (1, reps))
    return cos, sin_a, sin_b


def _band_mask():
    qi = jnp.arange(ATT_BLOCK)[:, None]
    kj = jnp.arange(WINDOW)[None, :]
    prev = jnp.where(kj >= qi, 0.0, NEG_INF)
    nxt = jnp.where(kj <= qi, 0.0, NEG_INF)
    return jnp.concatenate([prev, nxt], axis=1).astype(F32)


def kernel(x, c, ctx, c_ctx, ada_w, ada_b, ln_g, ln_b, a_w_in, a_ln_g, a_ln_b, a_w_s, a_b_s,
           a_w_out, b_w_in, b_sink, b_w_out):
    bsz, length, _ = x.shape
    n_ctx = ctx.shape[1]
    assert bsz + 1 <= MOD_ROWS

    c_rows = jnp.zeros((MOD_ROWS, D_MODEL), F32).at[:bsz].set(c).at[bsz].set(c_ctx)
    mod_all = _modulation(c_rows, ada_w, ada_b)
    tables = _rope_tables(length)
    mask = _band_mask()

    for i in range(DEPTH):
        j = i // N_MIXERS
        need_ctx_out = i < DEPTH - 1
        mod = mod_all[i, :bsz].reshape(bsz, 3, D_MODEL)
        mod_c = jnp.broadcast_to(mod_all[i, bsz].reshape(1, 3, D_MODEL), (bsz, 3, D_MODEL))
        g_i, b_i = ln_g[i].reshape(1, D_MODEL), ln_b[i].reshape(1, D_MODEL)
        if i % N_MIXERS == 0:
            b_s_full = jnp.repeat(a_b_s[j].T, A_GW, axis=1)
            weights = (a_w_in[j].astype(BF16), a_ln_g[j].reshape(1, A_WIDTH),
                       a_ln_b[j].reshape(1, A_WIDTH), a_w_s[j].astype(BF16), b_s_full,
                       a_w_out[j].astype(BF16), g_i, b_i)
            x_new = _layer_a(x, mod, *weights, tm=256)
            if need_ctx_out:
                ctx = _layer_a(ctx, mod_c, *weights, tm=n_ctx)
            x = x_new
        else:
            w_in = b_w_in[j].astype(BF16)
            w_out = b_w_out[j].astype(BF16)
            sink = b_sink[j].astype(F32)
            q, k, ksw, v, vsw, gz = _project_b(x, mod, w_in, tables, tm=256, rope=True)
            qc, kc, kswc, vc, vswc, gzc = _project_b(ctx, mod_c, w_in, tables, tm=n_ctx,
                                                     rope=False)
            x = _attend_b(q, k, ksw, v, vsw, kc, kswc, vc, vswc, gz, x, mod, w_out, sink,
                          g_i, b_i, mask, local=True)
            if need_ctx_out:
                ctx = _attend_b(qc, None, None, None, None, kc, kswc, vc, vswc, gzc, ctx, mod_c,
                                w_out, sink, g_i, b_i, None, local=False)
    return x
```

```python
import functools
import math

import jax
import jax.numpy as jnp
from jax import lax
from jax.experimental import pallas as pl
from jax.experimental.pallas import tpu as pltpu

D_MODEL = 1024
DEPTH = 4
GRID_W = 64
N_MIXERS = 2
CHUNK = 128
A_WIDTH = 2 * D_MODEL
A_GROUPS = 8
A_GW = A_WIDTH // A_GROUPS
HEAD_DIM = 64
N_HEADS = D_MODEL // HEAD_DIM
N_KV_HEADS = N_HEADS // 4
B_WIDTH = N_HEADS * HEAD_DIM
KV_WIDTH = N_KV_HEADS * HEAD_DIM
WINDOW = 128
ATT_BLOCK = 128
ATT_SUB = 2
ROPE_AXIS_DIM = HEAD_DIM // 2
ROPE_BASE = 10000.0
LN_EPS = 1e-5
NEG_INF = -1e30
ALPHA = (2.0 * DEPTH) ** 0.25

LANES = 128
MOD_ROWS = 8
VMEM_LIMIT = 48 * 1024 * 1024

BF16 = jnp.bfloat16
F32 = jnp.float32

_GELU_C0 = math.sqrt(2.0 / math.pi)
_GELU_C1 = _GELU_C0 * 0.044715


def _dot(a, b):
    return jnp.dot(a, b, preferred_element_type=F32)


def _dot_nt(a, b):
    return lax.dot_general(a, b, (((1,), (1,)), ((), ())), preferred_element_type=F32)


def _gelu(x):
    inner = x * (_GELU_C1 * (x * x) + _GELU_C0)
    hx = 0.5 * x
    return hx + hx * jnp.tanh(inner)


def _silu(z):
    hz = 0.5 * z
    return hz + hz * jnp.tanh(hz)


def _layer_norm(x, g, b):
    mu = jnp.mean(x, axis=-1, keepdims=True)
    xc = x - mu
    var = jnp.mean(xc * xc, axis=-1, keepdims=True)
    return xc * lax.rsqrt(var + LN_EPS) * g + b


def _const_spec(shape):
    nd = len(shape)
    return pl.BlockSpec(shape, lambda *_: (0,) * nd, pipeline_mode=pl.Buffered(1))


def _modulation_kernel(c_ref, w_ref, b_ref, o_ref):
    c = c_ref[...]
    cond = _silu(c).astype(BF16)
    o_ref[0] = _dot(cond, w_ref[0].astype(BF16)) + b_ref[0]


def _modulation(c_rows, ada_w, ada_b):
    tn = D_MODEL
    return pl.pallas_call(
        _modulation_kernel,
        out_shape=jax.ShapeDtypeStruct((DEPTH, MOD_ROWS, 3 * D_MODEL), F32),
        grid=(DEPTH, 3 * D_MODEL // tn),
        in_specs=[
            pl.BlockSpec((MOD_ROWS, D_MODEL), lambda i, j: (0, 0)),
            pl.BlockSpec((1, D_MODEL, tn), lambda i, j: (i, 0, j)),
            pl.BlockSpec((1, 1, tn), lambda i, j: (i, 0, j)),
        ],
        out_specs=pl.BlockSpec((1, MOD_ROWS, tn), lambda i, j: (i, 0, j)),
        compiler_params=pltpu.CompilerParams(
            dimension_semantics=("parallel", "parallel"), vmem_limit_bytes=VMEM_LIMIT),
        name="modulation",
    )(c_rows, ada_w, ada_b.reshape(DEPTH, 1, 3 * D_MODEL))


def _modulate(x, mod_ref):
    shift = mod_ref[0, 0:1, :]
    scale = mod_ref[0, 1:2, :]
    return x * (1.0 + scale) + shift


def _post_norm(x, out, mod_ref, g_ref, b_ref):
    gate = mod_ref[0, 2:3, :]
    return _layer_norm(ALPHA * x + gate * out, g_ref[...], b_ref[...])


def _layer_a_kernel(x_ref, mod_ref, w_in_ref, lng_ref, lnb_ref, ws_ref, bs_ref, w_out_ref,
                    g_ref, b_ref, o_ref, vn_ref, y_ref, *, tm):
    x = x_ref[0]
    h = _modulate(x, mod_ref).astype(BF16)
    v = _gelu(_dot(h, w_in_ref[:, A_WIDTH:2 * A_WIDTH]))
    vn_ref[...] = _layer_norm(v, lng_ref[...], lnb_ref[...]).astype(BF16)
    for g in range(A_GROUPS):
        cols = slice(g * A_GW, (g + 1) * A_GW)
        u = _gelu(_dot(h, w_in_ref[:, g * A_GW:(g + 1) * A_GW]))
        z = _dot(h, w_in_ref[:, 2 * A_WIDTH + g * A_GW:2 * A_WIDTH + (g + 1) * A_GW])
        uz = u * _silu(z)
        for c in range(tm // CHUNK):
            rows = slice(c * CHUNK, (c + 1) * CHUNK)
            s = _dot(ws_ref[g], vn_ref[rows, cols]) + bs_ref[:, cols]
            y_ref[rows, cols] = (uz[rows] * s).astype(BF16)
    out = _dot(y_ref[...], w_out_ref[...])
    o_ref[0] = _post_norm(x, out, mod_ref, g_ref, b_ref)


def _layer_a(x, mod, w_in, a_ln_g, a_ln_b, w_s, b_s_full, w_out, ln_g, ln_b, *, tm):
    bsz, length, _ = x.shape
    kernel = functools.partial(_layer_a_kernel, tm=tm)
    return pl.pallas_call(
        kernel,
        out_shape=jax.ShapeDtypeStruct(x.shape, F32),
        grid=(bsz, length // tm),
        in_specs=[
            pl.BlockSpec((1, tm, D_MODEL), lambda b, t: (b, t, 0)),
            pl.BlockSpec((1, 3, D_MODEL), lambda b, t: (b, 0, 0)),
            _const_spec((D_MODEL, 3 * A_WIDTH)),
            _const_spec((1, A_WIDTH)),
            _const_spec((1, A_WIDTH)),
            _const_spec((A_GROUPS, CHUNK, CHUNK)),
            _const_spec((CHUNK, A_WIDTH)),
            _const_spec((A_WIDTH, D_MODEL)),
            _const_spec((1, D_MODEL)),
            _const_spec((1, D_MODEL)),
        ],
        out_specs=pl.BlockSpec((1, tm, D_MODEL), lambda b, t: (b, t, 0)),
        scratch_shapes=[pltpu.VMEM((tm, A_WIDTH), BF16), pltpu.VMEM((tm, A_WIDTH), BF16)],
        compiler_params=pltpu.CompilerParams(
            dimension_semantics=("parallel", "parallel"), vmem_limit_bytes=VMEM_LIMIT),
        name="layer_a",
    )(x, mod, w_in, a_ln_g, a_ln_b, w_s, b_s_full, w_out, ln_g, ln_b)


def _rope(t, cos, sin_a, sin_b):
    n = t.shape[1] // LANES
    half = ROPE_AXIS_DIM // 2
    outs = []
    for j in range(n):
        blk = t[:, j * LANES:(j + 1) * LANES]
        up = pltpu.roll(blk, LANES - half, 1)
        dn = pltpu.roll(blk, half, 1)
        outs.append(blk * cos + up * sin_a + dn * sin_b)
    return jnp.concatenate(outs, axis=1)


def _project_b_kernel(x_ref, mod_ref, w_ref, cos_ref, sa_ref, sb_ref,
                      q_ref, kk_ref, vt_ref, gz_ref, *, rope):
    x = x_ref[0]
    h = _modulate(x, mod_ref).astype(BF16)
    q = _dot(h, w_ref[:, 0:B_WIDTH])
    k = _dot(h, w_ref[:, B_WIDTH:B_WIDTH + KV_WIDTH])
    v = _dot(h, w_ref[:, B_WIDTH + KV_WIDTH:B_WIDTH + 2 * KV_WIDTH])
    z = _dot(h, w_ref[:, B_WIDTH + 2 * KV_WIDTH:])
    if rope:
        cos, sa, sb = cos_ref[...], sa_ref[...], sb_ref[...]
        q = _rope(q, cos, sa, sb)
        k = _rope(k, cos, sa, sb)
    q = (q * (HEAD_DIM ** -0.5)).astype(BF16)
    gz = _silu(z).astype(BF16)
    v_t = v.T.astype(BF16)
    low = lax.broadcasted_iota(jnp.int32, (1, LANES), 1) < HEAD_DIM
    group_w = B_WIDTH // N_KV_HEADS
    for g in range(N_KV_HEADS):
        q_ref[0, g] = q[:, g * group_w:(g + 1) * group_w]
        gz_ref[0, g] = gz[:, g * group_w:(g + 1) * group_w]
        vt_ref[0, g] = v_t[g * HEAD_DIM:(g + 1) * HEAD_DIM, :]
    for j in range(KV_WIDTH // LANES):
        blk = k[:, j * LANES:(j + 1) * LANES]
        swapped = pltpu.roll(blk, HEAD_DIM, 1)
        kk_ref[0, 2 * j] = jnp.where(low, blk, swapped).astype(BF16)
        kk_ref[0, 2 * j + 1] = jnp.where(low, swapped, blk).astype(BF16)


def _project_b(x, mod, w_in, tables, *, tm, rope):
    bsz, length, _ = x.shape
    cos, sa, sb = tables
    group_w = B_WIDTH // N_KV_HEADS
    grouped = lambda width: jax.ShapeDtypeStruct((bsz, N_KV_HEADS, length, width), BF16)
    grouped_spec = lambda width: pl.BlockSpec((1, N_KV_HEADS, tm, width),
                                              lambda b, t: (b, 0, t, 0))
    tab_spec = pl.BlockSpec((tm, LANES), lambda b, t: (t, 0))
    kernel = functools.partial(_project_b_kernel, rope=rope)
    return pl.pallas_call(
        kernel,
        out_shape=(grouped(group_w), grouped(LANES),
                   jax.ShapeDtypeStruct((bsz, N_KV_HEADS, HEAD_DIM, length), BF16),
                   grouped(group_w)),
        grid=(bsz, length // tm),
        in_specs=[
            pl.BlockSpec((1, tm, D_MODEL), lambda b, t: (b, t, 0)),
            pl.BlockSpec((1, 3, D_MODEL), lambda b, t: (b, 0, 0)),
            _const_spec((D_MODEL, 2 * B_WIDTH + 2 * KV_WIDTH)),
            tab_spec, tab_spec, tab_spec,
        ],
        out_specs=(grouped_spec(group_w), grouped_spec(LANES),
                   pl.BlockSpec((1, N_KV_HEADS, HEAD_DIM, tm), lambda b, t: (b, 0, 0, t)),
                   grouped_spec(group_w)),
        compiler_params=pltpu.CompilerParams(
            dimension_semantics=("parallel", "parallel"), vmem_limit_bytes=VMEM_LIMIT),
        name="project_b_rope" if rope else "project_b",
    )(x, mod, w_in, cos, sa, sb)


def _attend_b_kernel(*refs, local, n_steps):
    if local:
        (sink_ref, q_ref, *tile_refs, kx_ref, vtx_ref, gz_ref, x_ref, mod_ref, w_out_ref,
         g_ref, b_ref, mask_ref, o_ref, sx_ref, m_ref, ot_ref, og_ref, sl_ref) = refs
        k_refs, vt_refs = tile_refs[:ATT_SUB + 2], tile_refs[ATT_SUB + 2:]
    else:
        (sink_ref, q_ref, kx_ref, vtx_ref, gz_ref, x_ref, mod_ref, w_out_ref, g_ref, b_ref,
         o_ref, sx_ref, m_ref, ot_ref, og_ref) = refs

    tq = ATT_BLOCK
    low = lax.broadcasted_iota(jnp.int32, (1, LANES), 1) < HEAD_DIM
    left = lax.broadcasted_iota(jnp.int32, (1, 2 * tq), 1) < tq
    if local:
        i = pl.program_id(1)
        first = jnp.where(i == 0, NEG_INF, 0.0).astype(F32)
        last = jnp.where(i == n_steps - 1, NEG_INF, 0.0).astype(F32)
        bias_prev = [mask_ref[0:WINDOW, :] + (first if sub == 0 else 0.0)
                     for sub in range(ATT_SUB)]
        bias_next = [mask_ref[WINDOW:2 * WINDOW, :] + (last if sub == ATT_SUB - 1 else 0.0)
                     for sub in range(ATT_SUB)]

    def sink_row(g, half):
        return jnp.where(left, sink_ref[4 * g + half], sink_ref[4 * g + 2 + half])

    def scores(g, slot):
        kx = kx_ref[0, g]
        for sub in range(ATT_SUB):
            rows = slice(sub * tq, (sub + 1) * tq)
            qp = jnp.concatenate([q_ref[0, g, rows, 0:LANES], q_ref[0, g, rows, LANES:2 * LANES]],
                                 axis=0)
            if local:
                kl = jnp.concatenate([k_refs[sub + t][0, g] for t in range(3)], axis=0)
            for half in range(2):
                qh = jnp.where(low, qp, jnp.zeros_like(qp)) if half == 0 else \
                    jnp.where(low, jnp.zeros_like(qp), qp)
                sx = _dot_nt(kx, qh)
                sx_ref[slot, sub, half] = sx
                m = jnp.max(sx, axis=0, keepdims=True)
                if local:
                    sl = _dot_nt(kl, qh)
                    sl = jnp.concatenate(
                        [sl[0:WINDOW] + bias_prev[sub], sl[WINDOW:2 * WINDOW],
                         sl[2 * WINDOW:3 * WINDOW] + bias_next[sub]], axis=0)
                    sl_ref[slot, sub, half] = sl
                    m = jnp.maximum(m, jnp.max(sl, axis=0, keepdims=True))
                m_ref[slot, sub, half] = jnp.maximum(m, sink_row(g, half))

    def attend(g, slot):
        vtx = vtx_ref[0, g]
        for sub in range(ATT_SUB):
            if local:
                vt = jnp.concatenate([vt_refs[sub + t][0, g] for t in range(3)], axis=1)
            for half in range(2):
                m = m_ref[slot, sub, half]
                px = jnp.exp(sx_ref[slot, sub, half] - m)
                denom = jnp.sum(px, axis=0, keepdims=True) + jnp.exp(sink_row(g, half) - m)
                o = _dot(vtx, px.astype(BF16))
                if local:
                    pl_ = jnp.exp(sl_ref[slot, sub, half] - m)
                    denom = denom + jnp.sum(pl_, axis=0, keepdims=True)
                    o = o + _dot(vt, pl_.astype(BF16))
                ot_ref[g, sub, half] = o * (1.0 / denom)

    scores(0, 0)

    def body(g, carry):
        slot = jnp.bitwise_and(g, 1)
        attend(g, slot)
        scores(g + 1, 1 - slot)
        return carry

    lax.fori_loop(0, N_KV_HEADS - 1, body, 0)
    attend(N_KV_HEADS - 1, (N_KV_HEADS - 1) % 2)

    out = None
    for g in range(N_KV_HEADS):
        for sub in range(ATT_SUB):
            rows = slice(sub * tq, (sub + 1) * tq)
            o_t = jnp.concatenate([ot_ref[g, sub, 0], ot_ref[g, sub, 1]], axis=0)
            for r in range(2):
                oc = slice(r * LANES, (r + 1) * LANES)
                o_blk = o_t[:, r * tq:(r + 1) * tq].T
                og_ref[g, rows, oc] = (o_blk * gz_ref[0, g, rows, oc].astype(F32)).astype(BF16)
        part = _dot(og_ref[g], w_out_ref[g])
        out = part if out is None else out + part
    o_ref[0] = _post_norm(x_ref[0], out, mod_ref, g_ref, b_ref)


def _attend_b(q, kk, vt, kkx, vtx, gz, x, mod, w_out, sink, ln_g, ln_b, mask, *, local):
    bsz, _, length, group_w = q.shape
    n_ctx = kkx.shape[2]
    tq = ATT_BLOCK
    rows = ATT_SUB * tq
    n_steps = length // rows
    n_tiles = length // tq
    grouped_spec = pl.BlockSpec((1, N_KV_HEADS, rows, group_w), lambda b, i: (b, 0, i, 0))
    x_spec = pl.BlockSpec((1, rows, D_MODEL), lambda b, i: (b, i, 0))
    tile = lambda t: (lambda i: jnp.clip(ATT_SUB * i + t - 1, 0, n_tiles - 1))
    k_specs = [pl.BlockSpec((1, N_KV_HEADS, tq, LANES), lambda b, i, f=tile(t): (b, 0, f(i), 0))
               for t in range(ATT_SUB + 2)]
    vt_specs = [pl.BlockSpec((1, N_KV_HEADS, HEAD_DIM, tq),
                             lambda b, i, f=tile(t): (b, 0, 0, f(i))) for t in range(ATT_SUB + 2)]
    in_specs = [pl.BlockSpec(memory_space=pltpu.SMEM), grouped_spec]
    args = [sink, q]
    if local:
        in_specs += k_specs + vt_specs
        args += [kk] * (ATT_SUB + 2) + [vt] * (ATT_SUB + 2)
    in_specs += [pl.BlockSpec((1, N_KV_HEADS, n_ctx, LANES), lambda b, i: (b, 0, 0, 0)),
                 pl.BlockSpec((1, N_KV_HEADS, HEAD_DIM, n_ctx), lambda b, i: (b, 0, 0, 0))]
    args += [kkx, vtx]
    in_specs += [grouped_spec, x_spec, pl.BlockSpec((1, 3, D_MODEL), lambda b, i: (b, 0, 0)),
                 _const_spec((N_KV_HEADS, group_w, D_MODEL)), _const_spec((1, D_MODEL)),
                 _const_spec((1, D_MODEL))]
    args += [gz, x, mod, w_out.reshape(N_KV_HEADS, group_w, D_MODEL), ln_g, ln_b]
    scratch = [pltpu.VMEM((2, ATT_SUB, 2, n_ctx, 2 * tq), F32),
               pltpu.VMEM((2, ATT_SUB, 2, 1, 2 * tq), F32),
               pltpu.VMEM((N_KV_HEADS, ATT_SUB, 2, HEAD_DIM, 2 * tq), F32),
               pltpu.VMEM((N_KV_HEADS, rows, group_w), BF16)]
    if local:
        in_specs.append(_const_spec((2 * WINDOW, 2 * ATT_BLOCK)))
        args.append(mask)
        scratch.append(pltpu.VMEM((2, ATT_SUB, 2, 3 * WINDOW, 2 * tq), F32))
    kernel = functools.partial(_attend_b_kernel, local=local, n_steps=n_steps)
    return pl.pallas_call(
        kernel,
        out_shape=jax.ShapeDtypeStruct(x.shape, F32),
        grid=(bsz, n_steps),
        in_specs=in_specs,
        out_specs=x_spec,
        scratch_shapes=scratch,
        compiler_params=pltpu.CompilerParams(
            dimension_semantics=("parallel", "parallel"), vmem_limit_bytes=VMEM_LIMIT),
        name="attend_b_local" if local else "attend_b_ctx",
    )(*args)


def _rope_tables(length):
    rows = length // GRID_W
    row = jnp.repeat(jnp.arange(rows), GRID_W).astype(F32)
    col = jnp.tile(jnp.arange(GRID_W), rows).astype(F32)
    n_freq = ROPE_AXIS_DIM // 2
    inv = ROPE_BASE ** (-jnp.arange(n_freq, dtype=F32) / n_freq)
    ang_r, ang_c = row[:, None] * inv[None, :], col[:, None] * inv[None, :]
    cr, sr, cc, sc = jnp.cos(ang_r), jnp.sin(ang_r), jnp.cos(ang_c), jnp.sin(ang_c)
    zero = jnp.zeros_like(sr)
    reps = LANES // HEAD_DIM
    cos = jnp.tile(jnp.concatenate([cr, cr, cc, cc], axis=1), (1, reps))
    sin_a = jnp.tile(jnp.concatenate([-sr, zero, -sc, zero], axis=1), (1, reps))
    sin_b = jnp.tile(jnp.concatenate([zero, sr, zero, sc], axis=1), (1, reps))
    return cos, sin_a, sin_b


def _band_mask():
    kj = jnp.arange(WINDOW)[:, None]
    qi = jnp.arange(ATT_BLOCK)[None, :]
    prev = jnp.where(kj >= qi, 0.0, NEG_INF)
    nxt = jnp.where(kj <= qi, 0.0, NEG_INF)
    return jnp.tile(jnp.concatenate([prev, nxt], axis=0), (1, 2)).astype(F32)


def kernel(x, c, ctx, c_ctx, ada_w, ada_b, ln_g, ln_b, a_w_in, a_ln_g, a_ln_b, a_w_s, a_b_s,
           a_w_out, b_w_in, b_sink, b_w_out):
    bsz, length, _ = x.shape
    n_ctx = ctx.shape[1]
    assert bsz + 1 <= MOD_ROWS

    c_rows = jnp.zeros((MOD_ROWS, D_MODEL), F32).at[:bsz].set(c).at[bsz].set(c_ctx)
    mod_all = _modulation(c_rows, ada_w, ada_b)
    tables = _rope_tables(length)
    mask = _band_mask()

    for i in range(DEPTH):
        j = i // N_MIXERS
        need_ctx_out = i < DEPTH - 1
        mod = mod_all[i, :bsz].reshape(bsz, 3, D_MODEL)
        mod_c = jnp.broadcast_to(mod_all[i, bsz].reshape(1, 3, D_MODEL), (bsz, 3, D_MODEL))
        g_i, b_i = ln_g[i].reshape(1, D_MODEL), ln_b[i].reshape(1, D_MODEL)
        if i % N_MIXERS == 0:
            b_s_full = jnp.repeat(a_b_s[j].T, A_GW, axis=1)
            weights = (a_w_in[j].astype(BF16), a_ln_g[j].reshape(1, A_WIDTH),
                       a_ln_b[j].reshape(1, A_WIDTH), a_w_s[j].astype(BF16), b_s_full,
                       a_w_out[j].astype(BF16), g_i, b_i)
            x_new = _layer_a(x, mod, *weights, tm=256)
            if need_ctx_out:
                ctx = _layer_a(ctx, mod_c, *weights, tm=n_ctx)
            x = x_new
        else:
            w_in = b_w_in[j].astype(BF16)
            w_out = b_w_out[j].astype(BF16)
            sink = b_sink[j].astype(F32)
            q, kk, vt, gz = _project_b(x, mod, w_in, tables, tm=256, rope=True)
            qc, kkc, vtc, gzc = _project_b(ctx, mod_c, w_in, tables, tm=n_ctx, rope=False)
            x = _attend_b(q, kk, vt, kkc, vtc, gz, x, mod, w_out, sink, g_i, b_i, mask,
                          local=True)
            if need_ctx_out:
                ctx = _attend_b(qc, None, None, kkc, vtc, gzc, ctx, mod_c, w_out, sink, g_i, b_i,
                                None, local=False)
    return x
```

```python
import functools
import math

import jax
import jax.numpy as jnp
from jax import lax
from jax.experimental import pallas as pl
from jax.experimental.pallas import tpu as pltpu

D_MODEL = 1024
DEPTH = 4
GRID_W = 64
N_MIXERS = 2
CHUNK = 128
A_WIDTH = 2 * D_MODEL
A_GROUPS = 8
A_GW = A_WIDTH // A_GROUPS
HEAD_DIM = 64
N_HEADS = D_MODEL // HEAD_DIM
N_KV_HEADS = N_HEADS // 4
B_WIDTH = N_HEADS * HEAD_DIM
KV_WIDTH = N_KV_HEADS * HEAD_DIM
WINDOW = 128
ATT_BLOCK = 128
ATT_SUB = 4
ROPE_AXIS_DIM = HEAD_DIM // 2
ROPE_BASE = 10000.0
LN_EPS = 1e-5
NEG_INF = -1e30
ALPHA = (2.0 * DEPTH) ** 0.25

LANES = 128
MOD_ROWS = 8
VMEM_LIMIT = 48 * 1024 * 1024

BF16 = jnp.bfloat16
F32 = jnp.float32

LOG2E = math.log2(math.e)
Q_SCALE = HEAD_DIM ** -0.5 * LOG2E
ONES_ROWS = 16

_GELU_C0 = math.sqrt(2.0 / math.pi)
_GELU_C1 = _GELU_C0 * 0.044715


def _dot(a, b):
    return jnp.dot(a, b, preferred_element_type=F32)


def _dot_nt(a, b):
    return lax.dot_general(a, b, (((1,), (1,)), ((), ())), preferred_element_type=F32)


def _gelu(x):
    inner = x * (_GELU_C1 * (x * x) + _GELU_C0)
    hx = 0.5 * x
    return hx + hx * jnp.tanh(inner)


def _silu(z):
    hz = 0.5 * z
    return hz + hz * jnp.tanh(hz)


def _layer_norm(x, g, b):
    mu = jnp.mean(x, axis=-1, keepdims=True)
    xc = x - mu
    var = jnp.mean(xc * xc, axis=-1, keepdims=True)
    return xc * lax.rsqrt(var + LN_EPS) * g + b


def _const_spec(shape):
    nd = len(shape)
    return pl.BlockSpec(shape, lambda *_: (0,) * nd, pipeline_mode=pl.Buffered(1))


def _modulation_kernel(c_ref, w_ref, b_ref, o_ref):
    c = c_ref[...]
    cond = _silu(c).astype(BF16)
    o_ref[0] = _dot(cond, w_ref[0].astype(BF16)) + b_ref[0]


def _modulation(c_rows, ada_w, ada_b):
    tn = D_MODEL
    return pl.pallas_call(
        _modulation_kernel,
        out_shape=jax.ShapeDtypeStruct((DEPTH, MOD_ROWS, 3 * D_MODEL), F32),
        grid=(DEPTH, 3 * D_MODEL // tn),
        in_specs=[
            pl.BlockSpec((MOD_ROWS, D_MODEL), lambda i, j: (0, 0)),
            pl.BlockSpec((1, D_MODEL, tn), lambda i, j: (i, 0, j)),
            pl.BlockSpec((1, 1, tn), lambda i, j: (i, 0, j)),
        ],
        out_specs=pl.BlockSpec((1, MOD_ROWS, tn), lambda i, j: (i, 0, j)),
        compiler_params=pltpu.CompilerParams(
            dimension_semantics=("parallel", "parallel"), vmem_limit_bytes=VMEM_LIMIT),
        name="modulation",
    )(c_rows, ada_w, ada_b.reshape(DEPTH, 1, 3 * D_MODEL))


def _modulate(x, mod_ref):
    shift = mod_ref[0, 0:1, :]
    scale = mod_ref[0, 1:2, :]
    return x * (1.0 + scale) + shift


def _post_norm(x, out, mod_ref, g_ref, b_ref):
    gate = mod_ref[0, 2:3, :]
    return _layer_norm(ALPHA * x + gate * out, g_ref[...], b_ref[...])


def _layer_a_kernel(x_ref, mod_ref, w_in_ref, lng_ref, lnb_ref, ws_ref, bs_ref, w_out_ref,
                    g_ref, b_ref, o_ref, vn_ref, y_ref, *, tm):
    x = x_ref[0]
    h = _modulate(x, mod_ref).astype(BF16)
    v = _gelu(_dot(h, w_in_ref[:, A_WIDTH:2 * A_WIDTH]))
    vn_ref[...] = _layer_norm(v, lng_ref[...], lnb_ref[...]).astype(BF16)
    for g in range(A_GROUPS):
        cols = slice(g * A_GW, (g + 1) * A_GW)
        u = _gelu(_dot(h, w_in_ref[:, g * A_GW:(g + 1) * A_GW]))
        z = _dot(h, w_in_ref[:, 2 * A_WIDTH + g * A_GW:2 * A_WIDTH + (g + 1) * A_GW])
        uz = u * _silu(z)
        for c in range(tm // CHUNK):
            rows = slice(c * CHUNK, (c + 1) * CHUNK)
            s = _dot(ws_ref[g], vn_ref[rows, cols]) + bs_ref[:, cols]
            y_ref[rows, cols] = (uz[rows] * s).astype(BF16)
    out = _dot(y_ref[...], w_out_ref[...])
    o_ref[0] = _post_norm(x, out, mod_ref, g_ref, b_ref)


def _layer_a(x, mod, w_in, a_ln_g, a_ln_b, w_s, b_s_full, w_out, ln_g, ln_b, *, tm):
    bsz, length, _ = x.shape
    kernel = functools.partial(_layer_a_kernel, tm=tm)
    return pl.pallas_call(
        kernel,
        out_shape=jax.ShapeDtypeStruct(x.shape, F32),
        grid=(bsz, length // tm),
        in_specs=[
            pl.BlockSpec((1, tm, D_MODEL), lambda b, t: (b, t, 0)),
            pl.BlockSpec((1, 3, D_MODEL), lambda b, t: (b, 0, 0)),
            _const_spec((D_MODEL, 3 * A_WIDTH)),
            _const_spec((1, A_WIDTH)),
            _const_spec((1, A_WIDTH)),
            _const_spec((A_GROUPS, CHUNK, CHUNK)),
            _const_spec((CHUNK, A_WIDTH)),
            _const_spec((A_WIDTH, D_MODEL)),
            _const_spec((1, D_MODEL)),
            _const_spec((1, D_MODEL)),
        ],
        out_specs=pl.BlockSpec((1, tm, D_MODEL), lambda b, t: (b, t, 0)),
        scratch_shapes=[pltpu.VMEM((tm, A_WIDTH), BF16), pltpu.VMEM((tm, A_WIDTH), BF16)],
        compiler_params=pltpu.CompilerParams(
            dimension_semantics=("parallel", "parallel"), vmem_limit_bytes=VMEM_LIMIT),
        name="layer_a",
    )(x, mod, w_in, a_ln_g, a_ln_b, w_s, b_s_full, w_out, ln_g, ln_b)


def _rope(t, cos, sin_a, sin_b):
    n = t.shape[1] // LANES
    half = ROPE_AXIS_DIM // 2
    outs = []
    for j in range(n):
        blk = t[:, j * LANES:(j + 1) * LANES]
        up = pltpu.roll(blk, LANES - half, 1)
        dn = pltpu.roll(blk, half, 1)
        outs.append(blk * cos + up * sin_a + dn * sin_b)
    return jnp.concatenate(outs, axis=1)


def _project_b_kernel(x_ref, mod_ref, w_ref, cos_ref, sa_ref, sb_ref,
                      q_ref, kk_ref, vt_ref, gz_ref, *, rope):
    x = x_ref[0]
    h = _modulate(x, mod_ref).astype(BF16)
    q = _dot(h, w_ref[:, 0:B_WIDTH])
    k = _dot(h, w_ref[:, B_WIDTH:B_WIDTH + KV_WIDTH])
    v = _dot(h, w_ref[:, B_WIDTH + KV_WIDTH:B_WIDTH + 2 * KV_WIDTH])
    z = _dot(h, w_ref[:, B_WIDTH + 2 * KV_WIDTH:])
    if rope:
        cos, sa, sb = cos_ref[...], sa_ref[...], sb_ref[...]
        q = _rope(q, cos, sa, sb)
        k = _rope(k, cos, sa, sb)
    q = (q * Q_SCALE).astype(BF16)
    gz = _silu(z).astype(BF16)
    v_t = v.T.astype(BF16)
    low = lax.broadcasted_iota(jnp.int32, (1, LANES), 1) < HEAD_DIM
    group_w = B_WIDTH // N_KV_HEADS
    for g in range(N_KV_HEADS):
        q_ref[0, g] = q[:, g * group_w:(g + 1) * group_w]
        gz_ref[0, g] = gz[:, g * group_w:(g + 1) * group_w]
        vt_ref[0, g] = v_t[g * HEAD_DIM:(g + 1) * HEAD_DIM, :]
    for j in range(KV_WIDTH // LANES):
        blk = k[:, j * LANES:(j + 1) * LANES]
        swapped = pltpu.roll(blk, HEAD_DIM, 1)
        kk_ref[0, 2 * j] = jnp.where(low, blk, swapped).astype(BF16)
        kk_ref[0, 2 * j + 1] = jnp.where(low, swapped, blk).astype(BF16)


def _project_b(x, mod, w_in, tables, *, tm, rope):
    bsz, length, _ = x.shape
    cos, sa, sb = tables
    group_w = B_WIDTH // N_KV_HEADS
    grouped = lambda width: jax.ShapeDtypeStruct((bsz, N_KV_HEADS, length, width), BF16)
    grouped_spec = lambda width: pl.BlockSpec((1, N_KV_HEADS, tm, width),
                                              lambda b, t: (b, 0, t, 0))
    tab_spec = pl.BlockSpec((tm, LANES), lambda b, t: (t, 0))
    kernel = functools.partial(_project_b_kernel, rope=rope)
    return pl.pallas_call(
        kernel,
        out_shape=(grouped(group_w), grouped(LANES),
                   jax.ShapeDtypeStruct((bsz, N_KV_HEADS, HEAD_DIM, length), BF16),
                   grouped(group_w)),
        grid=(bsz, length // tm),
        in_specs=[
            pl.BlockSpec((1, tm, D_MODEL), lambda b, t: (b, t, 0)),
            pl.BlockSpec((1, 3, D_MODEL), lambda b, t: (b, 0, 0)),
            _const_spec((D_MODEL, 2 * B_WIDTH + 2 * KV_WIDTH)),
            tab_spec, tab_spec, tab_spec,
        ],
        out_specs=(grouped_spec(group_w), grouped_spec(LANES),
                   pl.BlockSpec((1, N_KV_HEADS, HEAD_DIM, tm), lambda b, t: (b, 0, 0, t)),
                   grouped_spec(group_w)),
        compiler_params=pltpu.CompilerParams(
            dimension_semantics=("parallel", "parallel"), vmem_limit_bytes=VMEM_LIMIT),
        name="project_b_rope" if rope else "project_b",
    )(x, mod, w_in, cos, sa, sb)


def _attend_b_kernel(*refs, local, n_steps, n_sub):
    if local:
        (sink_ref, q_ref, *tile_refs, kx_ref, vtx_ref, gz_ref, x_ref, mod_ref, w_out_ref,
         g_ref, b_ref, mask_ref, o_ref, sx_ref, m_ref, ot_ref, og_ref, sl_ref) = refs
        k_refs, vt_refs = tile_refs[:n_sub + 2], tile_refs[n_sub + 2:]
    else:
        (sink_ref, q_ref, kx_ref, vtx_ref, gz_ref, x_ref, mod_ref, w_out_ref, g_ref, b_ref,
         o_ref, sx_ref, m_ref, ot_ref, og_ref) = refs

    tq = ATT_BLOCK
    low = lax.broadcasted_iota(jnp.int32, (1, LANES), 1) < HEAD_DIM
    left = lax.broadcasted_iota(jnp.int32, (1, 2 * tq), 1) < tq
    if local:
        i = pl.program_id(1)
        first = jnp.where(i == 0, NEG_INF, 0.0).astype(F32)
        last = jnp.where(i == n_steps - 1, NEG_INF, 0.0).astype(F32)
        bias_prev = [mask_ref[0:WINDOW, :] + (first if sub == 0 else 0.0)
                     for sub in range(n_sub)]
        bias_next = [mask_ref[WINDOW:2 * WINDOW, :] + (last if sub == n_sub - 1 else 0.0)
                     for sub in range(n_sub)]

    def sink_row(g, half):
        return jnp.where(left, sink_ref[4 * g + half], sink_ref[4 * g + 2 + half]) * LOG2E

    units = [(sub, half) for sub in range(n_sub) for half in range(2)]

    def scores(g, sub, half):
        rows = slice(sub * tq, (sub + 1) * tq)
        qp = jnp.concatenate([q_ref[0, g, rows, 0:LANES], q_ref[0, g, rows, LANES:2 * LANES]],
                             axis=0)
        qh = jnp.where(low, qp, jnp.zeros_like(qp)) if half == 0 else \
            jnp.where(low, jnp.zeros_like(qp), qp)
        sx = _dot_nt(kx_ref[0, g], qh)
        sx_ref[g, sub, half] = sx
        m = jnp.max(sx, axis=0, keepdims=True)
        if local:
            kl = jnp.concatenate([k_refs[sub + t][0, g] for t in range(3)], axis=0)
            sl = _dot_nt(kl, qh)
            sl = jnp.concatenate(
                [sl[0:WINDOW] + bias_prev[sub], sl[WINDOW:2 * WINDOW],
                 sl[2 * WINDOW:3 * WINDOW] + bias_next[sub]], axis=0)
            sl_ref[g, sub, half] = sl
            m = jnp.maximum(m, jnp.max(sl, axis=0, keepdims=True))
        m_ref[g, sub, half] = jnp.maximum(m, sink_row(g, half))

    def with_ones(vt):
        return jnp.concatenate([vt, jnp.ones((ONES_ROWS, vt.shape[1]), BF16)], axis=0)

    def attend(g, sub, half):
        m = m_ref[g, sub, half]
        px = jnp.exp2(sx_ref[g, sub, half] - m)
        o = _dot(with_ones(vtx_ref[0, g]), px.astype(BF16))
        if local:
            vt = jnp.concatenate([vt_refs[sub + t][0, g] for t in range(3)], axis=1)
            pl_ = jnp.exp2(sl_ref[g, sub, half] - m)
            o = o + _dot(with_ones(vt), pl_.astype(BF16))
        denom = o[HEAD_DIM:HEAD_DIM + 1] + jnp.exp2(sink_row(g, half) - m)
        ot_ref[g, sub, half] = o[0:HEAD_DIM] * (1.0 / denom)

    for sub, half in units:
        scores(0, sub, half)

    def body(g, carry):
        for sub, half in units:
            attend(g, sub, half)
            scores(g + 1, sub, half)
        return carry

    lax.fori_loop(0, N_KV_HEADS - 1, body, 0)
    for sub, half in units:
        attend(N_KV_HEADS - 1, sub, half)

    out = None
    for g in range(N_KV_HEADS):
        for sub in range(n_sub):
            rows = slice(sub * tq, (sub + 1) * tq)
            o_t = jnp.concatenate([ot_ref[g, sub, 0], ot_ref[g, sub, 1]], axis=0)
            for r in range(2):
                oc = slice(r * LANES, (r + 1) * LANES)
                o_blk = o_t[:, r * tq:(r + 1) * tq].T
                og_ref[g, rows, oc] = (o_blk * gz_ref[0, g, rows, oc].astype(F32)).astype(BF16)
        part = _dot(og_ref[g], w_out_ref[g])
        out = part if out is None else out + part
    o_ref[0] = _post_norm(x_ref[0], out, mod_ref, g_ref, b_ref)


def _attend_b(q, kk, vt, kkx, vtx, gz, x, mod, w_out, sink, ln_g, ln_b, mask, *, local):
    bsz, _, length, group_w = q.shape
    n_ctx = kkx.shape[2]
    tq = ATT_BLOCK
    n_tiles = length // tq
    n_sub = min(ATT_SUB, n_tiles)
    rows = n_sub * tq
    n_steps = length // rows
    grouped_spec = pl.BlockSpec((1, N_KV_HEADS, rows, group_w), lambda b, i: (b, 0, i, 0))
    x_spec = pl.BlockSpec((1, rows, D_MODEL), lambda b, i: (b, i, 0))
    tile = lambda t: (lambda i: jnp.clip(n_sub * i + t - 1, 0, n_tiles - 1))
    k_specs = [pl.BlockSpec((1, N_KV_HEADS, tq, LANES), lambda b, i, f=tile(t): (b, 0, f(i), 0))
               for t in range(n_sub + 2)]
    vt_specs = [pl.BlockSpec((1, N_KV_HEADS, HEAD_DIM, tq),
                             lambda b, i, f=tile(t): (b, 0, 0, f(i))) for t in range(n_sub + 2)]
    in_specs = [pl.BlockSpec(memory_space=pltpu.SMEM), grouped_spec]
    args = [sink, q]
    if local:
        in_specs += k_specs + vt_specs
        args += [kk] * (n_sub + 2) + [vt] * (n_sub + 2)
    in_specs += [pl.BlockSpec((1, N_KV_HEADS, n_ctx, LANES), lambda b, i: (b, 0, 0, 0)),
                 pl.BlockSpec((1, N_KV_HEADS, HEAD_DIM, n_ctx), lambda b, i: (b, 0, 0, 0))]
    args += [kkx, vtx]
    in_specs += [grouped_spec, x_spec, pl.BlockSpec((1, 3, D_MODEL), lambda b, i: (b, 0, 0)),
                 _const_spec((N_KV_HEADS, group_w, D_MODEL)), _const_spec((1, D_MODEL)),
                 _const_spec((1, D_MODEL))]
    args += [gz, x, mod, w_out.reshape(N_KV_HEADS, group_w, D_MODEL), ln_g, ln_b]
    scratch = [pltpu.VMEM((N_KV_HEADS, n_sub, 2, n_ctx, 2 * tq), F32),
               pltpu.VMEM((N_KV_HEADS, n_sub, 2, 1, 2 * tq), F32),
               pltpu.VMEM((N_KV_HEADS, n_sub, 2, HEAD_DIM, 2 * tq), F32),
               pltpu.VMEM((N_KV_HEADS, rows, group_w), BF16)]
    if local:
        in_specs.append(_const_spec((2 * WINDOW, 2 * ATT_BLOCK)))
        args.append(mask)
        scratch.append(pltpu.VMEM((N_KV_HEADS, n_sub, 2, 3 * WINDOW, 2 * tq), F32))
    kernel = functools.partial(_attend_b_kernel, local=local, n_steps=n_steps, n_sub=n_sub)
    return pl.pallas_call(
        kernel,
        out_shape=jax.ShapeDtypeStruct(x.shape, F32),
        grid=(bsz, n_steps),
        in_specs=in_specs,
        out_specs=x_spec,
        scratch_shapes=scratch,
        compiler_params=pltpu.CompilerParams(
            dimension_semantics=("parallel", "parallel"), vmem_limit_bytes=VMEM_LIMIT),
        name="attend_b_local" if local else "attend_b_ctx",
    )(*args)


def _rope_tables(length):
    rows = length // GRID_W
    row = jnp.repeat(jnp.arange(rows), GRID_W).astype(F32)
    col = jnp.tile(jnp.arange(GRID_W), rows).astype(F32)
    n_freq = ROPE_AXIS_DIM // 2
    inv = ROPE_BASE ** (-jnp.arange(n_freq, dtype=F32) / n_freq)
    ang_r, ang_c = row[:, None] * inv[None, :], col[:, None] * inv[None, :]
    cr, sr, cc, sc = jnp.cos(ang_r), jnp.sin(ang_r), jnp.cos(ang_c), jnp.sin(ang_c)
    zero = jnp.zeros_like(sr)
    reps = LANES // HEAD_DIM
    cos = jnp.tile(jnp.concatenate([cr, cr, cc, cc], axis=1), (1, reps))
    sin_a = jnp.tile(jnp.concatenate([-sr, zero, -sc, zero], axis=1), (1, reps))
    sin_b = jnp.tile(jnp.concatenate([zero, sr, zero, sc], axis=1), (1, reps))
    return cos, sin_a, sin_b


def _band_mask():
    kj = jnp.arange(WINDOW)[:, None]
    qi = jnp.arange(ATT_BLOCK)[None, :]
    prev = jnp.where(kj >= qi, 0.0, NEG_INF)
    nxt = jnp.where(kj <= qi, 0.0, NEG_INF)
    return jnp.tile(jnp.concatenate([prev, nxt], axis=0), (1, 2)).astype(F32)


def kernel(x, c, ctx, c_ctx, ada_w, ada_b, ln_g, ln_b, a_w_in, a_ln_g, a_ln_b, a_w_s, a_b_s,
           a_w_out, b_w_in, b_sink, b_w_out):
    bsz, length, _ = x.shape
    n_ctx = ctx.shape[1]
    assert bsz + 1 <= MOD_ROWS

    c_rows = jnp.zeros((MOD_ROWS, D_MODEL), F32).at[:bsz].set(c).at[bsz].set(c_ctx)
    mod_all = _modulation(c_rows, ada_w, ada_b)
    tables = _rope_tables(length)
    mask = _band_mask()

    for i in range(DEPTH):
        j = i // N_MIXERS
        need_ctx_out = i < DEPTH - 1
        mod = mod_all[i, :bsz].reshape(bsz, 3, D_MODEL)
        mod_c = jnp.broadcast_to(mod_all[i, bsz].reshape(1, 3, D_MODEL), (bsz, 3, D_MODEL))
        g_i, b_i = ln_g[i].reshape(1, D_MODEL), ln_b[i].reshape(1, D_MODEL)
        if i % N_MIXERS == 0:
            b_s_full = jnp.repeat(a_b_s[j].T, A_GW, axis=1)
            weights = (a_w_in[j].astype(BF16), a_ln_g[j].reshape(1, A_WIDTH),
                       a_ln_b[j].reshape(1, A_WIDTH), a_w_s[j].astype(BF16), b_s_full,
                       a_w_out[j].astype(BF16), g_i, b_i)
            x_new = _layer_a(x, mod, *weights, tm=512)
            if need_ctx_out:
                ctx = _layer_a(ctx, mod_c, *weights, tm=n_ctx)
            x = x_new
        else:
            w_in = b_w_in[j].astype(BF16)
            w_out = b_w_out[j].astype(BF16)
            sink = b_sink[j].astype(F32)
            q, kk, vt, gz = _project_b(x, mod, w_in, tables, tm=256, rope=True)
            qc, kkc, vtc, gzc = _project_b(ctx, mod_c, w_in, tables, tm=n_ctx, rope=False)
            x = _attend_b(q, kk, vt, kkc, vtc, gz, x, mod, w_out, sink, g_i, b_i, mask,
                          local=True)
            if need_ctx_out:
                ctx = _attend_b(qc, None, None, kkc, vtc, gzc, ctx, mod_c, w_out, sink, g_i, b_i,
                                None, local=False)
    return x
```

```python
import functools
import math

import jax
import jax.numpy as jnp
from jax import lax
from jax.experimental import pallas as pl
from jax.experimental.pallas import tpu as pltpu

D_MODEL = 1024
DEPTH = 4
GRID_W = 64
N_MIXERS = 2
CHUNK = 128
A_WIDTH = 2 * D_MODEL
A_GROUPS = 8
A_GW = A_WIDTH // A_GROUPS
HEAD_DIM = 64
N_HEADS = D_MODEL // HEAD_DIM
N_KV_HEADS = N_HEADS // 4
B_WIDTH = N_HEADS * HEAD_DIM
KV_WIDTH = N_KV_HEADS * HEAD_DIM
WINDOW = 128
ATT_BLOCK = 128
ATT_SUB = 4
ROPE_AXIS_DIM = HEAD_DIM // 2
ROPE_BASE = 10000.0
LN_EPS = 1e-5
NEG_INF = -1e30
ALPHA = (2.0 * DEPTH) ** 0.25

LANES = 128
MOD_ROWS = 8
VMEM_LIMIT = 48 * 1024 * 1024

BF16 = jnp.bfloat16
F32 = jnp.float32

LOG2E = math.log2(math.e)
Q_SCALE = HEAD_DIM ** -0.5 * LOG2E
ONES_ROWS = 16

_GELU_C0 = math.sqrt(2.0 / math.pi)
_GELU_C1 = _GELU_C0 * 0.044715


def _dot(a, b):
    return jnp.dot(a, b, preferred_element_type=F32)


def _dot_nt(a, b):
    return lax.dot_general(a, b, (((1,), (1,)), ((), ())), preferred_element_type=F32)


def _gelu(x):
    inner = x * (_GELU_C1 * (x * x) + _GELU_C0)
    hx = 0.5 * x
    return hx + hx * jnp.tanh(inner)


def _silu(z):
    hz = 0.5 * z
    return hz + hz * jnp.tanh(hz)


def _layer_norm(x, g, b):
    mu = jnp.mean(x, axis=-1, keepdims=True)
    xc = x - mu
    var = jnp.mean(xc * xc, axis=-1, keepdims=True)
    return xc * lax.rsqrt(var + LN_EPS) * g + b


def _const_spec(shape):
    nd = len(shape)
    return pl.BlockSpec(shape, lambda *_: (0,) * nd, pipeline_mode=pl.Buffered(1))


def _layer_spec(shape, j):
    nd = len(shape)
    return pl.BlockSpec((1,) + tuple(shape), lambda *_: (j,) + (0,) * nd,
                        pipeline_mode=pl.Buffered(1))


def _modulation_kernel(c_ref, w_ref, b_ref, o_ref):
    c = c_ref[...]
    cond = _silu(c).astype(BF16)
    o_ref[0] = _dot(cond, w_ref[0].astype(BF16)) + b_ref[0]


def _modulation(c_rows, ada_w, ada_b):
    tn = D_MODEL
    return pl.pallas_call(
        _modulation_kernel,
        out_shape=jax.ShapeDtypeStruct((DEPTH, MOD_ROWS, 3 * D_MODEL), F32),
        grid=(DEPTH, 3 * D_MODEL // tn),
        in_specs=[
            pl.BlockSpec((MOD_ROWS, D_MODEL), lambda i, j: (0, 0)),
            pl.BlockSpec((1, D_MODEL, tn), lambda i, j: (i, 0, j)),
            pl.BlockSpec((1, 1, tn), lambda i, j: (i, 0, j)),
        ],
        out_specs=pl.BlockSpec((1, MOD_ROWS, tn), lambda i, j: (i, 0, j)),
        compiler_params=pltpu.CompilerParams(
            dimension_semantics=("parallel", "parallel"), vmem_limit_bytes=VMEM_LIMIT),
        name="modulation",
    )(c_rows, ada_w, ada_b.reshape(DEPTH, 1, 3 * D_MODEL))


def _modulate(x, mod_ref):
    shift = mod_ref[0, 0:1, :]
    scale = mod_ref[0, 1:2, :]
    return x * (1.0 + scale) + shift


def _post_norm(x, out, mod_ref, g_ref, b_ref):
    gate = mod_ref[0, 2:3, :]
    return _layer_norm(ALPHA * x + gate * out, g_ref[...], b_ref[...])


def _layer_a_kernel(x_ref, mod_ref, w_in_ref, lng_ref, lnb_ref, ws_ref, bs_ref, w_out_ref,
                    g_ref, b_ref, o_ref, vn_ref, y_ref, *, tm):
    x = x_ref[0]
    h = _modulate(x, mod_ref).astype(BF16)
    v = _gelu(_dot(h, w_in_ref[0, :, A_WIDTH:2 * A_WIDTH]))
    vn_ref[...] = _layer_norm(v, lng_ref[...], lnb_ref[...]).astype(BF16)
    for g in range(A_GROUPS):
        cols = slice(g * A_GW, (g + 1) * A_GW)
        u = _gelu(_dot(h, w_in_ref[0, :, g * A_GW:(g + 1) * A_GW]))
        z = _dot(h, w_in_ref[0, :, 2 * A_WIDTH + g * A_GW:2 * A_WIDTH + (g + 1) * A_GW])
        uz = u * _silu(z)
        for c in range(tm // CHUNK):
            rows = slice(c * CHUNK, (c + 1) * CHUNK)
            s = _dot(ws_ref[0, g], vn_ref[rows, cols]) + bs_ref[:, cols]
            y_ref[rows, cols] = (uz[rows] * s).astype(BF16)
    out = _dot(y_ref[...], w_out_ref[0])
    o_ref[0] = _post_norm(x, out, mod_ref, g_ref, b_ref)


def _layer_a(x, mod, j, w_in, a_ln_g, a_ln_b, w_s, b_s_full, w_out, ln_g, ln_b, *, tm):
    bsz, length, _ = x.shape
    kernel = functools.partial(_layer_a_kernel, tm=tm)
    return pl.pallas_call(
        kernel,
        out_shape=jax.ShapeDtypeStruct(x.shape, F32),
        grid=(bsz, length // tm),
        in_specs=[
            pl.BlockSpec((1, tm, D_MODEL), lambda b, t: (b, t, 0)),
            pl.BlockSpec((1, 3, D_MODEL), lambda b, t: (b, 0, 0)),
            _layer_spec((D_MODEL, 3 * A_WIDTH), j),
            _const_spec((1, A_WIDTH)),
            _const_spec((1, A_WIDTH)),
            _layer_spec((A_GROUPS, CHUNK, CHUNK), j),
            _const_spec((CHUNK, A_WIDTH)),
            _layer_spec((A_WIDTH, D_MODEL), j),
            _const_spec((1, D_MODEL)),
            _const_spec((1, D_MODEL)),
        ],
        out_specs=pl.BlockSpec((1, tm, D_MODEL), lambda b, t: (b, t, 0)),
        scratch_shapes=[pltpu.VMEM((tm, A_WIDTH), BF16), pltpu.VMEM((tm, A_WIDTH), BF16)],
        compiler_params=pltpu.CompilerParams(
            dimension_semantics=("parallel", "parallel"), vmem_limit_bytes=VMEM_LIMIT),
        name="layer_a",
    )(x, mod, w_in, a_ln_g, a_ln_b, w_s, b_s_full, w_out, ln_g, ln_b)


def _rope(t, cos, sin_a, sin_b):
    n = t.shape[1] // LANES
    half = ROPE_AXIS_DIM // 2
    outs = []
    for j in range(n):
        blk = t[:, j * LANES:(j + 1) * LANES]
        up = pltpu.roll(blk, LANES - half, 1)
        dn = pltpu.roll(blk, half, 1)
        outs.append(blk * cos + up * sin_a + dn * sin_b)
    return jnp.concatenate(outs, axis=1)


def _project_b_kernel(x_ref, mod_ref, w_ref, cos_ref, sa_ref, sb_ref,
                      q_ref, kk_ref, vt_ref, gz_ref, *, rope):
    x = x_ref[0]
    h = _modulate(x, mod_ref).astype(BF16)
    q = _dot(h, w_ref[0, :, 0:B_WIDTH])
    k = _dot(h, w_ref[0, :, B_WIDTH:B_WIDTH + KV_WIDTH])
    v = _dot(h, w_ref[0, :, B_WIDTH + KV_WIDTH:B_WIDTH + 2 * KV_WIDTH])
    z = _dot(h, w_ref[0, :, B_WIDTH + 2 * KV_WIDTH:])
    if rope:
        cos, sa, sb = cos_ref[...], sa_ref[...], sb_ref[...]
        q = _rope(q, cos, sa, sb)
        k = _rope(k, cos, sa, sb)
    q = (q * Q_SCALE).astype(BF16)
    gz = _silu(z).astype(BF16)
    v_t = v.T.astype(BF16)
    low = lax.broadcasted_iota(jnp.int32, (1, LANES), 1) < HEAD_DIM
    group_w = B_WIDTH // N_KV_HEADS
    for g in range(N_KV_HEADS):
        q_ref[0, g] = q[:, g * group_w:(g + 1) * group_w]
        gz_ref[0, g] = gz[:, g * group_w:(g + 1) * group_w]
        vt_ref[0, g] = v_t[g * HEAD_DIM:(g + 1) * HEAD_DIM, :]
    for j in range(KV_WIDTH // LANES):
        blk = k[:, j * LANES:(j + 1) * LANES]
        swapped = pltpu.roll(blk, HEAD_DIM, 1)
        kk_ref[0, 2 * j] = jnp.where(low, blk, swapped).astype(BF16)
        kk_ref[0, 2 * j + 1] = jnp.where(low, swapped, blk).astype(BF16)


def _project_b(x, mod, j, w_in, tables, *, tm, rope):
    bsz, length, _ = x.shape
    cos, sa, sb = tables
    group_w = B_WIDTH // N_KV_HEADS
    grouped = lambda width: jax.ShapeDtypeStruct((bsz, N_KV_HEADS, length, width), BF16)
    grouped_spec = lambda width: pl.BlockSpec((1, N_KV_HEADS, tm, width),
                                              lambda b, t: (b, 0, t, 0))
    tab_spec = pl.BlockSpec((tm, LANES), lambda b, t: (t, 0))
    kernel = functools.partial(_project_b_kernel, rope=rope)
    return pl.pallas_call(
        kernel,
        out_shape=(grouped(group_w), grouped(LANES),
                   jax.ShapeDtypeStruct((bsz, N_KV_HEADS, HEAD_DIM, length), BF16),
                   grouped(group_w)),
        grid=(bsz, length // tm),
        in_specs=[
            pl.BlockSpec((1, tm, D_MODEL), lambda b, t: (b, t, 0)),
            pl.BlockSpec((1, 3, D_MODEL), lambda b, t: (b, 0, 0)),
            _layer_spec((D_MODEL, 2 * B_WIDTH + 2 * KV_WIDTH), j),
            tab_spec, tab_spec, tab_spec,
        ],
        out_specs=(grouped_spec(group_w), grouped_spec(LANES),
                   pl.BlockSpec((1, N_KV_HEADS, HEAD_DIM, tm), lambda b, t: (b, 0, 0, t)),
                   grouped_spec(group_w)),
        compiler_params=pltpu.CompilerParams(
            dimension_semantics=("parallel", "parallel"), vmem_limit_bytes=VMEM_LIMIT),
        name="project_b_rope" if rope else "project_b",
    )(x, mod, w_in, cos, sa, sb)


def _attend_b_kernel(*refs, local, n_steps, n_sub):
    if local:
        (sink_ref, q_ref, *tile_refs, kx_ref, vtx_ref, gz_ref, x_ref, mod_ref, w_out_ref,
         g_ref, b_ref, mask_ref, o_ref, sx_ref, m_ref, ot_ref, og_ref, sl_ref) = refs
        k_refs, vt_refs = tile_refs[:n_sub + 2], tile_refs[n_sub + 2:]
    else:
        (sink_ref, q_ref, kx_ref, vtx_ref, gz_ref, x_ref, mod_ref, w_out_ref, g_ref, b_ref,
         o_ref, sx_ref, m_ref, ot_ref, og_ref) = refs

    tq = ATT_BLOCK
    low = lax.broadcasted_iota(jnp.int32, (1, LANES), 1) < HEAD_DIM
    left = lax.broadcasted_iota(jnp.int32, (1, 2 * tq), 1) < tq
    if local:
        i = pl.program_id(1)
        first = jnp.where(i == 0, NEG_INF, 0.0).astype(F32)
        last = jnp.where(i == n_steps - 1, NEG_INF, 0.0).astype(F32)
        bias_prev = [mask_ref[0:WINDOW, :] + (first if sub == 0 else 0.0)
                     for sub in range(n_sub)]
        bias_next = [mask_ref[WINDOW:2 * WINDOW, :] + (last if sub == n_sub - 1 else 0.0)
                     for sub in range(n_sub)]

    def sink_row(g, half):
        return jnp.where(left, sink_ref[4 * g + half], sink_ref[4 * g + 2 + half]) * LOG2E

    units = [(sub, half) for sub in range(n_sub) for half in range(2)]

    def scores(g, sub, half):
        rows = slice(sub * tq, (sub + 1) * tq)
        qp = jnp.concatenate([q_ref[0, g, rows, 0:LANES], q_ref[0, g, rows, LANES:2 * LANES]],
                             axis=0)
        qh = jnp.where(low, qp, jnp.zeros_like(qp)) if half == 0 else \
            jnp.where(low, jnp.zeros_like(qp), qp)
        sx = _dot_nt(kx_ref[0, g], qh)
        sx_ref[g, sub, half] = sx
        m = jnp.max(sx, axis=0, keepdims=True)
        if local:
            kl = jnp.concatenate([k_refs[sub + t][0, g] for t in range(3)], axis=0)
            sl = _dot_nt(kl, qh)
            sl = jnp.concatenate(
                [sl[0:WINDOW] + bias_prev[sub], sl[WINDOW:2 * WINDOW],
                 sl[2 * WINDOW:3 * WINDOW] + bias_next[sub]], axis=0)
            sl_ref[g, sub, half] = sl
            m = jnp.maximum(m, jnp.max(sl, axis=0, keepdims=True))
        m_ref[g, sub, half] = jnp.maximum(m, sink_row(g, half))

    def with_ones(vt):
        return jnp.concatenate([vt, jnp.ones((ONES_ROWS, vt.shape[1]), BF16)], axis=0)

    def attend(g, sub, half):
        m = m_ref[g, sub, half]
        px = jnp.exp2(sx_ref[g, sub, half] - m)
        o = _dot(with_ones(vtx_ref[0, g]), px.astype(BF16))
        if local:
            vt = jnp.concatenate([vt_refs[sub + t][0, g] for t in range(3)], axis=1)
            pl_ = jnp.exp2(sl_ref[g, sub, half] - m)
            o = o + _dot(with_ones(vt), pl_.astype(BF16))
        denom = o[HEAD_DIM:HEAD_DIM + 1] + jnp.exp2(sink_row(g, half) - m)
        ot_ref[g, sub, half] = o[0:HEAD_DIM] * (1.0 / denom)

    for sub, half in units:
        scores(0, sub, half)

    def body(g, carry):
        for sub, half in units:
            attend(g, sub, half)
            scores(g + 1, sub, half)
        return carry

    lax.fori_loop(0, N_KV_HEADS - 1, body, 0)
    for sub, half in units:
        attend(N_KV_HEADS - 1, sub, half)

    out = None
    for g in range(N_KV_HEADS):
        for sub in range(n_sub):
            rows = slice(sub * tq, (sub + 1) * tq)
            o_t = jnp.concatenate([ot_ref[g, sub, 0], ot_ref[g, sub, 1]], axis=0)
            for r in range(2):
                oc = slice(r * LANES, (r + 1) * LANES)
                o_blk = o_t[:, r * tq:(r + 1) * tq].T
                og_ref[g, rows, oc] = (o_blk * gz_ref[0, g, rows, oc].astype(F32)).astype(BF16)
        part = _dot(og_ref[g], w_out_ref[0, g])
        out = part if out is None else out + part
    o_ref[0] = _post_norm(x_ref[0], out, mod_ref, g_ref, b_ref)


def _attend_b(q, kk, vt, kkx, vtx, gz, x, mod, j, w_out, sink, ln_g, ln_b, mask, *, local):
    bsz, _, length, group_w = q.shape
    n_ctx = kkx.shape[2]
    tq = ATT_BLOCK
    n_tiles = length // tq
    n_sub = min(ATT_SUB, n_tiles)
    rows = n_sub * tq
    n_steps = length // rows
    grouped_spec = pl.BlockSpec((1, N_KV_HEADS, rows, group_w), lambda b, i: (b, 0, i, 0))
    x_spec = pl.BlockSpec((1, rows, D_MODEL), lambda b, i: (b, i, 0))
    tile = lambda t: (lambda i: jnp.clip(n_sub * i + t - 1, 0, n_tiles - 1))
    k_specs = [pl.BlockSpec((1, N_KV_HEADS, tq, LANES), lambda b, i, f=tile(t): (b, 0, f(i), 0))
               for t in range(n_sub + 2)]
    vt_specs = [pl.BlockSpec((1, N_KV_HEADS, HEAD_DIM, tq),
                             lambda b, i, f=tile(t): (b, 0, 0, f(i))) for t in range(n_sub + 2)]
    in_specs = [pl.BlockSpec(memory_space=pltpu.SMEM), grouped_spec]
    args = [sink, q]
    if local:
        in_specs += k_specs + vt_specs
        args += [kk] * (n_sub + 2) + [vt] * (n_sub + 2)
    in_specs += [pl.BlockSpec((1, N_KV_HEADS, n_ctx, LANES), lambda b, i: (b, 0, 0, 0)),
                 pl.BlockSpec((1, N_KV_HEADS, HEAD_DIM, n_ctx), lambda b, i: (b, 0, 0, 0))]
    args += [kkx, vtx]
    in_specs += [grouped_spec, x_spec, pl.BlockSpec((1, 3, D_MODEL), lambda b, i: (b, 0, 0)),
                 _layer_spec((N_KV_HEADS, group_w, D_MODEL), j), _const_spec((1, D_MODEL)),
                 _const_spec((1, D_MODEL))]
    args += [gz, x, mod, w_out, ln_g, ln_b]
    scratch = [pltpu.VMEM((N_KV_HEADS, n_sub, 2, n_ctx, 2 * tq), F32),
               pltpu.VMEM((N_KV_HEADS, n_sub, 2, 1, 2 * tq), F32),
               pltpu.VMEM((N_KV_HEADS, n_sub, 2, HEAD_DIM, 2 * tq), F32),
               pltpu.VMEM((N_KV_HEADS, rows, group_w), BF16)]
    if local:
        in_specs.append(_const_spec((2 * WINDOW, 2 * ATT_BLOCK)))
        args.append(mask)
        scratch.append(pltpu.VMEM((N_KV_HEADS, n_sub, 2, 3 * WINDOW, 2 * tq), F32))
    kernel = functools.partial(_attend_b_kernel, local=local, n_steps=n_steps, n_sub=n_sub)
    return pl.pallas_call(
        kernel,
        out_shape=jax.ShapeDtypeStruct(x.shape, F32),
        grid=(bsz, n_steps),
        in_specs=in_specs,
        out_specs=x_spec,
        scratch_shapes=scratch,
        compiler_params=pltpu.CompilerParams(
            dimension_semantics=("parallel", "parallel"), vmem_limit_bytes=VMEM_LIMIT),
        name="attend_b_local" if local else "attend_b_ctx",
    )(*args)


def _rope_tables(length):
    rows = length // GRID_W
    row = jnp.repeat(jnp.arange(rows), GRID_W).astype(F32)
    col = jnp.tile(jnp.arange(GRID_W), rows).astype(F32)
    n_freq = ROPE_AXIS_DIM // 2
    inv = ROPE_BASE ** (-jnp.arange(n_freq, dtype=F32) / n_freq)
    ang_r, ang_c = row[:, None] * inv[None, :], col[:, None] * inv[None, :]
    cr, sr, cc, sc = jnp.cos(ang_r), jnp.sin(ang_r), jnp.cos(ang_c), jnp.sin(ang_c)
    zero = jnp.zeros_like(sr)
    reps = LANES // HEAD_DIM
    cos = jnp.tile(jnp.concatenate([cr, cr, cc, cc], axis=1), (1, reps))
    sin_a = jnp.tile(jnp.concatenate([-sr, zero, -sc, zero], axis=1), (1, reps))
    sin_b = jnp.tile(jnp.concatenate([zero, sr, zero, sc], axis=1), (1, reps))
    return cos, sin_a, sin_b


def _band_mask():
    kj = jnp.arange(WINDOW)[:, None]
    qi = jnp.arange(ATT_BLOCK)[None, :]
    prev = jnp.where(kj >= qi, 0.0, NEG_INF)
    nxt = jnp.where(kj <= qi, 0.0, NEG_INF)
    return jnp.tile(jnp.concatenate([prev, nxt], axis=0), (1, 2)).astype(F32)


def kernel(x, c, ctx, c_ctx, ada_w, ada_b, ln_g, ln_b, a_w_in, a_ln_g, a_ln_b, a_w_s, a_b_s,
           a_w_out, b_w_in, b_sink, b_w_out):
    bsz, length, _ = x.shape
    n_ctx = ctx.shape[1]
    assert bsz + 1 <= MOD_ROWS

    c_rows = jnp.zeros((MOD_ROWS, D_MODEL), F32).at[:bsz].set(c).at[bsz].set(c_ctx)
    mod_all = _modulation(c_rows, ada_w, ada_b)
    tables = _rope_tables(length)
    mask = _band_mask()
    a_w_in_h, a_w_s_h, a_w_out_h = (w.astype(BF16) for w in (a_w_in, a_w_s, a_w_out))
    b_w_in_h = b_w_in.astype(BF16)
    b_w_out_h = b_w_out.astype(BF16).reshape(-1, N_KV_HEADS, B_WIDTH // N_KV_HEADS, D_MODEL)

    for i in range(DEPTH):
        j = i // N_MIXERS
        need_ctx_out = i < DEPTH - 1
        mod = mod_all[i, :bsz].reshape(bsz, 3, D_MODEL)
        mod_c = jnp.broadcast_to(mod_all[i, bsz].reshape(1, 3, D_MODEL), (bsz, 3, D_MODEL))
        g_i, b_i = ln_g[i].reshape(1, D_MODEL), ln_b[i].reshape(1, D_MODEL)
        if i % N_MIXERS == 0:
            b_s_full = jnp.repeat(a_b_s[j].T, A_GW, axis=1)
            weights = (j, a_w_in_h, a_ln_g[j].reshape(1, A_WIDTH), a_ln_b[j].reshape(1, A_WIDTH),
                       a_w_s_h, b_s_full, a_w_out_h, g_i, b_i)
            x_new = _layer_a(x, mod, *weights, tm=512)
            if need_ctx_out:
                ctx = _layer_a(ctx, mod_c, *weights, tm=n_ctx)
            x = x_new
        else:
            sink = b_sink[j].astype(F32)
            q, kk, vt, gz = _project_b(x, mod, j, b_w_in_h, tables, tm=512, rope=True)
            qc, kkc, vtc, gzc = _project_b(ctx, mod_c, j, b_w_in_h, tables, tm=n_ctx, rope=False)
            x = _attend_b(q, kk, vt, kkc, vtc, gz, x, mod, j, b_w_out_h, sink, g_i, b_i, mask,
                          local=True)
            if need_ctx_out:
                ctx = _attend_b(qc, None, None, kkc, vtc, gzc, ctx, mod_c, j, b_w_out_h, sink,
                                g_i, b_i, None, local=False)
    return x
```

```python
import functools
import math

import jax
import jax.numpy as jnp
from jax import lax
from jax.experimental import pallas as pl
from jax.experimental.pallas import tpu as pltpu

D_MODEL = 1024
DEPTH = 4
GRID_W = 64
N_MIXERS = 2
CHUNK = 128
A_WIDTH = 2 * D_MODEL
A_GROUPS = 8
A_GW = A_WIDTH // A_GROUPS
HEAD_DIM = 64
N_HEADS = D_MODEL // HEAD_DIM
N_KV_HEADS = N_HEADS // 4
B_WIDTH = N_HEADS * HEAD_DIM
KV_WIDTH = N_KV_HEADS * HEAD_DIM
WINDOW = 128
ATT_BLOCK = 128
ATT_SUB = 4
ROPE_AXIS_DIM = HEAD_DIM // 2
ROPE_BASE = 10000.0
LN_EPS = 1e-5
NEG_INF = -1e30
ALPHA = (2.0 * DEPTH) ** 0.25

LANES = 128
MOD_ROWS = 8
VMEM_LIMIT = 48 * 1024 * 1024

BF16 = jnp.bfloat16
F32 = jnp.float32

LOG2E = math.log2(math.e)
Q_SCALE = HEAD_DIM ** -0.5 * LOG2E
ONES_ROWS = 16
A_SUB_ROWS = 256

_GELU_C0 = math.sqrt(2.0 / math.pi)
_GELU_C1 = _GELU_C0 * 0.044715


def _dot(a, b):
    return jnp.dot(a, b, preferred_element_type=F32)


def _dot_nt(a, b):
    return lax.dot_general(a, b, (((1,), (1,)), ((), ())), preferred_element_type=F32)


def _gelu(x):
    inner = x * (_GELU_C1 * (x * x) + _GELU_C0)
    hx = 0.5 * x
    return hx + hx * jnp.tanh(inner)


def _silu(z):
    hz = 0.5 * z
    return hz + hz * jnp.tanh(hz)


def _layer_norm(x, g, b):
    mu = jnp.mean(x, axis=-1, keepdims=True)
    xc = x - mu
    var = jnp.mean(xc * xc, axis=-1, keepdims=True)
    return xc * lax.rsqrt(var + LN_EPS) * g + b


def _const_spec(shape):
    nd = len(shape)
    return pl.BlockSpec(shape, lambda *_: (0,) * nd, pipeline_mode=pl.Buffered(1))


def _pack_rows(w):
    wb = lax.bitcast_convert_type(w.astype(BF16), jnp.uint16).astype(jnp.uint32)
    return wb[..., 0::2, :] | (wb[..., 1::2, :] << 16)


def _unpack_rows(w_packed):
    return pltpu.bitcast(w_packed, BF16)


def _layer_spec(shape, j):
    nd = len(shape)
    return pl.BlockSpec((1,) + tuple(shape), lambda *_: (j,) + (0,) * nd,
                        pipeline_mode=pl.Buffered(1))


def _modulation_kernel(c_ref, w_ref, b_ref, o_ref):
    c = c_ref[...]
    cond = _silu(c).astype(BF16)
    o_ref[0] = _dot(cond, w_ref[0].astype(BF16)) + b_ref[0]


def _modulation(c_rows, ada_w, ada_b):
    tn = D_MODEL
    return pl.pallas_call(
        _modulation_kernel,
        out_shape=jax.ShapeDtypeStruct((DEPTH, MOD_ROWS, 3 * D_MODEL), F32),
        grid=(DEPTH, 3 * D_MODEL // tn),
        in_specs=[
            pl.BlockSpec((MOD_ROWS, D_MODEL), lambda i, j: (0, 0)),
            pl.BlockSpec((1, D_MODEL, tn), lambda i, j: (i, 0, j)),
            pl.BlockSpec((1, 1, tn), lambda i, j: (i, 0, j)),
        ],
        out_specs=pl.BlockSpec((1, MOD_ROWS, tn), lambda i, j: (i, 0, j)),
        compiler_params=pltpu.CompilerParams(
            dimension_semantics=("parallel", "parallel"), vmem_limit_bytes=VMEM_LIMIT),
        name="modulation",
    )(c_rows, ada_w, ada_b.reshape(DEPTH, 1, 3 * D_MODEL))


def _modulate(x, mod_ref):
    shift = mod_ref[0, 0:1, :]
    scale = mod_ref[0, 1:2, :]
    return x * (1.0 + scale) + shift


def _post_norm(x, out, mod_ref, g_ref, b_ref):
    gate = mod_ref[0, 2:3, :]
    return _layer_norm(ALPHA * x + gate * out, g_ref[...], b_ref[...])


def _layer_a_kernel(x_ref, mod_ref, w_in_ref, lng_ref, lnb_ref, ws_ref, bs_ref, w_out_ref,
                    g_ref, b_ref, o_ref, vn_ref, y_ref, *, tm, ts):
    for t in range(tm // ts):
        sub = slice(t * ts, (t + 1) * ts)
        x = x_ref[0, sub, :]
        h = _modulate(x, mod_ref).astype(BF16)
        w_in = lambda lo, hi: _unpack_rows(w_in_ref[0, :, lo:hi])
        v = _gelu(_dot(h, w_in(A_WIDTH, 2 * A_WIDTH)))
        vn_ref[sub, :] = _layer_norm(v, lng_ref[...], lnb_ref[...]).astype(BF16)
        for g in range(A_GROUPS):
            cols = slice(g * A_GW, (g + 1) * A_GW)
            u = _gelu(_dot(h, w_in(g * A_GW, (g + 1) * A_GW)))
            z = _dot(h, w_in(2 * A_WIDTH + g * A_GW, 2 * A_WIDTH + (g + 1) * A_GW))
            uz = u * _silu(z)
            for c in range(ts // CHUNK):
                rows = slice(t * ts + c * CHUNK, t * ts + (c + 1) * CHUNK)
                s = _dot(ws_ref[0, g], vn_ref[rows, cols]) + bs_ref[:, cols]
                y_ref[rows, cols] = (uz[c * CHUNK:(c + 1) * CHUNK] * s).astype(BF16)
        out = _dot(y_ref[sub, :], _unpack_rows(w_out_ref[0]))
        o_ref[0, sub, :] = _post_norm(x, out, mod_ref, g_ref, b_ref)


def _layer_a(x, mod, j, w_in, a_ln_g, a_ln_b, w_s, b_s_full, w_out, ln_g, ln_b, *, tm):
    bsz, length, _ = x.shape
    kernel = functools.partial(_layer_a_kernel, tm=tm, ts=min(tm, A_SUB_ROWS))
    return pl.pallas_call(
        kernel,
        out_shape=jax.ShapeDtypeStruct(x.shape, F32),
        grid=(bsz, length // tm),
        in_specs=[
            pl.BlockSpec((1, tm, D_MODEL), lambda b, t: (b, t, 0)),
            pl.BlockSpec((1, 3, D_MODEL), lambda b, t: (b, 0, 0)),
            _layer_spec((D_MODEL // 2, 3 * A_WIDTH), j),
            _const_spec((1, A_WIDTH)),
            _const_spec((1, A_WIDTH)),
            _layer_spec((A_GROUPS, CHUNK, CHUNK), j),
            _const_spec((CHUNK, A_WIDTH)),
            _layer_spec((A_WIDTH // 2, D_MODEL), j),
            _const_spec((1, D_MODEL)),
            _const_spec((1, D_MODEL)),
        ],
        out_specs=pl.BlockSpec((1, tm, D_MODEL), lambda b, t: (b, t, 0)),
        scratch_shapes=[pltpu.VMEM((tm, A_WIDTH), BF16), pltpu.VMEM((tm, A_WIDTH), BF16)],
        compiler_params=pltpu.CompilerParams(
            dimension_semantics=("parallel", "parallel"), vmem_limit_bytes=VMEM_LIMIT),
        name="layer_a",
    )(x, mod, w_in, a_ln_g, a_ln_b, w_s, b_s_full, w_out, ln_g, ln_b)


def _rope(t, cos, sin_a, sin_b):
    n = t.shape[1] // LANES
    half = ROPE_AXIS_DIM // 2
    outs = []
    for j in range(n):
        blk = t[:, j * LANES:(j + 1) * LANES]
        up = pltpu.roll(blk, LANES - half, 1)
        dn = pltpu.roll(blk, half, 1)
        outs.append(blk * cos + up * sin_a + dn * sin_b)
    return jnp.concatenate(outs, axis=1)


def _project_b_kernel(x_ref, mod_ref, w_ref, cos_ref, sa_ref, sb_ref,
                      q_ref, kk_ref, vt_ref, gz_ref, *, rope):
    x = x_ref[0]
    h = _modulate(x, mod_ref).astype(BF16)
    w = lambda lo, hi: _unpack_rows(w_ref[0, :, lo:hi])
    q = _dot(h, w(0, B_WIDTH))
    k = _dot(h, w(B_WIDTH, B_WIDTH + KV_WIDTH))
    v = _dot(h, w(B_WIDTH + KV_WIDTH, B_WIDTH + 2 * KV_WIDTH))
    z = _dot(h, w(B_WIDTH + 2 * KV_WIDTH, 2 * B_WIDTH + 2 * KV_WIDTH))
    if rope:
        cos, sa, sb = cos_ref[...], sa_ref[...], sb_ref[...]
        q = _rope(q, cos, sa, sb)
        k = _rope(k, cos, sa, sb)
    q = (q * Q_SCALE).astype(BF16)
    gz = _silu(z).astype(BF16)
    v_t = v.T.astype(BF16)
    low = lax.broadcasted_iota(jnp.int32, (1, LANES), 1) < HEAD_DIM
    group_w = B_WIDTH // N_KV_HEADS
    for g in range(N_KV_HEADS):
        q_ref[0, g] = q[:, g * group_w:(g + 1) * group_w]
        gz_ref[0, g] = gz[:, g * group_w:(g + 1) * group_w]
        vt_ref[0, g] = v_t[g * HEAD_DIM:(g + 1) * HEAD_DIM, :]
    for j in range(KV_WIDTH // LANES):
        blk = k[:, j * LANES:(j + 1) * LANES]
        swapped = pltpu.roll(blk, HEAD_DIM, 1)
        kk_ref[0, 2 * j] = jnp.where(low, blk, swapped).astype(BF16)
        kk_ref[0, 2 * j + 1] = jnp.where(low, swapped, blk).astype(BF16)


def _project_b(x, mod, j, w_in, tables, *, tm, rope):
    bsz, length, _ = x.shape
    cos, sa, sb = tables
    group_w = B_WIDTH // N_KV_HEADS
    grouped = lambda width: jax.ShapeDtypeStruct((bsz, N_KV_HEADS, length, width), BF16)
    grouped_spec = lambda width: pl.BlockSpec((1, N_KV_HEADS, tm, width),
                                              lambda b, t: (b, 0, t, 0))
    tab_spec = pl.BlockSpec((tm, LANES), lambda b, t: (t, 0))
    kernel = functools.partial(_project_b_kernel, rope=rope)
    return pl.pallas_call(
        kernel,
        out_shape=(grouped(group_w), grouped(LANES),
                   jax.ShapeDtypeStruct((bsz, N_KV_HEADS, HEAD_DIM, length), BF16),
                   grouped(group_w)),
        grid=(bsz, length // tm),
        in_specs=[
            pl.BlockSpec((1, tm, D_MODEL), lambda b, t: (b, t, 0)),
            pl.BlockSpec((1, 3, D_MODEL), lambda b, t: (b, 0, 0)),
            _layer_spec((D_MODEL // 2, 2 * B_WIDTH + 2 * KV_WIDTH), j),
            tab_spec, tab_spec, tab_spec,
        ],
        out_specs=(grouped_spec(group_w), grouped_spec(LANES),
                   pl.BlockSpec((1, N_KV_HEADS, HEAD_DIM, tm), lambda b, t: (b, 0, 0, t)),
                   grouped_spec(group_w)),
        compiler_params=pltpu.CompilerParams(
            dimension_semantics=("parallel", "parallel"), vmem_limit_bytes=VMEM_LIMIT),
        name="project_b_rope" if rope else "project_b",
    )(x, mod, w_in, cos, sa, sb)


def _attend_b_kernel(*refs, local, n_steps, n_sub):
    if local:
        (sink_ref, q_ref, *tile_refs, kx_ref, vtx_ref, gz_ref, x_ref, mod_ref, w_out_ref,
         g_ref, b_ref, mask_ref, o_ref, sx_ref, m_ref, ot_ref, og_ref, sl_ref) = refs
        k_refs, vt_refs = tile_refs[:n_sub + 2], tile_refs[n_sub + 2:]
    else:
        (sink_ref, q_ref, kx_ref, vtx_ref, gz_ref, x_ref, mod_ref, w_out_ref, g_ref, b_ref,
         o_ref, sx_ref, m_ref, ot_ref, og_ref) = refs

    tq = ATT_BLOCK
    low = lax.broadcasted_iota(jnp.int32, (1, LANES), 1) < HEAD_DIM
    left = lax.broadcasted_iota(jnp.int32, (1, 2 * tq), 1) < tq
    if local:
        i = pl.program_id(1)
        first = jnp.where(i == 0, NEG_INF, 0.0).astype(F32)
        last = jnp.where(i == n_steps - 1, NEG_INF, 0.0).astype(F32)
        bias_prev = [mask_ref[0:WINDOW, :] + (first if sub == 0 else 0.0)
                     for sub in range(n_sub)]
        bias_next = [mask_ref[WINDOW:2 * WINDOW, :] + (last if sub == n_sub - 1 else 0.0)
                     for sub in range(n_sub)]

    def sink_row(g, half):
        return jnp.where(left, sink_ref[4 * g + half], sink_ref[4 * g + 2 + half]) * LOG2E

    units = [(sub, half) for sub in range(n_sub) for half in range(2)]

    def scores(g, sub, half):
        rows = slice(sub * tq, (sub + 1) * tq)
        qp = jnp.concatenate([q_ref[0, g, rows, 0:LANES], q_ref[0, g, rows, LANES:2 * LANES]],
                             axis=0)
        qh = jnp.where(low, qp, jnp.zeros_like(qp)) if half == 0 else \
            jnp.where(low, jnp.zeros_like(qp), qp)
        sx = _dot_nt(kx_ref[0, g], qh)
        sx_ref[g, sub, half] = sx
        m = jnp.max(sx, axis=0, keepdims=True)
        if local:
            kl = jnp.concatenate([k_refs[sub + t][0, g] for t in range(3)], axis=0)
            sl = _dot_nt(kl, qh)
            sl = jnp.concatenate(
                [sl[0:WINDOW] + bias_prev[sub], sl[WINDOW:2 * WINDOW],
                 sl[2 * WINDOW:3 * WINDOW] + bias_next[sub]], axis=0)
            sl_ref[g, sub, half] = sl
            m = jnp.maximum(m, jnp.max(sl, axis=0, keepdims=True))
        m_ref[g, sub, half] = jnp.maximum(m, sink_row(g, half))

    def with_ones(vt):
        return jnp.concatenate([vt, jnp.ones((ONES_ROWS, vt.shape[1]), BF16)], axis=0)

    def attend(g, sub, half):
        m = m_ref[g, sub, half]
        px = jnp.exp2(sx_ref[g, sub, half] - m)
        o = _dot(with_ones(vtx_ref[0, g]), px.astype(BF16))
        if local:
            vt = jnp.concatenate([vt_refs[sub + t][0, g] for t in range(3)], axis=1)
            pl_ = jnp.exp2(sl_ref[g, sub, half] - m)
            o = o + _dot(with_ones(vt), pl_.astype(BF16))
        denom = o[HEAD_DIM:HEAD_DIM + 1] + jnp.exp2(sink_row(g, half) - m)
        ot_ref[g, sub, half] = o[0:HEAD_DIM] * (1.0 / denom)

    for sub, half in units:
        scores(0, sub, half)

    def body(g, carry):
        for sub, half in units:
            attend(g, sub, half)
            scores(g + 1, sub, half)
        return carry

    lax.fori_loop(0, N_KV_HEADS - 1, body, 0)
    for sub, half in units:
        attend(N_KV_HEADS - 1, sub, half)

    out = None
    for g in range(N_KV_HEADS):
        for sub in range(n_sub):
            rows = slice(sub * tq, (sub + 1) * tq)
            o_t = jnp.concatenate([ot_ref[g, sub, 0], ot_ref[g, sub, 1]], axis=0)
            for r in range(2):
                oc = slice(r * LANES, (r + 1) * LANES)
                o_blk = o_t[:, r * tq:(r + 1) * tq].T
                og_ref[g, rows, oc] = (o_blk * gz_ref[0, g, rows, oc].astype(F32)).astype(BF16)
        part = _dot(og_ref[g], _unpack_rows(w_out_ref[0, g]))
        out = part if out is None else out + part
    o_ref[0] = _post_norm(x_ref[0], out, mod_ref, g_ref, b_ref)


def _attend_b(q, kk, vt, kkx, vtx, gz, x, mod, j, w_out, sink, ln_g, ln_b, mask, *, local):
    bsz, _, length, group_w = q.shape
    n_ctx = kkx.shape[2]
    tq = ATT_BLOCK
    n_tiles = length // tq
    n_sub = min(ATT_SUB, n_tiles)
    rows = n_sub * tq
    n_steps = length // rows
    grouped_spec = pl.BlockSpec((1, N_KV_HEADS, rows, group_w), lambda b, i: (b, 0, i, 0))
    x_spec = pl.BlockSpec((1, rows, D_MODEL), lambda b, i: (b, i, 0))
    tile = lambda t: (lambda i: jnp.clip(n_sub * i + t - 1, 0, n_tiles - 1))
    k_specs = [pl.BlockSpec((1, N_KV_HEADS, tq, LANES), lambda b, i, f=tile(t): (b, 0, f(i), 0))
               for t in range(n_sub + 2)]
    vt_specs = [pl.BlockSpec((1, N_KV_HEADS, HEAD_DIM, tq),
                             lambda b, i, f=tile(t): (b, 0, 0, f(i))) for t in range(n_sub + 2)]
    in_specs = [pl.BlockSpec(memory_space=pltpu.SMEM), grouped_spec]
    args = [sink, q]
    if local:
        in_specs += k_specs + vt_specs
        args += [kk] * (n_sub + 2) + [vt] * (n_sub + 2)
    in_specs += [pl.BlockSpec((1, N_KV_HEADS, n_ctx, LANES), lambda b, i: (b, 0, 0, 0)),
                 pl.BlockSpec((1, N_KV_HEADS, HEAD_DIM, n_ctx), lambda b, i: (b, 0, 0, 0))]
    args += [kkx, vtx]
    in_specs += [grouped_spec, x_spec, pl.BlockSpec((1, 3, D_MODEL), lambda b, i: (b, 0, 0)),
                 _layer_spec((N_KV_HEADS, group_w // 2, D_MODEL), j), _const_spec((1, D_MODEL)),
                 _const_spec((1, D_MODEL))]
    args += [gz, x, mod, w_out, ln_g, ln_b]
    scratch = [pltpu.VMEM((N_KV_HEADS, n_sub, 2, n_ctx, 2 * tq), F32),
               pltpu.VMEM((N_KV_HEADS, n_sub, 2, 1, 2 * tq), F32),
               pltpu.VMEM((N_KV_HEADS, n_sub, 2, HEAD_DIM, 2 * tq), F32),
               pltpu.VMEM((N_KV_HEADS, rows, group_w), BF16)]
    if local:
        in_specs.append(_const_spec((2 * WINDOW, 2 * ATT_BLOCK)))
        args.append(mask)
        scratch.append(pltpu.VMEM((N_KV_HEADS, n_sub, 2, 3 * WINDOW, 2 * tq), F32))
    kernel = functools.partial(_attend_b_kernel, local=local, n_steps=n_steps, n_sub=n_sub)
    return pl.pallas_call(
        kernel,
        out_shape=jax.ShapeDtypeStruct(x.shape, F32),
        grid=(bsz, n_steps),
        in_specs=in_specs,
        out_specs=x_spec,
        scratch_shapes=scratch,
        compiler_params=pltpu.CompilerParams(
            dimension_semantics=("parallel", "parallel"), vmem_limit_bytes=VMEM_LIMIT),
        name="attend_b_local" if local else "attend_b_ctx",
    )(*args)


def _rope_tables(length):
    rows = length // GRID_W
    row = jnp.repeat(jnp.arange(rows), GRID_W).astype(F32)
    col = jnp.tile(jnp.arange(GRID_W), rows).astype(F32)
    n_freq = ROPE_AXIS_DIM // 2
    inv = ROPE_BASE ** (-jnp.arange(n_freq, dtype=F32) / n_freq)
    ang_r, ang_c = row[:, None] * inv[None, :], col[:, None] * inv[None, :]
    cr, sr, cc, sc = jnp.cos(ang_r), jnp.sin(ang_r), jnp.cos(ang_c), jnp.sin(ang_c)
    zero = jnp.zeros_like(sr)
    reps = LANES // HEAD_DIM
    cos = jnp.tile(jnp.concatenate([cr, cr, cc, cc], axis=1), (1, reps))
    sin_a = jnp.tile(jnp.concatenate([-sr, zero, -sc, zero], axis=1), (1, reps))
    sin_b = jnp.tile(jnp.concatenate([zero, sr, zero, sc], axis=1), (1, reps))
    return cos, sin_a, sin_b


def _band_mask():
    kj = jnp.arange(WINDOW)[:, None]
    qi = jnp.arange(ATT_BLOCK)[None, :]
    prev = jnp.where(kj >= qi, 0.0, NEG_INF)
    nxt = jnp.where(kj <= qi, 0.0, NEG_INF)
    return jnp.tile(jnp.concatenate([prev, nxt], axis=0), (1, 2)).astype(F32)


def kernel(x, c, ctx, c_ctx, ada_w, ada_b, ln_g, ln_b, a_w_in, a_ln_g, a_ln_b, a_w_s, a_b_s,
           a_w_out, b_w_in, b_sink, b_w_out):
    bsz, length, _ = x.shape
    n_ctx = ctx.shape[1]
    assert bsz + 1 <= MOD_ROWS

    c_rows = jnp.zeros((MOD_ROWS, D_MODEL), F32).at[:bsz].set(c).at[bsz].set(c_ctx)
    mod_all = _modulation(c_rows, ada_w, ada_b)
    tables = _rope_tables(length)
    mask = _band_mask()
    a_w_in_h, a_w_out_h, b_w_in_h = _pack_rows(a_w_in), _pack_rows(a_w_out), _pack_rows(b_w_in)
    a_w_s_h = a_w_s.astype(BF16)
    b_w_out_h = _pack_rows(
        b_w_out.reshape(-1, N_KV_HEADS, B_WIDTH // N_KV_HEADS, D_MODEL))

    for i in range(DEPTH):
        j = i // N_MIXERS
        need_ctx_out = i < DEPTH - 1
        mod = mod_all[i, :bsz].reshape(bsz, 3, D_MODEL)
        mod_c = jnp.broadcast_to(mod_all[i, bsz].reshape(1, 3, D_MODEL), (bsz, 3, D_MODEL))
        g_i, b_i = ln_g[i].reshape(1, D_MODEL), ln_b[i].reshape(1, D_MODEL)
        if i % N_MIXERS == 0:
            b_s_full = jnp.repeat(a_b_s[j].T, A_GW, axis=1)
            weights = (j, a_w_in_h, a_ln_g[j].reshape(1, A_WIDTH), a_ln_b[j].reshape(1, A_WIDTH),
                       a_w_s_h, b_s_full, a_w_out_h, g_i, b_i)
            x_new = _layer_a(x, mod, *weights, tm=512)
            if need_ctx_out:
                ctx = _layer_a(ctx, mod_c, *weights, tm=n_ctx)
            x = x_new
        else:
            sink = b_sink[j].astype(F32)
            q, kk, vt, gz = _project_b(x, mod, j, b_w_in_h, tables, tm=512, rope=True)
            qc, kkc, vtc, gzc = _project_b(ctx, mod_c, j, b_w_in_h, tables, tm=n_ctx, rope=False)
            x = _attend_b(q, kk, vt, kkc, vtc, gz, x, mod, j, b_w_out_h, sink, g_i, b_i, mask,
                          local=True)
            if need_ctx_out:
                ctx = _attend_b(qc, None, None, kkc, vtc, gzc, ctx, mod_c, j, b_w_out_h, sink,
                                g_i, b_i, None, local=False)
    return x
```

```python
import functools
import math

import jax
import jax.numpy as jnp
from jax import lax
from jax.experimental import pallas as pl
from jax.experimental.pallas import tpu as pltpu

D_MODEL = 1024
DEPTH = 4
GRID_W = 64
N_MIXERS = 2
CHUNK = 128
A_WIDTH = 2 * D_MODEL
A_GROUPS = 8
A_GW = A_WIDTH // A_GROUPS
HEAD_DIM = 64
N_HEADS = D_MODEL // HEAD_DIM
N_KV_HEADS = N_HEADS // 4
B_WIDTH = N_HEADS * HEAD_DIM
KV_WIDTH = N_KV_HEADS * HEAD_DIM
WINDOW = 128
ATT_BLOCK = 128
ATT_SUB = 4
ROPE_AXIS_DIM = HEAD_DIM // 2
ROPE_BASE = 10000.0
LN_EPS = 1e-5
NEG_INF = -1e30
ALPHA = (2.0 * DEPTH) ** 0.25

LANES = 128
MOD_ROWS = 8
VMEM_LIMIT = 48 * 1024 * 1024

BF16 = jnp.bfloat16
F32 = jnp.float32

LOG2E = math.log2(math.e)
Q_SCALE = HEAD_DIM ** -0.5 * LOG2E
ONES_ROWS = 16
A_SUB_ROWS = 256
COPY_ROWS = 128

_GELU_C0 = math.sqrt(2.0 / math.pi)
_GELU_C1 = _GELU_C0 * 0.044715


def _dot(a, b):
    return jnp.dot(a, b, preferred_element_type=F32)


def _dot_nt(a, b):
    return lax.dot_general(a, b, (((1,), (1,)), ((), ())), preferred_element_type=F32)


def _gelu(x):
    inner = x * (_GELU_C1 * (x * x) + _GELU_C0)
    hx = 0.5 * x
    return hx + hx * jnp.tanh(inner)


def _silu(z):
    hz = 0.5 * z
    return hz + hz * jnp.tanh(hz)


def _layer_norm(x, g, b):
    mu = jnp.mean(x, axis=-1, keepdims=True)
    xc = x - mu
    var = jnp.mean(xc * xc, axis=-1, keepdims=True)
    return xc * lax.rsqrt(var + LN_EPS) * g + b


def _const_spec(shape):
    nd = len(shape)
    return pl.BlockSpec(shape, lambda *_: (0,) * nd, pipeline_mode=pl.Buffered(1))


def _layer_spec(shape, j):
    nd = len(shape)
    return pl.BlockSpec((1,) + tuple(shape), lambda *_: (j,) + (0,) * nd,
                        pipeline_mode=pl.Buffered(1))


def _modulation_kernel(c_ref, w_ref, b_ref, o_ref):
    c = c_ref[...]
    cond = _silu(c).astype(BF16)
    o_ref[0] = _dot(cond, w_ref[0].astype(BF16)) + b_ref[0]


def _modulation(c_rows, ada_w, ada_b):
    tn = D_MODEL
    return pl.pallas_call(
        _modulation_kernel,
        out_shape=jax.ShapeDtypeStruct((DEPTH, MOD_ROWS, 3 * D_MODEL), F32),
        grid=(DEPTH, 3 * D_MODEL // tn),
        in_specs=[
            pl.BlockSpec((MOD_ROWS, D_MODEL), lambda i, j: (0, 0)),
            pl.BlockSpec((1, D_MODEL, tn), lambda i, j: (i, 0, j)),
            pl.BlockSpec((1, 1, tn), lambda i, j: (i, 0, j)),
        ],
        out_specs=pl.BlockSpec((1, MOD_ROWS, tn), lambda i, j: (i, 0, j)),
        compiler_params=pltpu.CompilerParams(
            dimension_semantics=("parallel", "parallel"), vmem_limit_bytes=VMEM_LIMIT),
        name="modulation",
    )(c_rows, ada_w, ada_b.reshape(DEPTH, 1, 3 * D_MODEL))


def _modulate(x, mod_ref):
    shift = mod_ref[0, 0:1, :]
    scale = mod_ref[0, 1:2, :]
    return x * (1.0 + scale) + shift


def _post_norm(x, out, mod_ref, g_ref, b_ref):
    gate = mod_ref[0, 2:3, :]
    return _layer_norm(ALPHA * x + gate * out, g_ref[...], b_ref[...])


def _layer_a_kernel(x_ref, mod_ref, w_in_ref, lng_ref, lnb_ref, ws_ref, bs_ref, w_out_ref,
                    g_ref, b_ref, o_ref, vn_ref, y_ref, w_in_s, w_out_s, *, tm, ts):
    @pl.when((pl.program_id(0) == 0) & (pl.program_id(1) == 0))
    def _():
        def copy_rows(ref, dst, r):
            rows = pl.ds(pl.multiple_of(r * COPY_ROWS, COPY_ROWS), COPY_ROWS)
            dst[rows, :] = ref[0, rows, :]

        lax.fori_loop(0, D_MODEL // COPY_ROWS,
                      lambda r, c: (copy_rows(w_in_ref, w_in_s, r), c)[1], 0)
        lax.fori_loop(0, A_WIDTH // COPY_ROWS,
                      lambda r, c: (copy_rows(w_out_ref, w_out_s, r), c)[1], 0)

    for t in range(tm // ts):
        sub = slice(t * ts, (t + 1) * ts)
        x = x_ref[0, sub, :]
        h = _modulate(x, mod_ref).astype(BF16)
        w_in = lambda lo, hi: w_in_s[:, lo:hi]
        v = _gelu(_dot(h, w_in(A_WIDTH, 2 * A_WIDTH)))
        vn_ref[sub, :] = _layer_norm(v, lng_ref[...], lnb_ref[...]).astype(BF16)
        for g in range(A_GROUPS):
            cols = slice(g * A_GW, (g + 1) * A_GW)
            u = _gelu(_dot(h, w_in(g * A_GW, (g + 1) * A_GW)))
            z = _dot(h, w_in(2 * A_WIDTH + g * A_GW, 2 * A_WIDTH + (g + 1) * A_GW))
            uz = u * _silu(z)
            for c in range(ts // CHUNK):
                rows = slice(t * ts + c * CHUNK, t * ts + (c + 1) * CHUNK)
                s = _dot(ws_ref[0, g], vn_ref[rows, cols]) + bs_ref[:, cols]
                y_ref[rows, cols] = (uz[c * CHUNK:(c + 1) * CHUNK] * s).astype(BF16)
        out = _dot(y_ref[sub, :], w_out_s[...])
        o_ref[0, sub, :] = _post_norm(x, out, mod_ref, g_ref, b_ref)


def _layer_a(x, mod, j, w_in, a_ln_g, a_ln_b, w_s, b_s_full, w_out, ln_g, ln_b, *, tm):
    bsz, length, _ = x.shape
    kernel = functools.partial(_layer_a_kernel, tm=tm, ts=min(tm, A_SUB_ROWS))
    return pl.pallas_call(
        kernel,
        out_shape=jax.ShapeDtypeStruct(x.shape, F32),
        grid=(bsz, length // tm),
        in_specs=[
            pl.BlockSpec((1, tm, D_MODEL), lambda b, t: (b, t, 0)),
            pl.BlockSpec((1, 3, D_MODEL), lambda b, t: (b, 0, 0)),
            _layer_spec((D_MODEL, 3 * A_WIDTH), j),
            _const_spec((1, A_WIDTH)),
            _const_spec((1, A_WIDTH)),
            _layer_spec((A_GROUPS, CHUNK, CHUNK), j),
            _const_spec((CHUNK, A_WIDTH)),
            _layer_spec((A_WIDTH, D_MODEL), j),
            _const_spec((1, D_MODEL)),
            _const_spec((1, D_MODEL)),
        ],
        out_specs=pl.BlockSpec((1, tm, D_MODEL), lambda b, t: (b, t, 0)),
        scratch_shapes=[pltpu.VMEM((tm, A_WIDTH), BF16), pltpu.VMEM((tm, A_WIDTH), BF16),
                        pltpu.VMEM((D_MODEL, 3 * A_WIDTH), BF16),
                        pltpu.VMEM((A_WIDTH, D_MODEL), BF16)],
        compiler_params=pltpu.CompilerParams(
            dimension_semantics=("arbitrary", "arbitrary"), vmem_limit_bytes=VMEM_LIMIT),
        name="layer_a",
    )(x, mod, w_in, a_ln_g, a_ln_b, w_s, b_s_full, w_out, ln_g, ln_b)


def _rope(t, cos, sin_a, sin_b):
    n = t.shape[1] // LANES
    half = ROPE_AXIS_DIM // 2
    outs = []
    for j in range(n):
        blk = t[:, j * LANES:(j + 1) * LANES]
        up = pltpu.roll(blk, LANES - half, 1)
        dn = pltpu.roll(blk, half, 1)
        outs.append(blk * cos + up * sin_a + dn * sin_b)
    return jnp.concatenate(outs, axis=1)


def _project_b_kernel(x_ref, mod_ref, w_ref, cos_ref, sa_ref, sb_ref,
                      q_ref, kk_ref, vt_ref, gz_ref, *, rope):
    x = x_ref[0]
    h = _modulate(x, mod_ref).astype(BF16)
    w = lambda lo, hi: w_ref[0, :, lo:hi]
    q = _dot(h, w(0, B_WIDTH))
    k = _dot(h, w(B_WIDTH, B_WIDTH + KV_WIDTH))
    v = _dot(h, w(B_WIDTH + KV_WIDTH, B_WIDTH + 2 * KV_WIDTH))
    z = _dot(h, w(B_WIDTH + 2 * KV_WIDTH, 2 * B_WIDTH + 2 * KV_WIDTH))
    if rope:
        cos, sa, sb = cos_ref[...], sa_ref[...], sb_ref[...]
        q = _rope(q, cos, sa, sb)
        k = _rope(k, cos, sa, sb)
    q = (q * Q_SCALE).astype(BF16)
    gz = _silu(z).astype(BF16)
    v_t = v.T.astype(BF16)
    low = lax.broadcasted_iota(jnp.int32, (1, LANES), 1) < HEAD_DIM
    group_w = B_WIDTH // N_KV_HEADS
    for g in range(N_KV_HEADS):
        q_ref[0, g] = q[:, g * group_w:(g + 1) * group_w]
        gz_ref[0, g] = gz[:, g * group_w:(g + 1) * group_w]
        vt_ref[0, g] = v_t[g * HEAD_DIM:(g + 1) * HEAD_DIM, :]
    for j in range(KV_WIDTH // LANES):
        blk = k[:, j * LANES:(j + 1) * LANES]
        swapped = pltpu.roll(blk, HEAD_DIM, 1)
        kk_ref[0, 2 * j] = jnp.where(low, blk, swapped).astype(BF16)
        kk_ref[0, 2 * j + 1] = jnp.where(low, swapped, blk).astype(BF16)


def _project_b(x, mod, j, w_in, tables, *, tm, rope):
    bsz, length, _ = x.shape
    cos, sa, sb = tables
    group_w = B_WIDTH // N_KV_HEADS
    grouped = lambda width: jax.ShapeDtypeStruct((bsz, N_KV_HEADS, length, width), BF16)
    grouped_spec = lambda width: pl.BlockSpec((1, N_KV_HEADS, tm, width),
                                              lambda b, t: (b, 0, t, 0))
    tab_spec = pl.BlockSpec((tm, LANES), lambda b, t: (t, 0))
    kernel = functools.partial(_project_b_kernel, rope=rope)
    return pl.pallas_call(
        kernel,
        out_shape=(grouped(group_w), grouped(LANES),
                   jax.ShapeDtypeStruct((bsz, N_KV_HEADS, HEAD_DIM, length), BF16),
                   grouped(group_w)),
        grid=(bsz, length // tm),
        in_specs=[
            pl.BlockSpec((1, tm, D_MODEL), lambda b, t: (b, t, 0)),
            pl.BlockSpec((1, 3, D_MODEL), lambda b, t: (b, 0, 0)),
            _layer_spec((D_MODEL, 2 * B_WIDTH + 2 * KV_WIDTH), j),
            tab_spec, tab_spec, tab_spec,
        ],
        out_specs=(grouped_spec(group_w), grouped_spec(LANES),
                   pl.BlockSpec((1, N_KV_HEADS, HEAD_DIM, tm), lambda b, t: (b, 0, 0, t)),
                   grouped_spec(group_w)),
        compiler_params=pltpu.CompilerParams(
            dimension_semantics=("parallel", "parallel"), vmem_limit_bytes=VMEM_LIMIT),
        name="project_b_rope" if rope else "project_b",
    )(x, mod, w_in, cos, sa, sb)


def _attend_b_kernel(*refs, local, n_steps, n_sub):
    if local:
        (sink_ref, q_ref, *tile_refs, kx_ref, vtx_ref, gz_ref, x_ref, mod_ref, w_out_ref,
         g_ref, b_ref, mask_ref, o_ref, sx_ref, m_ref, ot_ref, og_ref, sl_ref) = refs
        k_refs, vt_refs = tile_refs[:n_sub + 2], tile_refs[n_sub + 2:]
    else:
        (sink_ref, q_ref, kx_ref, vtx_ref, gz_ref, x_ref, mod_ref, w_out_ref, g_ref, b_ref,
         o_ref, sx_ref, m_ref, ot_ref, og_ref) = refs

    tq = ATT_BLOCK
    low = lax.broadcasted_iota(jnp.int32, (1, LANES), 1) < HEAD_DIM
    left = lax.broadcasted_iota(jnp.int32, (1, 2 * tq), 1) < tq
    if local:
        i = pl.program_id(1)
        first = jnp.where(i == 0, NEG_INF, 0.0).astype(F32)
        last = jnp.where(i == n_steps - 1, NEG_INF, 0.0).astype(F32)
        bias_prev = [mask_ref[0:WINDOW, :] + (first if sub == 0 else 0.0)
                     for sub in range(n_sub)]
        bias_next = [mask_ref[WINDOW:2 * WINDOW, :] + (last if sub == n_sub - 1 else 0.0)
                     for sub in range(n_sub)]

    def sink_row(g, half):
        return jnp.where(left, sink_ref[4 * g + half], sink_ref[4 * g + 2 + half]) * LOG2E

    units = [(sub, half) for sub in range(n_sub) for half in range(2)]

    def scores(g, sub, half):
        rows = slice(sub * tq, (sub + 1) * tq)
        qp = jnp.concatenate([q_ref[0, g, rows, 0:LANES], q_ref[0, g, rows, LANES:2 * LANES]],
                             axis=0)
        qh = jnp.where(low, qp, jnp.zeros_like(qp)) if half == 0 else \
            jnp.where(low, jnp.zeros_like(qp), qp)
        sx = _dot_nt(kx_ref[0, g], qh)
        sx_ref[g, sub, half] = sx
        m = jnp.max(sx, axis=0, keepdims=True)
        if local:
            kl = jnp.concatenate([k_refs[sub + t][0, g] for t in range(3)], axis=0)
            sl = _dot_nt(kl, qh)
            sl = jnp.concatenate(
                [sl[0:WINDOW] + bias_prev[sub], sl[WINDOW:2 * WINDOW],
                 sl[2 * WINDOW:3 * WINDOW] + bias_next[sub]], axis=0)
            sl_ref[g, sub, half] = sl
            m = jnp.maximum(m, jnp.max(sl, axis=0, keepdims=True))
        m_ref[g, sub, half] = jnp.maximum(m, sink_row(g, half))

    def with_ones(vt):
        return jnp.concatenate([vt, jnp.ones((ONES_ROWS, vt.shape[1]), BF16)], axis=0)

    def attend(g, sub, half):
        m = m_ref[g, sub, half]
        px = jnp.exp2(sx_ref[g, sub, half] - m)
        o = _dot(with_ones(vtx_ref[0, g]), px.astype(BF16))
        if local:
            vt = jnp.concatenate([vt_refs[sub + t][0, g] for t in range(3)], axis=1)
            pl_ = jnp.exp2(sl_ref[g, sub, half] - m)
            o = o + _dot(with_ones(vt), pl_.astype(BF16))
        denom = o[HEAD_DIM:HEAD_DIM + 1] + jnp.exp2(sink_row(g, half) - m)
        ot_ref[g, sub, half] = o[0:HEAD_DIM] * (1.0 / denom)

    for sub, half in units:
        scores(0, sub, half)

    def body(g, carry):
        for sub, half in units:
            attend(g, sub, half)
            scores(g + 1, sub, half)
        return carry

    lax.fori_loop(0, N_KV_HEADS - 1, body, 0)
    for sub, half in units:
        attend(N_KV_HEADS - 1, sub, half)

    out = None
    for g in range(N_KV_HEADS):
        for sub in range(n_sub):
            rows = slice(sub * tq, (sub + 1) * tq)
            o_t = jnp.concatenate([ot_ref[g, sub, 0], ot_ref[g, sub, 1]], axis=0)
            for r in range(2):
                oc = slice(r * LANES, (r + 1) * LANES)
                o_blk = o_t[:, r * tq:(r + 1) * tq].T
                og_ref[g, rows, oc] = (o_blk * gz_ref[0, g, rows, oc].astype(F32)).astype(BF16)
        part = _dot(og_ref[g], w_out_ref[0, g])
        out = part if out is None else out + part
    o_ref[0] = _post_norm(x_ref[0], out, mod_ref, g_ref, b_ref)


def _attend_b(q, kk, vt, kkx, vtx, gz, x, mod, j, w_out, sink, ln_g, ln_b, mask, *, local):
    bsz, _, length, group_w = q.shape
    n_ctx = kkx.shape[2]
    tq = ATT_BLOCK
    n_tiles = length // tq
    n_sub = min(ATT_SUB, n_tiles)
    rows = n_sub * tq
    n_steps = length // rows
    grouped_spec = pl.BlockSpec((1, N_KV_HEADS, rows, group_w), lambda b, i: (b, 0, i, 0))
    x_spec = pl.BlockSpec((1, rows, D_MODEL), lambda b, i: (b, i, 0))
    tile = lambda t: (lambda i: jnp.clip(n_sub * i + t - 1, 0, n_tiles - 1))
    k_specs = [pl.BlockSpec((1, N_KV_HEADS, tq, LANES), lambda b, i, f=tile(t): (b, 0, f(i), 0))
               for t in range(n_sub + 2)]
    vt_specs = [pl.BlockSpec((1, N_KV_HEADS, HEAD_DIM, tq),
                             lambda b, i, f=tile(t): (b, 0, 0, f(i))) for t in range(n_sub + 2)]
    in_specs = [pl.BlockSpec(memory_space=pltpu.SMEM), grouped_spec]
    args = [sink, q]
    if local:
        in_specs += k_specs + vt_specs
        args += [kk] * (n_sub + 2) + [vt] * (n_sub + 2)
    in_specs += [pl.BlockSpec((1, N_KV_HEADS, n_ctx, LANES), lambda b, i: (b, 0, 0, 0)),
                 pl.BlockSpec((1, N_KV_HEADS, HEAD_DIM, n_ctx), lambda b, i: (b, 0, 0, 0))]
    args += [kkx, vtx]
    in_specs += [grouped_spec, x_spec, pl.BlockSpec((1, 3, D_MODEL), lambda b, i: (b, 0, 0)),
                 _layer_spec((N_KV_HEADS, group_w, D_MODEL), j), _const_spec((1, D_MODEL)),
                 _const_spec((1, D_MODEL))]
    args += [gz, x, mod, w_out, ln_g, ln_b]
    scratch = [pltpu.VMEM((N_KV_HEADS, n_sub, 2, n_ctx, 2 * tq), F32),
               pltpu.VMEM((N_KV_HEADS, n_sub, 2, 1, 2 * tq), F32),
               pltpu.VMEM((N_KV_HEADS, n_sub, 2, HEAD_DIM, 2 * tq), F32),
               pltpu.VMEM((N_KV_HEADS, rows, group_w), BF16)]
    if local:
        in_specs.append(_const_spec((2 * WINDOW, 2 * ATT_BLOCK)))
        args.append(mask)
        scratch.append(pltpu.VMEM((N_KV_HEADS, n_sub, 2, 3 * WINDOW, 2 * tq), F32))
    kernel = functools.partial(_attend_b_kernel, local=local, n_steps=n_steps, n_sub=n_sub)
    return pl.pallas_call(
        kernel,
        out_shape=jax.ShapeDtypeStruct(x.shape, F32),
        grid=(bsz, n_steps),
        in_specs=in_specs,
        out_specs=x_spec,
        scratch_shapes=scratch,
        compiler_params=pltpu.CompilerParams(
            dimension_semantics=("parallel", "parallel"), vmem_limit_bytes=VMEM_LIMIT),
        name="attend_b_local" if local else "attend_b_ctx",
    )(*args)


def _rope_tables(length):
    rows = length // GRID_W
    row = jnp.repeat(jnp.arange(rows), GRID_W).astype(F32)
    col = jnp.tile(jnp.arange(GRID_W), rows).astype(F32)
    n_freq = ROPE_AXIS_DIM // 2
    inv = ROPE_BASE ** (-jnp.arange(n_freq, dtype=F32) / n_freq)
    ang_r, ang_c = row[:, None] * inv[None, :], col[:, None] * inv[None, :]
    cr, sr, cc, sc = jnp.cos(ang_r), jnp.sin(ang_r), jnp.cos(ang_c), jnp.sin(ang_c)
    zero = jnp.zeros_like(sr)
    reps = LANES // HEAD_DIM
    cos = jnp.tile(jnp.concatenate([cr, cr, cc, cc], axis=1), (1, reps))
    sin_a = jnp.tile(jnp.concatenate([-sr, zero, -sc, zero], axis=1), (1, reps))
    sin_b = jnp.tile(jnp.concatenate([zero, sr, zero, sc], axis=1), (1, reps))
    return cos, sin_a, sin_b


def _band_mask():
    kj = jnp.arange(WINDOW)[:, None]
    qi = jnp.arange(ATT_BLOCK)[None, :]
    prev = jnp.where(kj >= qi, 0.0, NEG_INF)
    nxt = jnp.where(kj <= qi, 0.0, NEG_INF)
    return jnp.tile(jnp.concatenate([prev, nxt], axis=0), (1, 2)).astype(F32)


def kernel(x, c, ctx, c_ctx, ada_w, ada_b, ln_g, ln_b, a_w_in, a_ln_g, a_ln_b, a_w_s, a_b_s,
           a_w_out, b_w_in, b_sink, b_w_out):
    bsz, length, _ = x.shape
    n_ctx = ctx.shape[1]
    assert bsz + 1 <= MOD_ROWS

    c_rows = jnp.zeros((MOD_ROWS, D_MODEL), F32).at[:bsz].set(c).at[bsz].set(c_ctx)
    mod_all = _modulation(c_rows, ada_w, ada_b)
    tables = _rope_tables(length)
    mask = _band_mask()
    a_w_in_h, a_w_s_h, a_w_out_h = (w.astype(BF16) for w in (a_w_in, a_w_s, a_w_out))
    b_w_in_h = b_w_in.astype(BF16)
    b_w_out_h = b_w_out.astype(BF16).reshape(-1, N_KV_HEADS, B_WIDTH // N_KV_HEADS, D_MODEL)

    for i in range(DEPTH):
        j = i // N_MIXERS
        need_ctx_out = i < DEPTH - 1
        mod = mod_all[i, :bsz].reshape(bsz, 3, D_MODEL)
        mod_c = jnp.broadcast_to(mod_all[i, bsz].reshape(1, 3, D_MODEL), (bsz, 3, D_MODEL))
        g_i, b_i = ln_g[i].reshape(1, D_MODEL), ln_b[i].reshape(1, D_MODEL)
        if i % N_MIXERS == 0:
            b_s_full = jnp.repeat(a_b_s[j].T, A_GW, axis=1)
            weights = (j, a_w_in_h, a_ln_g[j].reshape(1, A_WIDTH), a_ln_b[j].reshape(1, A_WIDTH),
                       a_w_s_h, b_s_full, a_w_out_h, g_i, b_i)
            x_new = _layer_a(x, mod, *weights, tm=512)
            if need_ctx_out:
                ctx = _layer_a(ctx, mod_c, *weights, tm=n_ctx)
            x = x_new
        else:
            sink = b_sink[j].astype(F32)
            q, kk, vt, gz = _project_b(x, mod, j, b_w_in_h, tables, tm=512, rope=True)
            qc, kkc, vtc, gzc = _project_b(ctx, mod_c, j, b_w_in_h, tables, tm=n_ctx, rope=False)
            x = _attend_b(q, kk, vt, kkc, vtc, gz, x, mod, j, b_w_out_h, sink, g_i, b_i, mask,
                          local=True)
            if need_ctx_out:
                ctx = _attend_b(qc, None, None, kkc, vtc, gzc, ctx, mod_c, j, b_w_out_h, sink,
                                g_i, b_i, None, local=False)
    return x
```

```python
import functools
import math

import jax
import jax.numpy as jnp
from jax import lax
from jax.experimental import pallas as pl
from jax.experimental.pallas import tpu as pltpu

D_MODEL = 1024
DEPTH = 4
GRID_W = 64
N_MIXERS = 2
CHUNK = 128
A_WIDTH = 2 * D_MODEL
A_GROUPS = 8
A_GW = A_WIDTH // A_GROUPS
HEAD_DIM = 64
N_HEADS = D_MODEL // HEAD_DIM
N_KV_HEADS = N_HEADS // 4
B_WIDTH = N_HEADS * HEAD_DIM
KV_WIDTH = N_KV_HEADS * HEAD_DIM
WINDOW = 128
ATT_BLOCK = 128
ATT_SUB = 4
ROPE_AXIS_DIM = HEAD_DIM // 2
ROPE_BASE = 10000.0
LN_EPS = 1e-5
NEG_INF = -1e30
ALPHA = (2.0 * DEPTH) ** 0.25

LANES = 128
MOD_ROWS = 8
VMEM_LIMIT = 48 * 1024 * 1024

BF16 = jnp.bfloat16
F32 = jnp.float32

LOG2E = math.log2(math.e)
Q_SCALE = HEAD_DIM ** -0.5 * LOG2E
ONES_ROWS = 16
A_SUB_ROWS = 256
COPY_ROWS = 128

_GELU_C0 = math.sqrt(2.0 / math.pi)
_GELU_C1 = _GELU_C0 * 0.044715


def _dot(a, b):
    return jnp.dot(a, b, preferred_element_type=F32)


def _dot_nt(a, b):
    return lax.dot_general(a, b, (((1,), (1,)), ((), ())), preferred_element_type=F32)


def _gelu(x):
    inner = x * (_GELU_C1 * (x * x) + _GELU_C0)
    hx = 0.5 * x
    return hx + hx * jnp.tanh(inner)


def _silu(z):
    hz = 0.5 * z
    return hz + hz * jnp.tanh(hz)


def _layer_norm(x, g, b):
    mu = jnp.mean(x, axis=-1, keepdims=True)
    xc = x - mu
    var = jnp.mean(xc * xc, axis=-1, keepdims=True)
    return xc * lax.rsqrt(var + LN_EPS) * g + b


def _const_spec(shape):
    nd = len(shape)
    return pl.BlockSpec(shape, lambda *_: (0,) * nd, pipeline_mode=pl.Buffered(1))


def _layer_spec(shape, j):
    nd = len(shape)
    return pl.BlockSpec((1,) + tuple(shape), lambda *_: (j,) + (0,) * nd,
                        pipeline_mode=pl.Buffered(1))


def _modulation_kernel(c_ref, w_ref, b_ref, o_ref):
    c = c_ref[...]
    cond = _silu(c).astype(BF16)
    o_ref[0] = _dot(cond, w_ref[0].astype(BF16)) + b_ref[0]


def _modulation(c_rows, ada_w, ada_b):
    tn = D_MODEL
    return pl.pallas_call(
        _modulation_kernel,
        out_shape=jax.ShapeDtypeStruct((DEPTH, MOD_ROWS, 3 * D_MODEL), F32),
        grid=(DEPTH, 3 * D_MODEL // tn),
        in_specs=[
            pl.BlockSpec((MOD_ROWS, D_MODEL), lambda i, j: (0, 0)),
            pl.BlockSpec((1, D_MODEL, tn), lambda i, j: (i, 0, j)),
            pl.BlockSpec((1, 1, tn), lambda i, j: (i, 0, j)),
        ],
        out_specs=pl.BlockSpec((1, MOD_ROWS, tn), lambda i, j: (i, 0, j)),
        compiler_params=pltpu.CompilerParams(
            dimension_semantics=("parallel", "parallel"), vmem_limit_bytes=VMEM_LIMIT),
        name="modulation",
    )(c_rows, ada_w, ada_b.reshape(DEPTH, 1, 3 * D_MODEL))


def _modulate(x, mod_ref):
    shift = mod_ref[0, 0:1, :]
    scale = mod_ref[0, 1:2, :]
    return x * (1.0 + scale) + shift


def _post_norm(x, out, mod_ref, g_ref, b_ref):
    gate = mod_ref[0, 2:3, :]
    return _layer_norm(ALPHA * x + gate * out, g_ref[...], b_ref[...])


def _layer_a_kernel(x_ref, mod_ref, w_in_ref, lng_ref, lnb_ref, ws_ref, bs_ref, w_out_ref,
                    g_ref, b_ref, o_ref, vn_ref, y_ref, w_in_s, w_out_s, *, tm, ts):
    @pl.when((pl.program_id(0) == 0) & (pl.program_id(1) == 0))
    def _():
        def copy_rows(ref, dst, r):
            rows = pl.ds(pl.multiple_of(r * COPY_ROWS, COPY_ROWS), COPY_ROWS)
            dst[rows, :] = ref[0, rows, :]

        lax.fori_loop(0, D_MODEL // COPY_ROWS,
                      lambda r, c: (copy_rows(w_in_ref, w_in_s, r), c)[1], 0)
        lax.fori_loop(0, A_WIDTH // COPY_ROWS,
                      lambda r, c: (copy_rows(w_out_ref, w_out_s, r), c)[1], 0)

    for t in range(tm // ts):
        sub = slice(t * ts, (t + 1) * ts)
        x = x_ref[0, sub, :]
        h = _modulate(x, mod_ref).astype(BF16)
        w_in = lambda lo, hi: w_in_s[:, lo:hi]
        v = _gelu(_dot(h, w_in(A_WIDTH, 2 * A_WIDTH)))
        vn_ref[sub, :] = _layer_norm(v, lng_ref[...], lnb_ref[...]).astype(BF16)
        for g in range(A_GROUPS):
            cols = slice(g * A_GW, (g + 1) * A_GW)
            u = _gelu(_dot(h, w_in(g * A_GW, (g + 1) * A_GW)))
            z = _dot(h, w_in(2 * A_WIDTH + g * A_GW, 2 * A_WIDTH + (g + 1) * A_GW))
            uz = u * _silu(z)
            for c in range(ts // CHUNK):
                rows = slice(t * ts + c * CHUNK, t * ts + (c + 1) * CHUNK)
                s = _dot(ws_ref[0, g], vn_ref[rows, cols]) + bs_ref[:, cols]
                y_ref[rows, cols] = (uz[c * CHUNK:(c + 1) * CHUNK] * s).astype(BF16)
        out = _dot(y_ref[sub, :], w_out_s[...])
        o_ref[0, sub, :] = _post_norm(x, out, mod_ref, g_ref, b_ref)


def _layer_a(x, mod, j, w_in, a_ln_g, a_ln_b, w_s, b_s_full, w_out, ln_g, ln_b, *, tm):
    bsz, length, _ = x.shape
    kernel = functools.partial(_layer_a_kernel, tm=tm, ts=min(tm, A_SUB_ROWS))
    return pl.pallas_call(
        kernel,
        out_shape=jax.ShapeDtypeStruct(x.shape, F32),
        grid=(bsz, length // tm),
        in_specs=[
            pl.BlockSpec((1, tm, D_MODEL), lambda b, t: (b, t, 0)),
            pl.BlockSpec((1, 3, D_MODEL), lambda b, t: (b, 0, 0)),
            _layer_spec((D_MODEL, 3 * A_WIDTH), j),
            _const_spec((1, A_WIDTH)),
            _const_spec((1, A_WIDTH)),
            _layer_spec((A_GROUPS, CHUNK, CHUNK), j),
            _const_spec((CHUNK, A_WIDTH)),
            _layer_spec((A_WIDTH, D_MODEL), j),
            _const_spec((1, D_MODEL)),
            _const_spec((1, D_MODEL)),
        ],
        out_specs=pl.BlockSpec((1, tm, D_MODEL), lambda b, t: (b, t, 0)),
        scratch_shapes=[pltpu.VMEM((tm, A_WIDTH), BF16), pltpu.VMEM((tm, A_WIDTH), BF16),
                        pltpu.VMEM((D_MODEL, 3 * A_WIDTH), BF16),
                        pltpu.VMEM((A_WIDTH, D_MODEL), BF16)],
        compiler_params=pltpu.CompilerParams(
            dimension_semantics=("parallel", "parallel"), vmem_limit_bytes=VMEM_LIMIT),
        name="layer_a",
    )(x, mod, w_in, a_ln_g, a_ln_b, w_s, b_s_full, w_out, ln_g, ln_b)


def _rope(t, cos, sin_a, sin_b):
    n = t.shape[1] // LANES
    half = ROPE_AXIS_DIM // 2
    outs = []
    for j in range(n):
        blk = t[:, j * LANES:(j + 1) * LANES]
        up = pltpu.roll(blk, LANES - half, 1)
        dn = pltpu.roll(blk, half, 1)
        outs.append(blk * cos + up * sin_a + dn * sin_b)
    return jnp.concatenate(outs, axis=1)


def _project_b_kernel(x_ref, mod_ref, w_ref, cos_ref, sa_ref, sb_ref,
                      q_ref, kk_ref, vt_ref, gz_ref, *, rope):
    x = x_ref[0]
    h = _modulate(x, mod_ref).astype(BF16)
    w = lambda lo, hi: w_ref[0, :, lo:hi]
    q = _dot(h, w(0, B_WIDTH))
    k = _dot(h, w(B_WIDTH, B_WIDTH + KV_WIDTH))
    v = _dot(h, w(B_WIDTH + KV_WIDTH, B_WIDTH + 2 * KV_WIDTH))
    z = _dot(h, w(B_WIDTH + 2 * KV_WIDTH, 2 * B_WIDTH + 2 * KV_WIDTH))
    if rope:
        cos, sa, sb = cos_ref[...], sa_ref[...], sb_ref[...]
        q = _rope(q, cos, sa, sb)
        k = _rope(k, cos, sa, sb)
    q = (q * Q_SCALE).astype(BF16)
    gz = _silu(z).astype(BF16)
    v_t = v.T.astype(BF16)
    low = lax.broadcasted_iota(jnp.int32, (1, LANES), 1) < HEAD_DIM
    group_w = B_WIDTH // N_KV_HEADS
    for g in range(N_KV_HEADS):
        q_ref[0, g] = q[:, g * group_w:(g + 1) * group_w]
        gz_ref[0, g] = gz[:, g * group_w:(g + 1) * group_w]
        vt_ref[0, g] = v_t[g * HEAD_DIM:(g + 1) * HEAD_DIM, :]
    for j in range(KV_WIDTH // LANES):
        blk = k[:, j * LANES:(j + 1) * LANES]
        swapped = pltpu.roll(blk, HEAD_DIM, 1)
        kk_ref[0, 2 * j] = jnp.where(low, blk, swapped).astype(BF16)
        kk_ref[0, 2 * j + 1] = jnp.where(low, swapped, blk).astype(BF16)


def _project_b(x, mod, j, w_in, tables, *, tm, rope):
    bsz, length, _ = x.shape
    cos, sa, sb = tables
    group_w = B_WIDTH // N_KV_HEADS
    grouped = lambda width: jax.ShapeDtypeStruct((bsz, N_KV_HEADS, length, width), BF16)
    grouped_spec = lambda width: pl.BlockSpec((1, N_KV_HEADS, tm, width),
                                              lambda b, t: (b, 0, t, 0))
    tab_spec = pl.BlockSpec((tm, LANES), lambda b, t: (t, 0))
    kernel = functools.partial(_project_b_kernel, rope=rope)
    return pl.pallas_call(
        kernel,
        out_shape=(grouped(group_w), grouped(LANES),
                   jax.ShapeDtypeStruct((bsz, N_KV_HEADS, HEAD_DIM, length), BF16),
                   grouped(group_w)),
        grid=(bsz, length // tm),
        in_specs=[
            pl.BlockSpec((1, tm, D_MODEL), lambda b, t: (b, t, 0)),
            pl.BlockSpec((1, 3, D_MODEL), lambda b, t: (b, 0, 0)),
            _layer_spec((D_MODEL, 2 * B_WIDTH + 2 * KV_WIDTH), j),
            tab_spec, tab_spec, tab_spec,
        ],
        out_specs=(grouped_spec(group_w), grouped_spec(LANES),
                   pl.BlockSpec((1, N_KV_HEADS, HEAD_DIM, tm), lambda b, t: (b, 0, 0, t)),
                   grouped_spec(group_w)),
        compiler_params=pltpu.CompilerParams(
            dimension_semantics=("parallel", "parallel"), vmem_limit_bytes=VMEM_LIMIT),
        name="project_b_rope" if rope else "project_b",
    )(x, mod, w_in, cos, sa, sb)


def _attend_b_kernel(*refs, local, n_steps, n_sub):
    if local:
        (sink_ref, q_ref, *tile_refs, kx_ref, vtx_ref, gz_ref, x_ref, mod_ref, w_out_ref,
         g_ref, b_ref, mask_ref, o_ref, sx_ref, m_ref, ot_ref, og_ref, sl_ref) = refs
        k_refs, vt_refs = tile_refs[:n_sub + 2], tile_refs[n_sub + 2:]
    else:
        (sink_ref, q_ref, kx_ref, vtx_ref, gz_ref, x_ref, mod_ref, w_out_ref, g_ref, b_ref,
         o_ref, sx_ref, m_ref, ot_ref, og_ref) = refs

    tq = ATT_BLOCK
    low = lax.broadcasted_iota(jnp.int32, (1, LANES), 1) < HEAD_DIM
    left = lax.broadcasted_iota(jnp.int32, (1, 2 * tq), 1) < tq
    if local:
        i = pl.program_id(1)
        first = jnp.where(i == 0, NEG_INF, 0.0).astype(F32)
        last = jnp.where(i == n_steps - 1, NEG_INF, 0.0).astype(F32)
        bias_prev = [mask_ref[0:WINDOW, :] + (first if sub == 0 else 0.0)
                     for sub in range(n_sub)]
        bias_next = [mask_ref[WINDOW:2 * WINDOW, :] + (last if sub == n_sub - 1 else 0.0)
                     for sub in range(n_sub)]

    def sink_row(g, half):
        return jnp.where(left, sink_ref[4 * g + half], sink_ref[4 * g + 2 + half]) * LOG2E

    units = [(sub, half) for sub in range(n_sub) for half in range(2)]

    def scores(g, sub, half):
        rows = slice(sub * tq, (sub + 1) * tq)
        qp = jnp.concatenate([q_ref[0, g, rows, 0:LANES], q_ref[0, g, rows, LANES:2 * LANES]],
                             axis=0)
        qh = jnp.where(low, qp, jnp.zeros_like(qp)) if half == 0 else \
            jnp.where(low, jnp.zeros_like(qp), qp)
        sx = _dot_nt(kx_ref[0, g], qh)
        sx_ref[g, sub, half] = sx
        m = jnp.max(sx, axis=0, keepdims=True)
        if local:
            kl = jnp.concatenate([k_refs[sub + t][0, g] for t in range(3)], axis=0)
            sl = _dot_nt(kl, qh)
            sl = jnp.concatenate(
                [sl[0:WINDOW] + bias_prev[sub], sl[WINDOW:2 * WINDOW],
                 sl[2 * WINDOW:3 * WINDOW] + bias_next[sub]], axis=0)
            sl_ref[g, sub, half] = sl
            m = jnp.maximum(m, jnp.max(sl, axis=0, keepdims=True))
        m_ref[g, sub, half] = jnp.maximum(m, sink_row(g, half))

    def with_ones(vt):
        return jnp.concatenate([vt, jnp.ones((ONES_ROWS, vt.shape[1]), BF16)], axis=0)

    def attend(g, sub, half):
        m = m_ref[g, sub, half]
        px = jnp.exp2(sx_ref[g, sub, half] - m)
        o = _dot(with_ones(vtx_ref[0, g]), px.astype(BF16))
        if local:
            vt = jnp.concatenate([vt_refs[sub + t][0, g] for t in range(3)], axis=1)
            pl_ = jnp.exp2(sl_ref[g, sub, half] - m)
            o = o + _dot(with_ones(vt), pl_.astype(BF16))
        denom = o[HEAD_DIM:HEAD_DIM + 1] + jnp.exp2(sink_row(g, half) - m)
        ot_ref[g, sub, half] = o[0:HEAD_DIM] * (1.0 / denom)

    for sub, half in units:
        scores(0, sub, half)

    def body(g, carry):
        for sub, half in units:
            attend(g, sub, half)
            scores(g + 1, sub, half)
        return carry

    lax.fori_loop(0, N_KV_HEADS - 1, body, 0)

    def project_out(g):
        for sub in range(n_sub):
            rows = slice(sub * tq, (sub + 1) * tq)
            o_t = jnp.concatenate([ot_ref[g, sub, 0], ot_ref[g, sub, 1]], axis=0)
            for r in range(2):
                oc = slice(r * LANES, (r + 1) * LANES)
                o_blk = o_t[:, r * tq:(r + 1) * tq].T
                og_ref[g, rows, oc] = (o_blk * gz_ref[0, g, rows, oc].astype(F32)).astype(BF16)
        return _dot(og_ref[g], w_out_ref[0, g])

    last = N_KV_HEADS - 1
    out = None
    for g in range(last):
        for sub, half in units[g * len(units) // last:(g + 1) * len(units) // last]:
            attend(last, sub, half)
        part = project_out(g)
        out = part if out is None else out + part
    out = out + project_out(last)
    o_ref[0] = _post_norm(x_ref[0], out, mod_ref, g_ref, b_ref)


def _attend_b(q, kk, vt, kkx, vtx, gz, x, mod, j, w_out, sink, ln_g, ln_b, mask, *, local):
    bsz, _, length, group_w = q.shape
    n_ctx = kkx.shape[2]
    tq = ATT_BLOCK
    n_tiles = length // tq
    n_sub = min(ATT_SUB, n_tiles)
    rows = n_sub * tq
    n_steps = length // rows
    grouped_spec = pl.BlockSpec((1, N_KV_HEADS, rows, group_w), lambda b, i: (b, 0, i, 0))
    x_spec = pl.BlockSpec((1, rows, D_MODEL), lambda b, i: (b, i, 0))
    tile = lambda t: (lambda i: jnp.clip(n_sub * i + t - 1, 0, n_tiles - 1))
    k_specs = [pl.BlockSpec((1, N_KV_HEADS, tq, LANES), lambda b, i, f=tile(t): (b, 0, f(i), 0))
               for t in range(n_sub + 2)]
    vt_specs = [pl.BlockSpec((1, N_KV_HEADS, HEAD_DIM, tq),
                             lambda b, i, f=tile(t): (b, 0, 0, f(i))) for t in range(n_sub + 2)]
    in_specs = [pl.BlockSpec(memory_space=pltpu.SMEM), grouped_spec]
    args = [sink, q]
    if local:
        in_specs += k_specs + vt_specs
        args += [kk] * (n_sub + 2) + [vt] * (n_sub + 2)
    in_specs += [pl.BlockSpec((1, N_KV_HEADS, n_ctx, LANES), lambda b, i: (b, 0, 0, 0)),
                 pl.BlockSpec((1, N_KV_HEADS, HEAD_DIM, n_ctx), lambda b, i: (b, 0, 0, 0))]
    args += [kkx, vtx]
    in_specs += [grouped_spec, x_spec, pl.BlockSpec((1, 3, D_MODEL), lambda b, i: (b, 0, 0)),
                 _layer_spec((N_KV_HEADS, group_w, D_MODEL), j), _const_spec((1, D_MODEL)),
                 _const_spec((1, D_MODEL))]
    args += [gz, x, mod, w_out, ln_g, ln_b]
    scratch = [pltpu.VMEM((N_KV_HEADS, n_sub, 2, n_ctx, 2 * tq), F32),
               pltpu.VMEM((N_KV_HEADS, n_sub, 2, 1, 2 * tq), F32),
               pltpu.VMEM((N_KV_HEADS, n_sub, 2, HEAD_DIM, 2 * tq), F32),
               pltpu.VMEM((N_KV_HEADS, rows, group_w), BF16)]
    if local:
        in_specs.append(_const_spec((2 * WINDOW, 2 * ATT_BLOCK)))
        args.append(mask)
        scratch.append(pltpu.VMEM((N_KV_HEADS, n_sub, 2, 3 * WINDOW, 2 * tq), F32))
    kernel = functools.partial(_attend_b_kernel, local=local, n_steps=n_steps, n_sub=n_sub)
    return pl.pallas_call(
        kernel,
        out_shape=jax.ShapeDtypeStruct(x.shape, F32),
        grid=(bsz, n_steps),
        in_specs=in_specs,
        out_specs=x_spec,
        scratch_shapes=scratch,
        compiler_params=pltpu.CompilerParams(
            dimension_semantics=("parallel", "parallel"), vmem_limit_bytes=VMEM_LIMIT),
        name="attend_b_local" if local else "attend_b_ctx",
    )(*args)


def _rope_tables(length):
    rows = length // GRID_W
    row = jnp.repeat(jnp.arange(rows), GRID_W).astype(F32)
    col = jnp.tile(jnp.arange(GRID_W), rows).astype(F32)
    n_freq = ROPE_AXIS_DIM // 2
    inv = ROPE_BASE ** (-jnp.arange(n_freq, dtype=F32) / n_freq)
    ang_r, ang_c = row[:, None] * inv[None, :], col[:, None] * inv[None, :]
    cr, sr, cc, sc = jnp.cos(ang_r), jnp.sin(ang_r), jnp.cos(ang_c), jnp.sin(ang_c)
    zero = jnp.zeros_like(sr)
    reps = LANES // HEAD_DIM
    cos = jnp.tile(jnp.concatenate([cr, cr, cc, cc], axis=1), (1, reps))
    sin_a = jnp.tile(jnp.concatenate([-sr, zero, -sc, zero], axis=1), (1, reps))
    sin_b = jnp.tile(jnp.concatenate([zero, sr, zero, sc], axis=1), (1, reps))
    return cos, sin_a, sin_b


def _band_mask():
    kj = jnp.arange(WINDOW)[:, None]
    qi = jnp.arange(ATT_BLOCK)[None, :]
    prev = jnp.where(kj >= qi, 0.0, NEG_INF)
    nxt = jnp.where(kj <= qi, 0.0, NEG_INF)
    return jnp.tile(jnp.concatenate([prev, nxt], axis=0), (1, 2)).astype(F32)


def kernel(x, c, ctx, c_ctx, ada_w, ada_b, ln_g, ln_b, a_w_in, a_ln_g, a_ln_b, a_w_s, a_b_s,
           a_w_out, b_w_in, b_sink, b_w_out):
    bsz, length, _ = x.shape
    n_ctx = ctx.shape[1]
    assert bsz + 1 <= MOD_ROWS

    c_rows = jnp.zeros((MOD_ROWS, D_MODEL), F32).at[:bsz].set(c).at[bsz].set(c_ctx)
    mod_all = _modulation(c_rows, ada_w, ada_b)
    tables = _rope_tables(length)
    mask = _band_mask()
    a_w_in_h, a_w_s_h, a_w_out_h = (w.astype(BF16) for w in (a_w_in, a_w_s, a_w_out))
    b_w_in_h = b_w_in.astype(BF16)
    b_w_out_h = b_w_out.astype(BF16).reshape(-1, N_KV_HEADS, B_WIDTH // N_KV_HEADS, D_MODEL)

    for i in range(DEPTH):
        j = i // N_MIXERS
        need_ctx_out = i < DEPTH - 1
        mod = mod_all[i, :bsz].reshape(bsz, 3, D_MODEL)
        mod_c = jnp.broadcast_to(mod_all[i, bsz].reshape(1, 3, D_MODEL), (bsz, 3, D_MODEL))
        g_i, b_i = ln_g[i].reshape(1, D_MODEL), ln_b[i].reshape(1, D_MODEL)
        if i % N_MIXERS == 0:
            b_s_full = jnp.repeat(a_b_s[j].T, A_GW, axis=1)
            weights = (j, a_w_in_h, a_ln_g[j].reshape(1, A_WIDTH), a_ln_b[j].reshape(1, A_WIDTH),
                       a_w_s_h, b_s_full, a_w_out_h, g_i, b_i)
            x_new = _layer_a(x, mod, *weights, tm=512)
            if need_ctx_out:
                ctx = _layer_a(ctx, mod_c, *weights, tm=n_ctx)
            x = x_new
        else:
            sink = b_sink[j].astype(F32)
            q, kk, vt, gz = _project_b(x, mod, j, b_w_in_h, tables, tm=512, rope=True)
            qc, kkc, vtc, gzc = _project_b(ctx, mod_c, j, b_w_in_h, tables, tm=n_ctx, rope=False)
            x = _attend_b(q, kk, vt, kkc, vtc, gz, x, mod, j, b_w_out_h, sink, g_i, b_i, mask,
                          local=True)
            if need_ctx_out:
                ctx = _attend_b(qc, None, None, kkc, vtc, gzc, ctx, mod_c, j, b_w_out_h, sink,
                                g_i, b_i, None, local=False)
    return x
```

```python
import functools
import math

import jax
import jax.numpy as jnp
from jax import lax
from jax.experimental import pallas as pl
from jax.experimental.pallas import tpu as pltpu

D_MODEL = 1024
DEPTH = 4
GRID_W = 64
N_MIXERS = 2
CHUNK = 128
A_WIDTH = 2 * D_MODEL
A_GROUPS = 8
A_GW = A_WIDTH // A_GROUPS
HEAD_DIM = 64
N_HEADS = D_MODEL // HEAD_DIM
N_KV_HEADS = N_HEADS // 4
B_WIDTH = N_HEADS * HEAD_DIM
KV_WIDTH = N_KV_HEADS * HEAD_DIM
WINDOW = 128
ATT_BLOCK = 128
ATT_SUB = 4
ROPE_AXIS_DIM = HEAD_DIM // 2
ROPE_BASE = 10000.0
LN_EPS = 1e-5
NEG_INF = -1e30
ALPHA = (2.0 * DEPTH) ** 0.25

LANES = 128
MOD_ROWS = 8
VMEM_LIMIT = 48 * 1024 * 1024

BF16 = jnp.bfloat16
F32 = jnp.float32

LOG2E = math.log2(math.e)
Q_SCALE = HEAD_DIM ** -0.5 * LOG2E
ONES_ROWS = 16
A_SUB_ROWS = 256
COPY_ROWS = 128

_GELU_C0 = math.sqrt(2.0 / math.pi)
_GELU_C1 = _GELU_C0 * 0.044715


def _dot(a, b):
    return jnp.dot(a, b, preferred_element_type=F32)


def _dot_nt(a, b):
    return lax.dot_general(a, b, (((1,), (1,)), ((), ())), preferred_element_type=F32)


def _gelu(x):
    inner = x * (_GELU_C1 * (x * x) + _GELU_C0)
    hx = 0.5 * x
    return hx + hx * jnp.tanh(inner)


def _silu(z):
    hz = 0.5 * z
    return hz + hz * jnp.tanh(hz)


def _layer_norm(x, g, b):
    mu = jnp.mean(x, axis=-1, keepdims=True)
    xc = x - mu
    var = jnp.mean(xc * xc, axis=-1, keepdims=True)
    return xc * lax.rsqrt(var + LN_EPS) * g + b


def _const_spec(shape):
    nd = len(shape)
    return pl.BlockSpec(shape, lambda *_: (0,) * nd, pipeline_mode=pl.Buffered(1))


def _layer_spec(shape, j):
    nd = len(shape)
    return pl.BlockSpec((1,) + tuple(shape), lambda *_: (j,) + (0,) * nd,
                        pipeline_mode=pl.Buffered(1))


def _modulation_kernel(c_ref, w_ref, b_ref, o_ref):
    c = c_ref[...]
    cond = _silu(c).astype(BF16)
    o_ref[0] = _dot(cond, w_ref[0].astype(BF16)) + b_ref[0]


def _modulation(c_rows, ada_w, ada_b):
    tn = D_MODEL
    return pl.pallas_call(
        _modulation_kernel,
        out_shape=jax.ShapeDtypeStruct((DEPTH, MOD_ROWS, 3 * D_MODEL), F32),
        grid=(DEPTH, 3 * D_MODEL // tn),
        in_specs=[
            pl.BlockSpec((MOD_ROWS, D_MODEL), lambda i, j: (0, 0)),
            pl.BlockSpec((1, D_MODEL, tn), lambda i, j: (i, 0, j)),
            pl.BlockSpec((1, 1, tn), lambda i, j: (i, 0, j)),
        ],
        out_specs=pl.BlockSpec((1, MOD_ROWS, tn), lambda i, j: (i, 0, j)),
        compiler_params=pltpu.CompilerParams(
            dimension_semantics=("parallel", "parallel"), vmem_limit_bytes=VMEM_LIMIT),
        name="modulation",
    )(c_rows, ada_w, ada_b.reshape(DEPTH, 1, 3 * D_MODEL))


def _modulate(x, mod_ref):
    shift = mod_ref[0, 0:1, :]
    scale = mod_ref[0, 1:2, :]
    return x * (1.0 + scale) + shift


def _post_norm(x, out, mod_ref, g_ref, b_ref):
    gate = mod_ref[0, 2:3, :]
    return _layer_norm(ALPHA * x + gate * out, g_ref[...], b_ref[...])


def _layer_a_kernel(x_ref, mod_ref, w_in_ref, lng_ref, lnb_ref, ws_ref, bs_ref, w_out_ref,
                    g_ref, b_ref, o_ref, vn_ref, y_ref, w_in_s, w_out_s, *, tm, ts):
    @pl.when((pl.program_id(0) == 0) & (pl.program_id(1) == 0))
    def _():
        def copy_rows(ref, dst, r):
            rows = pl.ds(pl.multiple_of(r * COPY_ROWS, COPY_ROWS), COPY_ROWS)
            half_rows = pl.ds(pl.multiple_of(r * (COPY_ROWS // 2), COPY_ROWS // 2), COPY_ROWS // 2)
            dst[half_rows, :] = pltpu.bitcast(ref[0, rows, :], jnp.uint32)

        lax.fori_loop(0, D_MODEL // COPY_ROWS,
                      lambda r, c: (copy_rows(w_in_ref, w_in_s, r), c)[1], 0)
        lax.fori_loop(0, A_WIDTH // COPY_ROWS,
                      lambda r, c: (copy_rows(w_out_ref, w_out_s, r), c)[1], 0)

    for t in range(tm // ts):
        sub = slice(t * ts, (t + 1) * ts)
        x = x_ref[0, sub, :]
        h = _modulate(x, mod_ref).astype(BF16)
        w_in = lambda lo, hi: pltpu.bitcast(w_in_s[:, lo:hi], BF16)
        v = _gelu(_dot(h, w_in(A_WIDTH, 2 * A_WIDTH)))
        vn_ref[sub, :] = _layer_norm(v, lng_ref[...], lnb_ref[...]).astype(BF16)
        for g in range(A_GROUPS):
            cols = slice(g * A_GW, (g + 1) * A_GW)
            u = _gelu(_dot(h, w_in(g * A_GW, (g + 1) * A_GW)))
            z = _dot(h, w_in(2 * A_WIDTH + g * A_GW, 2 * A_WIDTH + (g + 1) * A_GW))
            uz = u * _silu(z)
            for c in range(ts // CHUNK):
                rows = slice(t * ts + c * CHUNK, t * ts + (c + 1) * CHUNK)
                s = _dot(ws_ref[0, g], vn_ref[rows, cols]) + bs_ref[:, cols]
                y_ref[rows, cols] = (uz[c * CHUNK:(c + 1) * CHUNK] * s).astype(BF16)
        out = _dot(y_ref[sub, :], pltpu.bitcast(w_out_s[...], BF16))
        o_ref[0, sub, :] = _post_norm(x, out, mod_ref, g_ref, b_ref)


def _layer_a(x, mod, j, w_in, a_ln_g, a_ln_b, w_s, b_s_full, w_out, ln_g, ln_b, *, tm):
    bsz, length, _ = x.shape
    kernel = functools.partial(_layer_a_kernel, tm=tm, ts=min(tm, A_SUB_ROWS))
    return pl.pallas_call(
        kernel,
        out_shape=jax.ShapeDtypeStruct(x.shape, F32),
        grid=(bsz, length // tm),
        in_specs=[
            pl.BlockSpec((1, tm, D_MODEL), lambda b, t: (b, t, 0)),
            pl.BlockSpec((1, 3, D_MODEL), lambda b, t: (b, 0, 0)),
            _layer_spec((D_MODEL, 3 * A_WIDTH), j),
            _const_spec((1, A_WIDTH)),
            _const_spec((1, A_WIDTH)),
            _layer_spec((A_GROUPS, CHUNK, CHUNK), j),
            _const_spec((CHUNK, A_WIDTH)),
            _layer_spec((A_WIDTH, D_MODEL), j),
            _const_spec((1, D_MODEL)),
            _const_spec((1, D_MODEL)),
        ],
        out_specs=pl.BlockSpec((1, tm, D_MODEL), lambda b, t: (b, t, 0)),
        scratch_shapes=[pltpu.VMEM((tm, A_WIDTH), BF16), pltpu.VMEM((tm, A_WIDTH), BF16),
                        pltpu.VMEM((D_MODEL // 2, 3 * A_WIDTH), jnp.uint32),
                        pltpu.VMEM((A_WIDTH // 2, D_MODEL), jnp.uint32)],
        compiler_params=pltpu.CompilerParams(
            dimension_semantics=("arbitrary", "arbitrary"), vmem_limit_bytes=VMEM_LIMIT),
        name="layer_a",
    )(x, mod, w_in, a_ln_g, a_ln_b, w_s, b_s_full, w_out, ln_g, ln_b)


def _rope(t, cos, sin_a, sin_b):
    n = t.shape[1] // LANES
    half = ROPE_AXIS_DIM // 2
    outs = []
    for j in range(n):
        blk = t[:, j * LANES:(j + 1) * LANES]
        up = pltpu.roll(blk, LANES - half, 1)
        dn = pltpu.roll(blk, half, 1)
        outs.append(blk * cos + up * sin_a + dn * sin_b)
    return jnp.concatenate(outs, axis=1)


def _project_b_kernel(x_ref, mod_ref, w_ref, cos_ref, sa_ref, sb_ref,
                      q_ref, kk_ref, vt_ref, gz_ref, *, rope):
    x = x_ref[0]
    h = _modulate(x, mod_ref).astype(BF16)
    w = lambda lo, hi: w_ref[0, :, lo:hi]
    q = _dot(h, w(0, B_WIDTH))
    k = _dot(h, w(B_WIDTH, B_WIDTH + KV_WIDTH))
    v = _dot(h, w(B_WIDTH + KV_WIDTH, B_WIDTH + 2 * KV_WIDTH))
    z = _dot(h, w(B_WIDTH + 2 * KV_WIDTH, 2 * B_WIDTH + 2 * KV_WIDTH))
    if rope:
        cos, sa, sb = cos_ref[...], sa_ref[...], sb_ref[...]
        q = _rope(q, cos, sa, sb)
        k = _rope(k, cos, sa, sb)
    q = (q * Q_SCALE).astype(BF16)
    gz = _silu(z).astype(BF16)
    v_t = v.T.astype(BF16)
    low = lax.broadcasted_iota(jnp.int32, (1, LANES), 1) < HEAD_DIM
    group_w = B_WIDTH // N_KV_HEADS
    for g in range(N_KV_HEADS):
        q_ref[0, g] = q[:, g * group_w:(g + 1) * group_w]
        gz_ref[0, g] = gz[:, g * group_w:(g + 1) * group_w]
        vt_ref[0, g] = v_t[g * HEAD_DIM:(g + 1) * HEAD_DIM, :]
    for j in range(KV_WIDTH // LANES):
        blk = k[:, j * LANES:(j + 1) * LANES]
        swapped = pltpu.roll(blk, HEAD_DIM, 1)
        kk_ref[0, 2 * j] = jnp.where(low, blk, swapped).astype(BF16)
        kk_ref[0, 2 * j + 1] = jnp.where(low, swapped, blk).astype(BF16)


def _project_b(x, mod, j, w_in, tables, *, tm, rope):
    bsz, length, _ = x.shape
    cos, sa, sb = tables
    group_w = B_WIDTH // N_KV_HEADS
    grouped = lambda width: jax.ShapeDtypeStruct((bsz, N_KV_HEADS, length, width), BF16)
    grouped_spec = lambda width: pl.BlockSpec((1, N_KV_HEADS, tm, width),
                                              lambda b, t: (b, 0, t, 0))
    tab_spec = pl.BlockSpec((tm, LANES), lambda b, t: (t, 0))
    kernel = functools.partial(_project_b_kernel, rope=rope)
    return pl.pallas_call(
        kernel,
        out_shape=(grouped(group_w), grouped(LANES),
                   jax.ShapeDtypeStruct((bsz, N_KV_HEADS, HEAD_DIM, length), BF16),
                   grouped(group_w)),
        grid=(bsz, length // tm),
        in_specs=[
            pl.BlockSpec((1, tm, D_MODEL), lambda b, t: (b, t, 0)),
            pl.BlockSpec((1, 3, D_MODEL), lambda b, t: (b, 0, 0)),
            _layer_spec((D_MODEL, 2 * B_WIDTH + 2 * KV_WIDTH), j),
            tab_spec, tab_spec, tab_spec,
        ],
        out_specs=(grouped_spec(group_w), grouped_spec(LANES),
                   pl.BlockSpec((1, N_KV_HEADS, HEAD_DIM, tm), lambda b, t: (b, 0, 0, t)),
                   grouped_spec(group_w)),
        compiler_params=pltpu.CompilerParams(
            dimension_semantics=("parallel", "parallel"), vmem_limit_bytes=VMEM_LIMIT),
        name="project_b_rope" if rope else "project_b",
    )(x, mod, w_in, cos, sa, sb)


def _attend_b_kernel(*refs, local, n_steps, n_sub):
    if local:
        (sink_ref, q_ref, *tile_refs, kx_ref, vtx_ref, gz_ref, x_ref, mod_ref, w_out_ref,
         g_ref, b_ref, mask_ref, o_ref, sx_ref, m_ref, ot_ref, og_ref, sl_ref) = refs
        k_refs, vt_refs = tile_refs[:n_sub + 2], tile_refs[n_sub + 2:]
    else:
        (sink_ref, q_ref, kx_ref, vtx_ref, gz_ref, x_ref, mod_ref, w_out_ref, g_ref, b_ref,
         o_ref, sx_ref, m_ref, ot_ref, og_ref) = refs

    tq = ATT_BLOCK
    low = lax.broadcasted_iota(jnp.int32, (1, LANES), 1) < HEAD_DIM
    left = lax.broadcasted_iota(jnp.int32, (1, 2 * tq), 1) < tq
    if local:
        i = pl.program_id(1)
        first = jnp.where(i == 0, NEG_INF, 0.0).astype(F32)
        last = jnp.where(i == n_steps - 1, NEG_INF, 0.0).astype(F32)
        bias_prev = [mask_ref[0:WINDOW, :] + (first if sub == 0 else 0.0)
                     for sub in range(n_sub)]
        bias_next = [mask_ref[WINDOW:2 * WINDOW, :] + (last if sub == n_sub - 1 else 0.0)
                     for sub in range(n_sub)]

    def sink_row(g, half):
        return jnp.where(left, sink_ref[4 * g + half], sink_ref[4 * g + 2 + half]) * LOG2E

    units = [(sub, half) for sub in range(n_sub) for half in range(2)]

    def scores(g, sub, half):
        rows = slice(sub * tq, (sub + 1) * tq)
        qp = jnp.concatenate([q_ref[0, g, rows, 0:LANES], q_ref[0, g, rows, LANES:2 * LANES]],
                             axis=0)
        qh = jnp.where(low, qp, jnp.zeros_like(qp)) if half == 0 else \
            jnp.where(low, jnp.zeros_like(qp), qp)
        sx = _dot_nt(kx_ref[0, g], qh)
        sx_ref[g, sub, half] = sx
        m = jnp.max(sx, axis=0, keepdims=True)
        if local:
            kl = jnp.concatenate([k_refs[sub + t][0, g] for t in range(3)], axis=0)
            sl = _dot_nt(kl, qh)
            sl = jnp.concatenate(
                [sl[0:WINDOW] + bias_prev[sub], sl[WINDOW:2 * WINDOW],
                 sl[2 * WINDOW:3 * WINDOW] + bias_next[sub]], axis=0)
            sl_ref[g, sub, half] = sl
            m = jnp.maximum(m, jnp.max(sl, axis=0, keepdims=True))
        m_ref[g, sub, half] = jnp.maximum(m, sink_row(g, half))

    def with_ones(vt):
        return jnp.concatenate([vt, jnp.ones((ONES_ROWS, vt.shape[1]), BF16)], axis=0)

    def attend(g, sub, half):
        m = m_ref[g, sub, half]
        px = jnp.exp2(sx_ref[g, sub, half] - m)
        o = _dot(with_ones(vtx_ref[0, g]), px.astype(BF16))
        if local:
            vt = jnp.concatenate([vt_refs[sub + t][0, g] for t in range(3)], axis=1)
            pl_ = jnp.exp2(sl_ref[g, sub, half] - m)
            o = o + _dot(with_ones(vt), pl_.astype(BF16))
        denom = o[HEAD_DIM:HEAD_DIM + 1] + jnp.exp2(sink_row(g, half) - m)
        ot_ref[g, sub, half] = o[0:HEAD_DIM] * (1.0 / denom)

    for sub, half in units:
        scores(0, sub, half)

    def body(g, carry):
        for sub, half in units:
            attend(g, sub, half)
            scores(g + 1, sub, half)
        return carry

    lax.fori_loop(0, N_KV_HEADS - 1, body, 0)

    def project_out(g):
        for sub in range(n_sub):
            rows = slice(sub * tq, (sub + 1) * tq)
            o_t = jnp.concatenate([ot_ref[g, sub, 0], ot_ref[g, sub, 1]], axis=0)
            for r in range(2):
                oc = slice(r * LANES, (r + 1) * LANES)
                o_blk = o_t[:, r * tq:(r + 1) * tq].T
                og_ref[g, rows, oc] = (o_blk * gz_ref[0, g, rows, oc].astype(F32)).astype(BF16)
        return _dot(og_ref[g], w_out_ref[0, g])

    last = N_KV_HEADS - 1
    out = None
    for g in range(last):
        for sub, half in units[g * len(units) // last:(g + 1) * len(units) // last]:
            attend(last, sub, half)
        part = project_out(g)
        out = part if out is None else out + part
    out = out + project_out(last)
    o_ref[0] = _post_norm(x_ref[0], out, mod_ref, g_ref, b_ref)


def _attend_b(q, kk, vt, kkx, vtx, gz, x, mod, j, w_out, sink, ln_g, ln_b, mask, *, local):
    bsz, _, length, group_w = q.shape
    n_ctx = kkx.shape[2]
    tq = ATT_BLOCK
    n_tiles = length // tq
    n_sub = min(ATT_SUB, n_tiles)
    rows = n_sub * tq
    n_steps = length // rows
    grouped_spec = pl.BlockSpec((1, N_KV_HEADS, rows, group_w), lambda b, i: (b, 0, i, 0))
    x_spec = pl.BlockSpec((1, rows, D_MODEL), lambda b, i: (b, i, 0))
    tile = lambda t: (lambda i: jnp.clip(n_sub * i + t - 1, 0, n_tiles - 1))
    k_specs = [pl.BlockSpec((1, N_KV_HEADS, tq, LANES), lambda b, i, f=tile(t): (b, 0, f(i), 0))
               for t in range(n_sub + 2)]
    vt_specs = [pl.BlockSpec((1, N_KV_HEADS, HEAD_DIM, tq),
                             lambda b, i, f=tile(t): (b, 0, 0, f(i))) for t in range(n_sub + 2)]
    in_specs = [pl.BlockSpec(memory_space=pltpu.SMEM), grouped_spec]
    args = [sink, q]
    if local:
        in_specs += k_specs + vt_specs
        args += [kk] * (n_sub + 2) + [vt] * (n_sub + 2)
    in_specs += [pl.BlockSpec((1, N_KV_HEADS, n_ctx, LANES), lambda b, i: (b, 0, 0, 0)),
                 pl.BlockSpec((1, N_KV_HEADS, HEAD_DIM, n_ctx), lambda b, i: (b, 0, 0, 0))]
    args += [kkx, vtx]
    in_specs += [grouped_spec, x_spec, pl.BlockSpec((1, 3, D_MODEL), lambda b, i: (b, 0, 0)),
                 _layer_spec((N_KV_HEADS, group_w, D_MODEL), j), _const_spec((1, D_MODEL)),
                 _const_spec((1, D_MODEL))]
    args += [gz, x, mod, w_out, ln_g, ln_b]
    scratch = [pltpu.VMEM((N_KV_HEADS, n_sub, 2, n_ctx, 2 * tq), F32),
               pltpu.VMEM((N_KV_HEADS, n_sub, 2, 1, 2 * tq), F32),
               pltpu.VMEM((N_KV_HEADS, n_sub, 2, HEAD_DIM, 2 * tq), F32),
               pltpu.VMEM((N_KV_HEADS, rows, group_w), BF16)]
    if local:
        in_specs.append(_const_spec((2 * WINDOW, 2 * ATT_BLOCK)))
        args.append(mask)
        scratch.append(pltpu.VMEM((N_KV_HEADS, n_sub, 2, 3 * WINDOW, 2 * tq), F32))
    kernel = functools.partial(_attend_b_kernel, local=local, n_steps=n_steps, n_sub=n_sub)
    return pl.pallas_call(
        kernel,
        out_shape=jax.ShapeDtypeStruct(x.shape, F32),
        grid=(bsz, n_steps),
        in_specs=in_specs,
        out_specs=x_spec,
        scratch_shapes=scratch,
        compiler_params=pltpu.CompilerParams(
            dimension_semantics=("parallel", "parallel"), vmem_limit_bytes=VMEM_LIMIT),
        name="attend_b_local" if local else "attend_b_ctx",
    )(*args)


def _rope_tables(length):
    rows = length // GRID_W
    row = jnp.repeat(jnp.arange(rows), GRID_W).astype(F32)
    col = jnp.tile(jnp.arange(GRID_W), rows).astype(F32)
    n_freq = ROPE_AXIS_DIM // 2
    inv = ROPE_BASE ** (-jnp.arange(n_freq, dtype=F32) / n_freq)
    ang_r, ang_c = row[:, None] * inv[None, :], col[:, None] * inv[None, :]
    cr, sr, cc, sc = jnp.cos(ang_r), jnp.sin(ang_r), jnp.cos(ang_c), jnp.sin(ang_c)
    zero = jnp.zeros_like(sr)
    reps = LANES // HEAD_DIM
    cos = jnp.tile(jnp.concatenate([cr, cr, cc, cc], axis=1), (1, reps))
    sin_a = jnp.tile(jnp.concatenate([-sr, zero, -sc, zero], axis=1), (1, reps))
    sin_b = jnp.tile(jnp.concatenate([zero, sr, zero, sc], axis=1), (1, reps))
    return cos, sin_a, sin_b


def _band_mask():
    kj = jnp.arange(WINDOW)[:, None]
    qi = jnp.arange(ATT_BLOCK)[None, :]
    prev = jnp.where(kj >= qi, 0.0, NEG_INF)
    nxt = jnp.where(kj <= qi, 0.0, NEG_INF)
    return jnp.tile(jnp.concatenate([prev, nxt], axis=0), (1, 2)).astype(F32)


def kernel(x, c, ctx, c_ctx, ada_w, ada_b, ln_g, ln_b, a_w_in, a_ln_g, a_ln_b, a_w_s, a_b_s,
           a_w_out, b_w_in, b_sink, b_w_out):
    bsz, length, _ = x.shape
    n_ctx = ctx.shape[1]
    assert bsz + 1 <= MOD_ROWS

    c_rows = jnp.zeros((MOD_ROWS, D_MODEL), F32).at[:bsz].set(c).at[bsz].set(c_ctx)
    mod_all = _modulation(c_rows, ada_w, ada_b)
    tables = _rope_tables(length)
    mask = _band_mask()
    a_w_in_h, a_w_s_h, a_w_out_h = (w.astype(BF16) for w in (a_w_in, a_w_s, a_w_out))
    b_w_in_h = b_w_in.astype(BF16)
    b_w_out_h = b_w_out.astype(BF16).reshape(-1, N_KV_HEADS, B_WIDTH // N_KV_HEADS, D_MODEL)

    for i in range(DEPTH):
        j = i // N_MIXERS
        need_ctx_out = i < DEPTH - 1
        mod = mod_all[i, :bsz].reshape(bsz, 3, D_MODEL)
        mod_c = jnp.broadcast_to(mod_all[i, bsz].reshape(1, 3, D_MODEL), (bsz, 3, D_MODEL))
        g_i, b_i = ln_g[i].reshape(1, D_MODEL), ln_b[i].reshape(1, D_MODEL)
        if i % N_MIXERS == 0:
            b_s_full = jnp.repeat(a_b_s[j].T, A_GW, axis=1)
            weights = (j, a_w_in_h, a_ln_g[j].reshape(1, A_WIDTH), a_ln_b[j].reshape(1, A_WIDTH),
                       a_w_s_h, b_s_full, a_w_out_h, g_i, b_i)
            x_new = _layer_a(x, mod, *weights, tm=512)
            if need_ctx_out:
                ctx = _layer_a(ctx, mod_c, *weights, tm=n_ctx)
            x = x_new
        else:
            sink = b_sink[j].astype(F32)
            q, kk, vt, gz = _project_b(x, mod, j, b_w_in_h, tables, tm=512, rope=True)
            qc, kkc, vtc, gzc = _project_b(ctx, mod_c, j, b_w_in_h, tables, tm=n_ctx, rope=False)
            x = _attend_b(q, kk, vt, kkc, vtc, gz, x, mod, j, b_w_out_h, sink, g_i, b_i, mask,
                          local=True)
            if need_ctx_out:
                ctx = _attend_b(qc, None, None, kkc, vtc, gzc, ctx, mod_c, j, b_w_out_h, sink,
                                g_i, b_i, None, local=False)
    return x
```

```python
import functools
import math

import jax
import jax.numpy as jnp
from jax import lax
from jax.experimental import pallas as pl
from jax.experimental.pallas import tpu as pltpu

D_MODEL = 1024
DEPTH = 4
GRID_W = 64
N_MIXERS = 2
CHUNK = 128
A_WIDTH = 2 * D_MODEL
A_GROUPS = 8
A_GW = A_WIDTH // A_GROUPS
HEAD_DIM = 64
N_HEADS = D_MODEL // HEAD_DIM
N_KV_HEADS = N_HEADS // 4
B_WIDTH = N_HEADS * HEAD_DIM
KV_WIDTH = N_KV_HEADS * HEAD_DIM
WINDOW = 128
ATT_BLOCK = 128
ATT_SUB = 4
ROPE_AXIS_DIM = HEAD_DIM // 2
ROPE_BASE = 10000.0
LN_EPS = 1e-5
NEG_INF = -1e30
ALPHA = (2.0 * DEPTH) ** 0.25

LANES = 128
MOD_ROWS = 8
VMEM_LIMIT = 48 * 1024 * 1024

BF16 = jnp.bfloat16
F32 = jnp.float32

LOG2E = math.log2(math.e)
Q_SCALE = HEAD_DIM ** -0.5 * LOG2E
ONES_ROWS = 16
A_SUB_ROWS = 256
COPY_ROWS = 128

_GELU_C0 = math.sqrt(2.0 / math.pi)
_GELU_C1 = _GELU_C0 * 0.044715


def _dot(a, b):
    return jnp.dot(a, b, preferred_element_type=F32)


def _dot_nt(a, b):
    return lax.dot_general(a, b, (((1,), (1,)), ((), ())), preferred_element_type=F32)


def _gelu(x):
    inner = x * (_GELU_C1 * (x * x) + _GELU_C0)
    hx = 0.5 * x
    return hx + hx * jnp.tanh(inner)


def _silu(z):
    hz = 0.5 * z
    return hz + hz * jnp.tanh(hz)


def _layer_norm(x, g, b):
    mu = jnp.mean(x, axis=-1, keepdims=True)
    xc = x - mu
    var = jnp.mean(xc * xc, axis=-1, keepdims=True)
    return xc * lax.rsqrt(var + LN_EPS) * g + b


def _const_spec(shape):
    nd = len(shape)
    return pl.BlockSpec(shape, lambda *_: (0,) * nd, pipeline_mode=pl.Buffered(1))


def _as_words(t):
    return pltpu.bitcast(t, jnp.uint32)


def _as_bf16(t):
    return pltpu.bitcast(t, BF16)


def _half(rows):
    return slice(rows.start // 2, rows.stop // 2)


def _stage_weight_once(w_ref, w_s):
    @pl.when((pl.program_id(0) == 0) & (pl.program_id(1) == 0))
    def _():
        def copy_rows(r, carry):
            rows = pl.ds(pl.multiple_of(r * COPY_ROWS, COPY_ROWS), COPY_ROWS)
            half_rows = pl.ds(pl.multiple_of(r * (COPY_ROWS // 2), COPY_ROWS // 2), COPY_ROWS // 2)
            w_s[half_rows, :] = _as_words(w_ref[0, rows, :])
            return carry

        lax.fori_loop(0, w_ref.shape[1] // COPY_ROWS, copy_rows, 0)


def _layer_spec(shape, j):
    nd = len(shape)
    return pl.BlockSpec((1,) + tuple(shape), lambda *_: (j,) + (0,) * nd,
                        pipeline_mode=pl.Buffered(1))


def _modulation_kernel(c_ref, w_ref, b_ref, o_ref):
    c = c_ref[...]
    cond = _silu(c).astype(BF16)
    o_ref[0] = _dot(cond, w_ref[0].astype(BF16)) + b_ref[0]


def _modulation(c_rows, ada_w, ada_b):
    tn = D_MODEL
    return pl.pallas_call(
        _modulation_kernel,
        out_shape=jax.ShapeDtypeStruct((DEPTH, MOD_ROWS, 3 * D_MODEL), F32),
        grid=(DEPTH, 3 * D_MODEL // tn),
        in_specs=[
            pl.BlockSpec((MOD_ROWS, D_MODEL), lambda i, j: (0, 0)),
            pl.BlockSpec((1, D_MODEL, tn), lambda i, j: (i, 0, j)),
            pl.BlockSpec((1, 1, tn), lambda i, j: (i, 0, j)),
        ],
        out_specs=pl.BlockSpec((1, MOD_ROWS, tn), lambda i, j: (i, 0, j)),
        compiler_params=pltpu.CompilerParams(
            dimension_semantics=("parallel", "parallel"), vmem_limit_bytes=VMEM_LIMIT),
        name="modulation",
    )(c_rows, ada_w, ada_b.reshape(DEPTH, 1, 3 * D_MODEL))


def _modulate(x, mod_ref):
    shift = mod_ref[0, 0:1, :]
    scale = mod_ref[0, 1:2, :]
    return x * (1.0 + scale) + shift


def _post_norm(x, out, mod_ref, g_ref, b_ref):
    gate = mod_ref[0, 2:3, :]
    return _layer_norm(ALPHA * x + gate * out, g_ref[...], b_ref[...])


def _layer_a_kernel(x_ref, mod_ref, w_in_ref, lng_ref, lnb_ref, ws_ref, bs_ref, w_out_ref,
                    g_ref, b_ref, o_ref, vn_ref, y_ref, w_in_s, w_out_s, *, tm, ts):
    _stage_weight_once(w_in_ref, w_in_s)
    _stage_weight_once(w_out_ref, w_out_s)

    for t in range(tm // ts):
        sub = slice(t * ts, (t + 1) * ts)
        x = x_ref[0, sub, :]
        h = _modulate(x, mod_ref).astype(BF16)
        w_in = lambda lo, hi: _as_bf16(w_in_s[:, lo:hi])
        v = _gelu(_dot(h, w_in(A_WIDTH, 2 * A_WIDTH)))
        vn_ref[_half(sub), :] = _as_words(
            _layer_norm(v, lng_ref[...], lnb_ref[...]).astype(BF16))
        for g in range(A_GROUPS):
            cols = slice(g * A_GW, (g + 1) * A_GW)
            u = _gelu(_dot(h, w_in(g * A_GW, (g + 1) * A_GW)))
            z = _dot(h, w_in(2 * A_WIDTH + g * A_GW, 2 * A_WIDTH + (g + 1) * A_GW))
            uz = u * _silu(z)
            for c in range(ts // CHUNK):
                rows = _half(slice(t * ts + c * CHUNK, t * ts + (c + 1) * CHUNK))
                s = _dot(ws_ref[0, g], _as_bf16(vn_ref[rows, cols])) + bs_ref[:, cols]
                y_ref[rows, cols] = _as_words((uz[c * CHUNK:(c + 1) * CHUNK] * s).astype(BF16))
        out = _dot(_as_bf16(y_ref[_half(sub), :]), _as_bf16(w_out_s[...]))
        o_ref[0, sub, :] = _post_norm(x, out, mod_ref, g_ref, b_ref)


def _layer_a(x, mod, j, w_in, a_ln_g, a_ln_b, w_s, b_s_full, w_out, ln_g, ln_b, *, tm):
    bsz, length, _ = x.shape
    kernel = functools.partial(_layer_a_kernel, tm=tm, ts=min(tm, A_SUB_ROWS))
    return pl.pallas_call(
        kernel,
        out_shape=jax.ShapeDtypeStruct(x.shape, F32),
        grid=(bsz, length // tm),
        in_specs=[
            pl.BlockSpec((1, tm, D_MODEL), lambda b, t: (b, t, 0)),
            pl.BlockSpec((1, 3, D_MODEL), lambda b, t: (b, 0, 0)),
            _layer_spec((D_MODEL, 3 * A_WIDTH), j),
            _const_spec((1, A_WIDTH)),
            _const_spec((1, A_WIDTH)),
            _layer_spec((A_GROUPS, CHUNK, CHUNK), j),
            _const_spec((CHUNK, A_WIDTH)),
            _layer_spec((A_WIDTH, D_MODEL), j),
            _const_spec((1, D_MODEL)),
            _const_spec((1, D_MODEL)),
        ],
        out_specs=pl.BlockSpec((1, tm, D_MODEL), lambda b, t: (b, t, 0)),
        scratch_shapes=[pltpu.VMEM((tm // 2, A_WIDTH), jnp.uint32),
                        pltpu.VMEM((tm // 2, A_WIDTH), jnp.uint32),
                        pltpu.VMEM((D_MODEL // 2, 3 * A_WIDTH), jnp.uint32),
                        pltpu.VMEM((A_WIDTH // 2, D_MODEL), jnp.uint32)],
        compiler_params=pltpu.CompilerParams(
            dimension_semantics=("arbitrary", "arbitrary"), vmem_limit_bytes=VMEM_LIMIT),
        name="layer_a",
    )(x, mod, w_in, a_ln_g, a_ln_b, w_s, b_s_full, w_out, ln_g, ln_b)


def _rope(t, cos, sin_a, sin_b):
    n = t.shape[1] // LANES
    half = ROPE_AXIS_DIM // 2
    outs = []
    for j in range(n):
        blk = t[:, j * LANES:(j + 1) * LANES]
        up = pltpu.roll(blk, LANES - half, 1)
        dn = pltpu.roll(blk, half, 1)
        outs.append(blk * cos + up * sin_a + dn * sin_b)
    return jnp.concatenate(outs, axis=1)


def _project_b_kernel(x_ref, mod_ref, w_ref, cos_ref, sa_ref, sb_ref,
                      q_ref, kk_ref, vt_ref, gz_ref, w_s, *, rope):
    _stage_weight_once(w_ref, w_s)
    x = x_ref[0]
    h = _modulate(x, mod_ref).astype(BF16)
    w = lambda lo, hi: _as_bf16(w_s[:, lo:hi])
    q = _dot(h, w(0, B_WIDTH))
    k = _dot(h, w(B_WIDTH, B_WIDTH + KV_WIDTH))
    v = _dot(h, w(B_WIDTH + KV_WIDTH, B_WIDTH + 2 * KV_WIDTH))
    z = _dot(h, w(B_WIDTH + 2 * KV_WIDTH, 2 * B_WIDTH + 2 * KV_WIDTH))
    if rope:
        cos, sa, sb = cos_ref[...], sa_ref[...], sb_ref[...]
        q = _rope(q, cos, sa, sb)
        k = _rope(k, cos, sa, sb)
    q = (q * Q_SCALE).astype(BF16)
    gz = _silu(z).astype(BF16)
    v_t = v.T.astype(BF16)
    low = lax.broadcasted_iota(jnp.int32, (1, LANES), 1) < HEAD_DIM
    group_w = B_WIDTH // N_KV_HEADS
    for g in range(N_KV_HEADS):
        q_ref[0, g] = q[:, g * group_w:(g + 1) * group_w]
        gz_ref[0, g] = gz[:, g * group_w:(g + 1) * group_w]
        vt_ref[0, g] = v_t[g * HEAD_DIM:(g + 1) * HEAD_DIM, :]
    for j in range(KV_WIDTH // LANES):
        blk = k[:, j * LANES:(j + 1) * LANES]
        swapped = pltpu.roll(blk, HEAD_DIM, 1)
        kk_ref[0, 2 * j] = jnp.where(low, blk, swapped).astype(BF16)
        kk_ref[0, 2 * j + 1] = jnp.where(low, swapped, blk).astype(BF16)


def _project_b(x, mod, j, w_in, tables, *, tm, rope):
    bsz, length, _ = x.shape
    cos, sa, sb = tables
    group_w = B_WIDTH // N_KV_HEADS
    grouped = lambda width: jax.ShapeDtypeStruct((bsz, N_KV_HEADS, length, width), BF16)
    grouped_spec = lambda width: pl.BlockSpec((1, N_KV_HEADS, tm, width),
                                              lambda b, t: (b, 0, t, 0))
    tab_spec = pl.BlockSpec((tm, LANES), lambda b, t: (t, 0))
    kernel = functools.partial(_project_b_kernel, rope=rope)
    return pl.pallas_call(
        kernel,
        out_shape=(grouped(group_w), grouped(LANES),
                   jax.ShapeDtypeStruct((bsz, N_KV_HEADS, HEAD_DIM, length), BF16),
                   grouped(group_w)),
        grid=(bsz, length // tm),
        in_specs=[
            pl.BlockSpec((1, tm, D_MODEL), lambda b, t: (b, t, 0)),
            pl.BlockSpec((1, 3, D_MODEL), lambda b, t: (b, 0, 0)),
            _layer_spec((D_MODEL, 2 * B_WIDTH + 2 * KV_WIDTH), j),
            tab_spec, tab_spec, tab_spec,
        ],
        out_specs=(grouped_spec(group_w), grouped_spec(LANES),
                   pl.BlockSpec((1, N_KV_HEADS, HEAD_DIM, tm), lambda b, t: (b, 0, 0, t)),
                   grouped_spec(group_w)),
        scratch_shapes=[pltpu.VMEM((D_MODEL // 2, 2 * B_WIDTH + 2 * KV_WIDTH), jnp.uint32)],
        compiler_params=pltpu.CompilerParams(
            dimension_semantics=("arbitrary", "arbitrary"), vmem_limit_bytes=VMEM_LIMIT),
        name="project_b_rope" if rope else "project_b",
    )(x, mod, w_in, cos, sa, sb)


def _attend_b_kernel(*refs, local, n_steps, n_sub):
    if local:
        (sink_ref, q_ref, *tile_refs, kx_ref, vtx_ref, gz_ref, x_ref, mod_ref, w_out_ref,
         g_ref, b_ref, mask_ref, o_ref, sx_ref, m_ref, ot_ref, og_ref, sl_ref) = refs
        k_refs, vt_refs = tile_refs[:n_sub + 2], tile_refs[n_sub + 2:]
    else:
        (sink_ref, q_ref, kx_ref, vtx_ref, gz_ref, x_ref, mod_ref, w_out_ref, g_ref, b_ref,
         o_ref, sx_ref, m_ref, ot_ref, og_ref) = refs

    tq = ATT_BLOCK
    low = lax.broadcasted_iota(jnp.int32, (1, LANES), 1) < HEAD_DIM
    left = lax.broadcasted_iota(jnp.int32, (1, 2 * tq), 1) < tq
    if local:
        i = pl.program_id(1)
        first = jnp.where(i == 0, NEG_INF, 0.0).astype(F32)
        last = jnp.where(i == n_steps - 1, NEG_INF, 0.0).astype(F32)
        bias_prev = [mask_ref[0:WINDOW, :] + (first if sub == 0 else 0.0)
                     for sub in range(n_sub)]
        bias_next = [mask_ref[WINDOW:2 * WINDOW, :] + (last if sub == n_sub - 1 else 0.0)
                     for sub in range(n_sub)]

    def sink_row(g, half):
        return jnp.where(left, sink_ref[4 * g + half], sink_ref[4 * g + 2 + half]) * LOG2E

    units = [(sub, half) for sub in range(n_sub) for half in range(2)]

    def scores(g, sub, half):
        rows = slice(sub * tq, (sub + 1) * tq)
        qp = jnp.concatenate([q_ref[0, g, rows, 0:LANES], q_ref[0, g, rows, LANES:2 * LANES]],
                             axis=0)
        qh = jnp.where(low, qp, jnp.zeros_like(qp)) if half == 0 else \
            jnp.where(low, jnp.zeros_like(qp), qp)
        sx = _dot_nt(kx_ref[0, g], qh)
        sx_ref[g, sub, half] = sx
        m = jnp.max(sx, axis=0, keepdims=True)
        if local:
            kl = jnp.concatenate([k_refs[sub + t][0, g] for t in range(3)], axis=0)
            sl = _dot_nt(kl, qh)
            sl = jnp.concatenate(
                [sl[0:WINDOW] + bias_prev[sub], sl[WINDOW:2 * WINDOW],
                 sl[2 * WINDOW:3 * WINDOW] + bias_next[sub]], axis=0)
            sl_ref[g, sub, half] = sl
            m = jnp.maximum(m, jnp.max(sl, axis=0, keepdims=True))
        m_ref[g, sub, half] = jnp.maximum(m, sink_row(g, half))

    def with_ones(vt):
        return jnp.concatenate([vt, jnp.ones((ONES_ROWS, vt.shape[1]), BF16)], axis=0)

    def attend(g, sub, half):
        m = m_ref[g, sub, half]
        px = jnp.exp2(sx_ref[g, sub, half] - m)
        o = _dot(with_ones(vtx_ref[0, g]), px.astype(BF16))
        if local:
            vt = jnp.concatenate([vt_refs[sub + t][0, g] for t in range(3)], axis=1)
            pl_ = jnp.exp2(sl_ref[g, sub, half] - m)
            o = o + _dot(with_ones(vt), pl_.astype(BF16))
        denom = o[HEAD_DIM:HEAD_DIM + 1] + jnp.exp2(sink_row(g, half) - m)
        ot_ref[g, sub, half] = o[0:HEAD_DIM] * (1.0 / denom)

    for sub, half in units:
        scores(0, sub, half)

    def body(g, carry):
        for sub, half in units:
            attend(g, sub, half)
            scores(g + 1, sub, half)
        return carry

    lax.fori_loop(0, N_KV_HEADS - 1, body, 0)

    def project_out(g):
        for sub in range(n_sub):
            rows = slice(sub * tq, (sub + 1) * tq)
            o_t = jnp.concatenate([ot_ref[g, sub, 0], ot_ref[g, sub, 1]], axis=0)
            for r in range(2):
                oc = slice(r * LANES, (r + 1) * LANES)
                o_blk = o_t[:, r * tq:(r + 1) * tq].T
                og_ref[g, _half(rows), oc] = _as_words(
                    (o_blk * gz_ref[0, g, rows, oc].astype(F32)).astype(BF16))
        return _dot(_as_bf16(og_ref[g]), w_out_ref[0, g])

    last = N_KV_HEADS - 1
    out = None
    for g in range(last):
        for sub, half in units[g * len(units) // last:(g + 1) * len(units) // last]:
            attend(last, sub, half)
        part = project_out(g)
        out = part if out is None else out + part
    out = out + project_out(last)
    o_ref[0] = _post_norm(x_ref[0], out, mod_ref, g_ref, b_ref)


def _attend_b(q, kk, vt, kkx, vtx, gz, x, mod, j, w_out, sink, ln_g, ln_b, mask, *, local):
    bsz, _, length, group_w = q.shape
    n_ctx = kkx.shape[2]
    tq = ATT_BLOCK
    n_tiles = length // tq
    n_sub = min(ATT_SUB, n_tiles)
    rows = n_sub * tq
    n_steps = length // rows
    grouped_spec = pl.BlockSpec((1, N_KV_HEADS, rows, group_w), lambda b, i: (b, 0, i, 0))
    x_spec = pl.BlockSpec((1, rows, D_MODEL), lambda b, i: (b, i, 0))
    tile = lambda t: (lambda i: jnp.clip(n_sub * i + t - 1, 0, n_tiles - 1))
    k_specs = [pl.BlockSpec((1, N_KV_HEADS, tq, LANES), lambda b, i, f=tile(t): (b, 0, f(i), 0))
               for t in range(n_sub + 2)]
    vt_specs = [pl.BlockSpec((1, N_KV_HEADS, HEAD_DIM, tq),
                             lambda b, i, f=tile(t): (b, 0, 0, f(i))) for t in range(n_sub + 2)]
    in_specs = [pl.BlockSpec(memory_space=pltpu.SMEM), grouped_spec]
    args = [sink, q]
    if local:
        in_specs += k_specs + vt_specs
        args += [kk] * (n_sub + 2) + [vt] * (n_sub + 2)
    in_specs += [pl.BlockSpec((1, N_KV_HEADS, n_ctx, LANES), lambda b, i: (b, 0, 0, 0)),
                 pl.BlockSpec((1, N_KV_HEADS, HEAD_DIM, n_ctx), lambda b, i: (b, 0, 0, 0))]
    args += [kkx, vtx]
    in_specs += [grouped_spec, x_spec, pl.BlockSpec((1, 3, D_MODEL), lambda b, i: (b, 0, 0)),
                 _layer_spec((N_KV_HEADS, group_w, D_MODEL), j), _const_spec((1, D_MODEL)),
                 _const_spec((1, D_MODEL))]
    args += [gz, x, mod, w_out, ln_g, ln_b]
    scratch = [pltpu.VMEM((N_KV_HEADS, n_sub, 2, n_ctx, 2 * tq), F32),
               pltpu.VMEM((N_KV_HEADS, n_sub, 2, 1, 2 * tq), F32),
               pltpu.VMEM((N_KV_HEADS, n_sub, 2, HEAD_DIM, 2 * tq), F32),
               pltpu.VMEM((N_KV_HEADS, rows // 2, group_w), jnp.uint32)]
    if local:
        in_specs.append(_const_spec((2 * WINDOW, 2 * ATT_BLOCK)))
        args.append(mask)
        scratch.append(pltpu.VMEM((N_KV_HEADS, n_sub, 2, 3 * WINDOW, 2 * tq), F32))
    kernel = functools.partial(_attend_b_kernel, local=local, n_steps=n_steps, n_sub=n_sub)
    return pl.pallas_call(
        kernel,
        out_shape=jax.ShapeDtypeStruct(x.shape, F32),
        grid=(bsz, n_steps),
        in_specs=in_specs,
        out_specs=x_spec,
        scratch_shapes=scratch,
        compiler_params=pltpu.CompilerParams(
            dimension_semantics=("parallel", "parallel"), vmem_limit_bytes=VMEM_LIMIT),
        name="attend_b_local" if local else "attend_b_ctx",
    )(*args)


def _rope_tables(length):
    rows = length // GRID_W
    row = jnp.repeat(jnp.arange(rows), GRID_W).astype(F32)
    col = jnp.tile(jnp.arange(GRID_W), rows).astype(F32)
    n_freq = ROPE_AXIS_DIM // 2
    inv = ROPE_BASE ** (-jnp.arange(n_freq, dtype=F32) / n_freq)
    ang_r, ang_c = row[:, None] * inv[None, :], col[:, None] * inv[None, :]
    cr, sr, cc, sc = jnp.cos(ang_r), jnp.sin(ang_r), jnp.cos(ang_c), jnp.sin(ang_c)
    zero = jnp.zeros_like(sr)
    reps = LANES // HEAD_DIM
    cos = jnp.tile(jnp.concatenate([cr, cr, cc, cc], axis=1), (1, reps))
    sin_a = jnp.tile(jnp.concatenate([-sr, zero, -sc, zero], axis=1), (1, reps))
    sin_b = jnp.tile(jnp.concatenate([zero, sr, zero, sc], axis=1), (1, reps))
    return cos, sin_a, sin_b


def _band_mask():
    kj = jnp.arange(WINDOW)[:, None]
    qi = jnp.arange(ATT_BLOCK)[None, :]
    prev = jnp.where(kj >= qi, 0.0, NEG_INF)
    nxt = jnp.where(kj <= qi, 0.0, NEG_INF)
    return jnp.tile(jnp.concatenate([prev, nxt], axis=0), (1, 2)).astype(F32)


def kernel(x, c, ctx, c_ctx, ada_w, ada_b, ln_g, ln_b, a_w_in, a_ln_g, a_ln_b, a_w_s, a_b_s,
           a_w_out, b_w_in, b_sink, b_w_out):
    bsz, length, _ = x.shape
    n_ctx = ctx.shape[1]
    assert bsz + 1 <= MOD_ROWS

    c_rows = jnp.zeros((MOD_ROWS, D_MODEL), F32).at[:bsz].set(c).at[bsz].set(c_ctx)
    mod_all = _modulation(c_rows, ada_w, ada_b)
    tables = _rope_tables(length)
    mask = _band_mask()
    a_w_in_h, a_w_s_h, a_w_out_h = (w.astype(BF16) for w in (a_w_in, a_w_s, a_w_out))
    b_w_in_h = b_w_in.astype(BF16)
    b_w_out_h = b_w_out.astype(BF16).reshape(-1, N_KV_HEADS, B_WIDTH // N_KV_HEADS, D_MODEL)

    for i in range(DEPTH):
        j = i // N_MIXERS
        need_ctx_out = i < DEPTH - 1
        mod = mod_all[i, :bsz].reshape(bsz, 3, D_MODEL)
        mod_c = jnp.broadcast_to(mod_all[i, bsz].reshape(1, 3, D_MODEL), (bsz, 3, D_MODEL))
        g_i, b_i = ln_g[i].reshape(1, D_MODEL), ln_b[i].reshape(1, D_MODEL)
        if i % N_MIXERS == 0:
            b_s_full = jnp.repeat(a_b_s[j].T, A_GW, axis=1)
            weights = (j, a_w_in_h, a_ln_g[j].reshape(1, A_WIDTH), a_ln_b[j].reshape(1, A_WIDTH),
                       a_w_s_h, b_s_full, a_w_out_h, g_i, b_i)
            x_new = _layer_a(x, mod, *weights, tm=512)
            if need_ctx_out:
                ctx = _layer_a(ctx, mod_c, *weights, tm=n_ctx)
            x = x_new
        else:
            sink = b_sink[j].astype(F32)
            q, kk, vt, gz = _project_b(x, mod, j, b_w_in_h, tables, tm=512, rope=True)
            qc, kkc, vtc, gzc = _project_b(ctx, mod_c, j, b_w_in_h, tables, tm=n_ctx, rope=False)
            x = _attend_b(q, kk, vt, kkc, vtc, gz, x, mod, j, b_w_out_h, sink, g_i, b_i, mask,
                          local=True)
            if need_ctx_out:
                ctx = _attend_b(qc, None, None, kkc, vtc, gzc, ctx, mod_c, j, b_w_out_h, sink,
                                g_i, b_i, None, local=False)
    return x
```

```python
import functools
import math

import jax
import jax.numpy as jnp
from jax import lax
from jax.experimental import pallas as pl
from jax.experimental.pallas import tpu as pltpu

D_MODEL = 1024
DEPTH = 4
GRID_W = 64
N_MIXERS = 2
CHUNK = 128
A_WIDTH = 2 * D_MODEL
A_GROUPS = 8
A_GW = A_WIDTH // A_GROUPS
HEAD_DIM = 64
N_HEADS = D_MODEL // HEAD_DIM
N_KV_HEADS = N_HEADS // 4
B_WIDTH = N_HEADS * HEAD_DIM
KV_WIDTH = N_KV_HEADS * HEAD_DIM
WINDOW = 128
ATT_BLOCK = 128
ATT_SUB = 4
ROPE_AXIS_DIM = HEAD_DIM // 2
ROPE_BASE = 10000.0
LN_EPS = 1e-5
NEG_INF = -1e30
ALPHA = (2.0 * DEPTH) ** 0.25

LANES = 128
MOD_ROWS = 8
VMEM_LIMIT = 48 * 1024 * 1024

BF16 = jnp.bfloat16
F32 = jnp.float32

LOG2E = math.log2(math.e)
Q_SCALE = HEAD_DIM ** -0.5 * LOG2E
ONES_ROWS = 16
A_SUB_ROWS = 256
COPY_ROWS = 128

_GELU_C0 = math.sqrt(2.0 / math.pi)
_GELU_C1 = _GELU_C0 * 0.044715


def _dot(a, b):
    return jnp.dot(a, b, preferred_element_type=F32)


def _dot_nt(a, b):
    return lax.dot_general(a, b, (((1,), (1,)), ((), ())), preferred_element_type=F32)


def _gelu(x):
    inner = x * (_GELU_C1 * (x * x) + _GELU_C0)
    hx = 0.5 * x
    return hx + hx * jnp.tanh(inner)


def _silu(z):
    hz = 0.5 * z
    return hz + hz * jnp.tanh(hz)


def _layer_norm(x, g, b):
    mu = jnp.mean(x, axis=-1, keepdims=True)
    xc = x - mu
    var = jnp.mean(xc * xc, axis=-1, keepdims=True)
    return xc * lax.rsqrt(var + LN_EPS) * g + b


def _const_spec(shape):
    nd = len(shape)
    return pl.BlockSpec(shape, lambda *_: (0,) * nd, pipeline_mode=pl.Buffered(1))


def _as_words(t):
    return pltpu.bitcast(t, jnp.uint32)


def _as_bf16(t):
    return pltpu.bitcast(t, BF16)


def _half(rows):
    return slice(rows.start // 2, rows.stop // 2)


def _stage_weight_once(w_ref, w_s):
    @pl.when((pl.program_id(0) == 0) & (pl.program_id(1) == 0))
    def _():
        def copy_rows(r, carry):
            rows = pl.ds(pl.multiple_of(r * COPY_ROWS, COPY_ROWS), COPY_ROWS)
            half_rows = pl.ds(pl.multiple_of(r * (COPY_ROWS // 2), COPY_ROWS // 2), COPY_ROWS // 2)
            w_s[half_rows, :] = _as_words(w_ref[0, rows, :])
            return carry

        lax.fori_loop(0, w_ref.shape[1] // COPY_ROWS, copy_rows, 0)


def _layer_spec(shape, j):
    nd = len(shape)
    return pl.BlockSpec((1,) + tuple(shape), lambda *_: (j,) + (0,) * nd,
                        pipeline_mode=pl.Buffered(1))


def _modulation_kernel(c_ref, w_ref, b_ref, o_ref):
    c = c_ref[...]
    cond = _silu(c).astype(BF16)
    o_ref[0] = _dot(cond, w_ref[0].astype(BF16)) + b_ref[0]


def _modulation(c_rows, ada_w, ada_b):
    tn = D_MODEL
    return pl.pallas_call(
        _modulation_kernel,
        out_shape=jax.ShapeDtypeStruct((DEPTH, MOD_ROWS, 3 * D_MODEL), F32),
        grid=(DEPTH, 3 * D_MODEL // tn),
        in_specs=[
            pl.BlockSpec((MOD_ROWS, D_MODEL), lambda i, j: (0, 0)),
            pl.BlockSpec((1, D_MODEL, tn), lambda i, j: (i, 0, j)),
            pl.BlockSpec((1, 1, tn), lambda i, j: (i, 0, j)),
        ],
        out_specs=pl.BlockSpec((1, MOD_ROWS, tn), lambda i, j: (i, 0, j)),
        compiler_params=pltpu.CompilerParams(
            dimension_semantics=("parallel", "parallel"), vmem_limit_bytes=VMEM_LIMIT),
        name="modulation",
    )(c_rows, ada_w, ada_b.reshape(DEPTH, 1, 3 * D_MODEL))


def _modulate(x, mod_ref):
    shift = mod_ref[0, 0:1, :]
    scale = mod_ref[0, 1:2, :]
    return x * (1.0 + scale) + shift


def _post_norm(x, out, mod_ref, g_ref, b_ref):
    gate = mod_ref[0, 2:3, :]
    return _layer_norm(ALPHA * x + gate * out, g_ref[...], b_ref[...])


def _layer_a_kernel(x_ref, mod_ref, w_in_ref, lng_ref, lnb_ref, ws_ref, bs_ref, w_out_ref,
                    g_ref, b_ref, o_ref, vn_ref, y_ref, w_in_s, w_out_s, *, tm, ts):
    _stage_weight_once(w_in_ref, w_in_s)
    _stage_weight_once(w_out_ref, w_out_s)

    w_in = lambda lo, hi: _as_bf16(w_in_s[:, lo:hi])
    subs = [slice(t * ts, (t + 1) * ts) for t in range(tm // ts)]

    def spatial_norm(sub):
        h = _modulate(x_ref[0, sub, :], mod_ref).astype(BF16)
        v = _gelu(_dot(h, w_in(A_WIDTH, 2 * A_WIDTH)))
        vn_ref[_half(sub), :] = _as_words(
            _layer_norm(v, lng_ref[...], lnb_ref[...]).astype(BF16))
        return h

    def gated_mix(sub, h):
        for g in range(A_GROUPS):
            cols = slice(g * A_GW, (g + 1) * A_GW)
            u = _gelu(_dot(h, w_in(g * A_GW, (g + 1) * A_GW)))
            z = _dot(h, w_in(2 * A_WIDTH + g * A_GW, 2 * A_WIDTH + (g + 1) * A_GW))
            uz = u * _silu(z)
            for c in range(ts // CHUNK):
                rows = _half(slice(sub.start + c * CHUNK, sub.start + (c + 1) * CHUNK))
                s = _dot(ws_ref[0, g], _as_bf16(vn_ref[rows, cols])) + bs_ref[:, cols]
                y_ref[rows, cols] = _as_words((uz[c * CHUNK:(c + 1) * CHUNK] * s).astype(BF16))

    def project_out(sub):
        out = _dot(_as_bf16(y_ref[_half(sub), :]), _as_bf16(w_out_s[...]))
        o_ref[0, sub, :] = _post_norm(x_ref[0, sub, :], out, mod_ref, g_ref, b_ref)

    h = spatial_norm(subs[0])
    for t, sub in enumerate(subs):
        gated_mix(sub, h)
        if t + 1 < len(subs):
            h = spatial_norm(subs[t + 1])
        project_out(sub)


def _layer_a(x, mod, j, w_in, a_ln_g, a_ln_b, w_s, b_s_full, w_out, ln_g, ln_b, *, tm):
    bsz, length, _ = x.shape
    kernel = functools.partial(_layer_a_kernel, tm=tm, ts=min(tm, A_SUB_ROWS))
    return pl.pallas_call(
        kernel,
        out_shape=jax.ShapeDtypeStruct(x.shape, F32),
        grid=(bsz, length // tm),
        in_specs=[
            pl.BlockSpec((1, tm, D_MODEL), lambda b, t: (b, t, 0)),
            pl.BlockSpec((1, 3, D_MODEL), lambda b, t: (b, 0, 0)),
            _layer_spec((D_MODEL, 3 * A_WIDTH), j),
            _const_spec((1, A_WIDTH)),
            _const_spec((1, A_WIDTH)),
            _layer_spec((A_GROUPS, CHUNK, CHUNK), j),
            _const_spec((CHUNK, A_WIDTH)),
            _layer_spec((A_WIDTH, D_MODEL), j),
            _const_spec((1, D_MODEL)),
            _const_spec((1, D_MODEL)),
        ],
        out_specs=pl.BlockSpec((1, tm, D_MODEL), lambda b, t: (b, t, 0)),
        scratch_shapes=[pltpu.VMEM((tm // 2, A_WIDTH), jnp.uint32),
                        pltpu.VMEM((tm // 2, A_WIDTH), jnp.uint32),
                        pltpu.VMEM((D_MODEL // 2, 3 * A_WIDTH), jnp.uint32),
                        pltpu.VMEM((A_WIDTH // 2, D_MODEL), jnp.uint32)],
        compiler_params=pltpu.CompilerParams(
            dimension_semantics=("arbitrary", "arbitrary"), vmem_limit_bytes=VMEM_LIMIT),
        name="layer_a",
    )(x, mod, w_in, a_ln_g, a_ln_b, w_s, b_s_full, w_out, ln_g, ln_b)


def _rope(t, cos, sin_a, sin_b):
    n = t.shape[1] // LANES
    half = ROPE_AXIS_DIM // 2
    outs = []
    for j in range(n):
        blk = t[:, j * LANES:(j + 1) * LANES]
        up = pltpu.roll(blk, LANES - half, 1)
        dn = pltpu.roll(blk, half, 1)
        outs.append(blk * cos + up * sin_a + dn * sin_b)
    return jnp.concatenate(outs, axis=1)


def _project_b_kernel(x_ref, mod_ref, w_ref, cos_ref, sa_ref, sb_ref,
                      q_ref, kk_ref, vt_ref, gz_ref, w_s, *, rope):
    _stage_weight_once(w_ref, w_s)
    x = x_ref[0]
    h = _modulate(x, mod_ref).astype(BF16)
    w = lambda lo, hi: _as_bf16(w_s[:, lo:hi])
    q = _dot(h, w(0, B_WIDTH))
    k = _dot(h, w(B_WIDTH, B_WIDTH + KV_WIDTH))
    v = _dot(h, w(B_WIDTH + KV_WIDTH, B_WIDTH + 2 * KV_WIDTH))
    z = _dot(h, w(B_WIDTH + 2 * KV_WIDTH, 2 * B_WIDTH + 2 * KV_WIDTH))
    if rope:
        cos, sa, sb = cos_ref[...], sa_ref[...], sb_ref[...]
        q = _rope(q, cos, sa, sb)
        k = _rope(k, cos, sa, sb)
    q = (q * Q_SCALE).astype(BF16)
    gz = _silu(z).astype(BF16)
    v_t = v.T.astype(BF16)
    low = lax.broadcasted_iota(jnp.int32, (1, LANES), 1) < HEAD_DIM
    group_w = B_WIDTH // N_KV_HEADS
    for g in range(N_KV_HEADS):
        q_ref[0, g] = q[:, g * group_w:(g + 1) * group_w]
        gz_ref[0, g] = gz[:, g * group_w:(g + 1) * group_w]
        vt_ref[0, g] = v_t[g * HEAD_DIM:(g + 1) * HEAD_DIM, :]
    for j in range(KV_WIDTH // LANES):
        blk = k[:, j * LANES:(j + 1) * LANES]
        swapped = pltpu.roll(blk, HEAD_DIM, 1)
        kk_ref[0, 2 * j] = jnp.where(low, blk, swapped).astype(BF16)
        kk_ref[0, 2 * j + 1] = jnp.where(low, swapped, blk).astype(BF16)


def _project_b(x, mod, j, w_in, tables, *, tm, rope):
    bsz, length, _ = x.shape
    cos, sa, sb = tables
    group_w = B_WIDTH // N_KV_HEADS
    grouped = lambda width: jax.ShapeDtypeStruct((bsz, N_KV_HEADS, length, width), BF16)
    grouped_spec = lambda width: pl.BlockSpec((1, N_KV_HEADS, tm, width),
                                              lambda b, t: (b, 0, t, 0))
    tab_spec = pl.BlockSpec((tm, LANES), lambda b, t: (t, 0))
    kernel = functools.partial(_project_b_kernel, rope=rope)
    return pl.pallas_call(
        kernel,
        out_shape=(grouped(group_w), grouped(LANES),
                   jax.ShapeDtypeStruct((bsz, N_KV_HEADS, HEAD_DIM, length), BF16),
                   grouped(group_w)),
        grid=(bsz, length // tm),
        in_specs=[
            pl.BlockSpec((1, tm, D_MODEL), lambda b, t: (b, t, 0)),
            pl.BlockSpec((1, 3, D_MODEL), lambda b, t: (b, 0, 0)),
            _layer_spec((D_MODEL, 2 * B_WIDTH + 2 * KV_WIDTH), j),
            tab_spec, tab_spec, tab_spec,
        ],
        out_specs=(grouped_spec(group_w), grouped_spec(LANES),
                   pl.BlockSpec((1, N_KV_HEADS, HEAD_DIM, tm), lambda b, t: (b, 0, 0, t)),
                   grouped_spec(group_w)),
        scratch_shapes=[pltpu.VMEM((D_MODEL // 2, 2 * B_WIDTH + 2 * KV_WIDTH), jnp.uint32)],
        compiler_params=pltpu.CompilerParams(
            dimension_semantics=("arbitrary", "arbitrary"), vmem_limit_bytes=VMEM_LIMIT),
        name="project_b_rope" if rope else "project_b",
    )(x, mod, w_in, cos, sa, sb)


def _attend_b_kernel(*refs, local, n_steps, n_sub):
    if local:
        (sink_ref, q_ref, *tile_refs, kx_ref, vtx_ref, gz_ref, x_ref, mod_ref, w_out_ref,
         g_ref, b_ref, mask_ref, o_ref, sx_ref, m_ref, ot_ref, og_ref, sl_ref) = refs
        k_refs, vt_refs = tile_refs[:n_sub + 2], tile_refs[n_sub + 2:]
    else:
        (sink_ref, q_ref, kx_ref, vtx_ref, gz_ref, x_ref, mod_ref, w_out_ref, g_ref, b_ref,
         o_ref, sx_ref, m_ref, ot_ref, og_ref) = refs

    tq = ATT_BLOCK
    low = lax.broadcasted_iota(jnp.int32, (1, LANES), 1) < HEAD_DIM
    left = lax.broadcasted_iota(jnp.int32, (1, 2 * tq), 1) < tq
    if local:
        i = pl.program_id(1)
        first = jnp.where(i == 0, NEG_INF, 0.0).astype(F32)
        last = jnp.where(i == n_steps - 1, NEG_INF, 0.0).astype(F32)
        bias_prev = [mask_ref[0:WINDOW, :] + (first if sub == 0 else 0.0)
                     for sub in range(n_sub)]
        bias_next = [mask_ref[WINDOW:2 * WINDOW, :] + (last if sub == n_sub - 1 else 0.0)
                     for sub in range(n_sub)]

    def sink_row(g, half):
        return jnp.where(left, sink_ref[4 * g + half], sink_ref[4 * g + 2 + half]) * LOG2E

    units = [(sub, half) for sub in range(n_sub) for half in range(2)]

    def scores(g, sub, half):
        rows = slice(sub * tq, (sub + 1) * tq)
        qp = jnp.concatenate([q_ref[0, g, rows, 0:LANES], q_ref[0, g, rows, LANES:2 * LANES]],
                             axis=0)
        qh = jnp.where(low, qp, jnp.zeros_like(qp)) if half == 0 else \
            jnp.where(low, jnp.zeros_like(qp), qp)
        sx = _dot_nt(kx_ref[0, g], qh)
        sx_ref[g, sub, half] = sx
        m = jnp.max(sx, axis=0, keepdims=True)
        if local:
            kl = jnp.concatenate([k_refs[sub + t][0, g] for t in range(3)], axis=0)
            sl = _dot_nt(kl, qh)
            sl = jnp.concatenate(
                [sl[0:WINDOW] + bias_prev[sub], sl[WINDOW:2 * WINDOW],
                 sl[2 * WINDOW:3 * WINDOW] + bias_next[sub]], axis=0)
            sl_ref[g, sub, half] = sl
            m = jnp.maximum(m, jnp.max(sl, axis=0, keepdims=True))
        m_ref[g, sub, half] = jnp.maximum(m, sink_row(g, half))

    def with_ones(vt):
        return jnp.concatenate([vt, jnp.ones((ONES_ROWS, vt.shape[1]), BF16)], axis=0)

    def attend(g, sub, half):
        m = m_ref[g, sub, half]
        px = jnp.exp2(sx_ref[g, sub, half] - m)
        o = _dot(with_ones(vtx_ref[0, g]), px.astype(BF16))
        if local:
            vt = jnp.concatenate([vt_refs[sub + t][0, g] for t in range(3)], axis=1)
            pl_ = jnp.exp2(sl_ref[g, sub, half] - m)
            o = o + _dot(with_ones(vt), pl_.astype(BF16))
        denom = o[HEAD_DIM:HEAD_DIM + 1] + jnp.exp2(sink_row(g, half) - m)
        ot_ref[g, sub, half] = o[0:HEAD_DIM] * (1.0 / denom)

    for sub, half in units:
        scores(0, sub, half)

    def body(g, carry):
        for sub, half in units:
            attend(g, sub, half)
            scores(g + 1, sub, half)
        return carry

    lax.fori_loop(0, N_KV_HEADS - 1, body, 0)

    def project_out(g):
        for sub in range(n_sub):
            rows = slice(sub * tq, (sub + 1) * tq)
            o_t = jnp.concatenate([ot_ref[g, sub, 0], ot_ref[g, sub, 1]], axis=0)
            for r in range(2):
                oc = slice(r * LANES, (r + 1) * LANES)
                o_blk = o_t[:, r * tq:(r + 1) * tq].T
                og_ref[g, _half(rows), oc] = _as_words(
                    (o_blk * gz_ref[0, g, rows, oc].astype(F32)).astype(BF16))
        return _dot(_as_bf16(og_ref[g]), w_out_ref[0, g])

    last = N_KV_HEADS - 1
    out = None
    for g in range(last):
        for sub, half in units[g * len(units) // last:(g + 1) * len(units) // last]:
            attend(last, sub, half)
        part = project_out(g)
        out = part if out is None else out + part
    out = out + project_out(last)
    o_ref[0] = _post_norm(x_ref[0], out, mod_ref, g_ref, b_ref)


def _attend_b(q, kk, vt, kkx, vtx, gz, x, mod, j, w_out, sink, ln_g, ln_b, mask, *, local):
    bsz, _, length, group_w = q.shape
    n_ctx = kkx.shape[2]
    tq = ATT_BLOCK
    n_tiles = length // tq
    n_sub = min(ATT_SUB, n_tiles)
    rows = n_sub * tq
    n_steps = length // rows
    grouped_spec = pl.BlockSpec((1, N_KV_HEADS, rows, group_w), lambda b, i: (b, 0, i, 0))
    x_spec = pl.BlockSpec((1, rows, D_MODEL), lambda b, i: (b, i, 0))
    tile = lambda t: (lambda i: jnp.clip(n_sub * i + t - 1, 0, n_tiles - 1))
    k_specs = [pl.BlockSpec((1, N_KV_HEADS, tq, LANES), lambda b, i, f=tile(t): (b, 0, f(i), 0))
               for t in range(n_sub + 2)]
    vt_specs = [pl.BlockSpec((1, N_KV_HEADS, HEAD_DIM, tq),
                             lambda b, i, f=tile(t): (b, 0, 0, f(i))) for t in range(n_sub + 2)]
    in_specs = [pl.BlockSpec(memory_space=pltpu.SMEM), grouped_spec]
    args = [sink, q]
    if local:
        in_specs += k_specs + vt_specs
        args += [kk] * (n_sub + 2) + [vt] * (n_sub + 2)
    in_specs += [pl.BlockSpec((1, N_KV_HEADS, n_ctx, LANES), lambda b, i: (b, 0, 0, 0)),
                 pl.BlockSpec((1, N_KV_HEADS, HEAD_DIM, n_ctx), lambda b, i: (b, 0, 0, 0))]
    args += [kkx, vtx]
    in_specs += [grouped_spec, x_spec, pl.BlockSpec((1, 3, D_MODEL), lambda b, i: (b, 0, 0)),
                 _layer_spec((N_KV_HEADS, group_w, D_MODEL), j), _const_spec((1, D_MODEL)),
                 _const_spec((1, D_MODEL))]
    args += [gz, x, mod, w_out, ln_g, ln_b]
    scratch = [pltpu.VMEM((N_KV_HEADS, n_sub, 2, n_ctx, 2 * tq), F32),
               pltpu.VMEM((N_KV_HEADS, n_sub, 2, 1, 2 * tq), F32),
               pltpu.VMEM((N_KV_HEADS, n_sub, 2, HEAD_DIM, 2 * tq), F32),
               pltpu.VMEM((N_KV_HEADS, rows // 2, group_w), jnp.uint32)]
    if local:
        in_specs.append(_const_spec((2 * WINDOW, 2 * ATT_BLOCK)))
        args.append(mask)
        scratch.append(pltpu.VMEM((N_KV_HEADS, n_sub, 2, 3 * WINDOW, 2 * tq), F32))
    kernel = functools.partial(_attend_b_kernel, local=local, n_steps=n_steps, n_sub=n_sub)
    return pl.pallas_call(
        kernel,
        out_shape=jax.ShapeDtypeStruct(x.shape, F32),
        grid=(bsz, n_steps),
        in_specs=in_specs,
        out_specs=x_spec,
        scratch_shapes=scratch,
        compiler_params=pltpu.CompilerParams(
            dimension_semantics=("parallel", "parallel"), vmem_limit_bytes=VMEM_LIMIT),
        name="attend_b_local" if local else "attend_b_ctx",
    )(*args)


def _rope_tables(length):
    rows = length // GRID_W
    row = jnp.repeat(jnp.arange(rows), GRID_W).astype(F32)
    col = jnp.tile(jnp.arange(GRID_W), rows).astype(F32)
    n_freq = ROPE_AXIS_DIM // 2
    inv = ROPE_BASE ** (-jnp.arange(n_freq, dtype=F32) / n_freq)
    ang_r, ang_c = row[:, None] * inv[None, :], col[:, None] * inv[None, :]
    cr, sr, cc, sc = jnp.cos(ang_r), jnp.sin(ang_r), jnp.cos(ang_c), jnp.sin(ang_c)
    zero = jnp.zeros_like(sr)
    reps = LANES // HEAD_DIM
    cos = jnp.tile(jnp.concatenate([cr, cr, cc, cc], axis=1), (1, reps))
    sin_a = jnp.tile(jnp.concatenate([-sr, zero, -sc, zero], axis=1), (1, reps))
    sin_b = jnp.tile(jnp.concatenate([zero, sr, zero, sc], axis=1), (1, reps))
    return cos, sin_a, sin_b


def _band_mask():
    kj = jnp.arange(WINDOW)[:, None]
    qi = jnp.arange(ATT_BLOCK)[None, :]
    prev = jnp.where(kj >= qi, 0.0, NEG_INF)
    nxt = jnp.where(kj <= qi, 0.0, NEG_INF)
    return jnp.tile(jnp.concatenate([prev, nxt], axis=0), (1, 2)).astype(F32)


def kernel(x, c, ctx, c_ctx, ada_w, ada_b, ln_g, ln_b, a_w_in, a_ln_g, a_ln_b, a_w_s, a_b_s,
           a_w_out, b_w_in, b_sink, b_w_out):
    bsz, length, _ = x.shape
    n_ctx = ctx.shape[1]
    assert bsz + 1 <= MOD_ROWS

    c_rows = jnp.zeros((MOD_ROWS, D_MODEL), F32).at[:bsz].set(c).at[bsz].set(c_ctx)
    mod_all = _modulation(c_rows, ada_w, ada_b)
    tables = _rope_tables(length)
    mask = _band_mask()
    a_w_in_h, a_w_s_h, a_w_out_h = (w.astype(BF16) for w in (a_w_in, a_w_s, a_w_out))
    b_w_in_h = b_w_in.astype(BF16)
    b_w_out_h = b_w_out.astype(BF16).reshape(-1, N_KV_HEADS, B_WIDTH // N_KV_HEADS, D_MODEL)

    for i in range(DEPTH):
        j = i // N_MIXERS
        need_ctx_out = i < DEPTH - 1
        mod = mod_all[i, :bsz].reshape(bsz, 3, D_MODEL)
        mod_c = jnp.broadcast_to(mod_all[i, bsz].reshape(1, 3, D_MODEL), (bsz, 3, D_MODEL))
        g_i, b_i = ln_g[i].reshape(1, D_MODEL), ln_b[i].reshape(1, D_MODEL)
        if i % N_MIXERS == 0:
            b_s_full = jnp.repeat(a_b_s[j].T, A_GW, axis=1)
            weights = (j, a_w_in_h, a_ln_g[j].reshape(1, A_WIDTH), a_ln_b[j].reshape(1, A_WIDTH),
                       a_w_s_h, b_s_full, a_w_out_h, g_i, b_i)
            x_new = _layer_a(x, mod, *weights, tm=512)
            if need_ctx_out:
                ctx = _layer_a(ctx, mod_c, *weights, tm=n_ctx)
            x = x_new
        else:
            sink = b_sink[j].astype(F32)
            q, kk, vt, gz = _project_b(x, mod, j, b_w_in_h, tables, tm=512, rope=True)
            qc, kkc, vtc, gzc = _project_b(ctx, mod_c, j, b_w_in_h, tables, tm=n_ctx, rope=False)
            x = _attend_b(q, kk, vt, kkc, vtc, gz, x, mod, j, b_w_out_h, sink, g_i, b_i, mask,
                          local=True)
            if need_ctx_out:
                ctx = _attend_b(qc, None, None, kkc, vtc, gzc, ctx, mod_c, j, b_w_out_h, sink,
                                g_i, b_i, None, local=False)
    return x
```

```python
import functools
import math

import jax
import jax.numpy as jnp
from jax import lax
from jax.experimental import pallas as pl
from jax.experimental.pallas import tpu as pltpu

D_MODEL = 1024
DEPTH = 4
GRID_W = 64
N_MIXERS = 2
CHUNK = 128
A_WIDTH = 2 * D_MODEL
A_GROUPS = 8
A_GW = A_WIDTH // A_GROUPS
HEAD_DIM = 64
N_HEADS = D_MODEL // HEAD_DIM
N_KV_HEADS = N_HEADS // 4
B_WIDTH = N_HEADS * HEAD_DIM
KV_WIDTH = N_KV_HEADS * HEAD_DIM
WINDOW = 128
ATT_BLOCK = 128
ATT_SUB = 4
ROPE_AXIS_DIM = HEAD_DIM // 2
ROPE_BASE = 10000.0
LN_EPS = 1e-5
NEG_INF = -1e30
ALPHA = (2.0 * DEPTH) ** 0.25

LANES = 128
MOD_ROWS = 8
VMEM_LIMIT = 48 * 1024 * 1024

BF16 = jnp.bfloat16
F32 = jnp.float32

LOG2E = math.log2(math.e)
Q_SCALE = HEAD_DIM ** -0.5 * LOG2E
ONES_ROWS = 16
A_SUB_ROWS = 256
COPY_ROWS = 128

_GELU_C0 = math.sqrt(2.0 / math.pi)
_GELU_C1 = _GELU_C0 * 0.044715


def _dot(a, b):
    return jnp.dot(a, b, preferred_element_type=F32)


def _dot_nt(a, b):
    return lax.dot_general(a, b, (((1,), (1,)), ((), ())), preferred_element_type=F32)


def _gelu(x):
    inner = x * (_GELU_C1 * (x * x) + _GELU_C0)
    hx = 0.5 * x
    return hx + hx * jnp.tanh(inner)


def _silu(z):
    hz = 0.5 * z
    return hz + hz * jnp.tanh(hz)


def _layer_norm(x, g, b):
    mu = jnp.mean(x, axis=-1, keepdims=True)
    xc = x - mu
    var = jnp.mean(xc * xc, axis=-1, keepdims=True)
    return xc * lax.rsqrt(var + LN_EPS) * g + b


def _const_spec(shape):
    nd = len(shape)
    return pl.BlockSpec(shape, lambda *_: (0,) * nd, pipeline_mode=pl.Buffered(1))


def _as_words(t):
    return pltpu.bitcast(t, jnp.uint32)


def _as_bf16(t):
    return pltpu.bitcast(t, BF16)


def _half(rows):
    return slice(rows.start // 2, rows.stop // 2)


def _stage_weight_once(w_hbm, j, w_s, buf, sem):
    n_chunks = w_hbm.shape[1] // COPY_ROWS

    def chunk_copy(c, slot):
        rows = pl.ds(pl.multiple_of(c * COPY_ROWS, COPY_ROWS), COPY_ROWS)
        return pltpu.make_async_copy(w_hbm.at[j, rows, :], buf.at[slot], sem.at[slot])

    @pl.when((pl.program_id(0) == 0) & (pl.program_id(1) == 0))
    def _():
        chunk_copy(0, 0).start()

        def convert_chunk(c, carry):
            slot = lax.rem(c, 2)

            @pl.when(c + 1 < n_chunks)
            def _():
                chunk_copy(c + 1, 1 - slot).start()

            chunk_copy(c, slot).wait()
            half_rows = pl.ds(pl.multiple_of(c * (COPY_ROWS // 2), COPY_ROWS // 2), COPY_ROWS // 2)
            w_s[half_rows, :] = _as_words(buf[slot].astype(BF16))
            return carry

        lax.fori_loop(0, n_chunks, convert_chunk, 0)


def _stage_scratch(k, n):
    return [pltpu.VMEM((k // 2, n), jnp.uint32), pltpu.VMEM((2, COPY_ROWS, n), F32),
            pltpu.SemaphoreType.DMA((2,))]


def _layer_spec(shape, j):
    nd = len(shape)
    return pl.BlockSpec((1,) + tuple(shape), lambda *_: (j,) + (0,) * nd,
                        pipeline_mode=pl.Buffered(1))


def _modulation_kernel(c_ref, w_ref, b_ref, o_ref):
    c = c_ref[...]
    cond = _silu(c).astype(BF16)
    o_ref[0] = _dot(cond, w_ref[0].astype(BF16)) + b_ref[0]


def _modulation(c_rows, ada_w, ada_b):
    tn = D_MODEL
    return pl.pallas_call(
        _modulation_kernel,
        out_shape=jax.ShapeDtypeStruct((DEPTH, MOD_ROWS, 3 * D_MODEL), F32),
        grid=(DEPTH, 3 * D_MODEL // tn),
        in_specs=[
            pl.BlockSpec((MOD_ROWS, D_MODEL), lambda i, j: (0, 0)),
            pl.BlockSpec((1, D_MODEL, tn), lambda i, j: (i, 0, j)),
            pl.BlockSpec((1, 1, tn), lambda i, j: (i, 0, j)),
        ],
        out_specs=pl.BlockSpec((1, MOD_ROWS, tn), lambda i, j: (i, 0, j)),
        compiler_params=pltpu.CompilerParams(
            dimension_semantics=("parallel", "parallel"), vmem_limit_bytes=VMEM_LIMIT),
        name="modulation",
    )(c_rows, ada_w, ada_b.reshape(DEPTH, 1, 3 * D_MODEL))


def _modulate(x, mod_ref):
    shift = mod_ref[0, 0:1, :]
    scale = mod_ref[0, 1:2, :]
    return x * (1.0 + scale) + shift


def _post_norm(x, out, mod_ref, g_ref, b_ref):
    gate = mod_ref[0, 2:3, :]
    return _layer_norm(ALPHA * x + gate * out, g_ref[...], b_ref[...])


def _layer_a_kernel(x_ref, mod_ref, w_in_ref, lng_ref, lnb_ref, ws_ref, bs_ref, w_out_ref,
                    g_ref, b_ref, o_ref, vn_ref, y_ref, w_in_s, w_in_buf, w_in_sem,
                    w_out_s, w_out_buf, w_out_sem, *, j, tm, ts):
    _stage_weight_once(w_in_ref, j, w_in_s, w_in_buf, w_in_sem)
    _stage_weight_once(w_out_ref, j, w_out_s, w_out_buf, w_out_sem)

    w_in = lambda lo, hi: _as_bf16(w_in_s[:, lo:hi])
    subs = [slice(t * ts, (t + 1) * ts) for t in range(tm // ts)]

    def spatial_norm(sub):
        h = _modulate(x_ref[0, sub, :], mod_ref).astype(BF16)
        v = _gelu(_dot(h, w_in(A_WIDTH, 2 * A_WIDTH)))
        vn_ref[_half(sub), :] = _as_words(
            _layer_norm(v, lng_ref[...], lnb_ref[...]).astype(BF16))
        return h

    def gated_mix(sub, h):
        for g in range(A_GROUPS):
            cols = slice(g * A_GW, (g + 1) * A_GW)
            u = _gelu(_dot(h, w_in(g * A_GW, (g + 1) * A_GW)))
            z = _dot(h, w_in(2 * A_WIDTH + g * A_GW, 2 * A_WIDTH + (g + 1) * A_GW))
            uz = u * _silu(z)
            for c in range(ts // CHUNK):
                rows = _half(slice(sub.start + c * CHUNK, sub.start + (c + 1) * CHUNK))
                s = _dot(ws_ref[0, g], _as_bf16(vn_ref[rows, cols])) + bs_ref[:, cols]
                y_ref[rows, cols] = _as_words((uz[c * CHUNK:(c + 1) * CHUNK] * s).astype(BF16))

    def project_out(sub):
        out = _dot(_as_bf16(y_ref[_half(sub), :]), _as_bf16(w_out_s[...]))
        o_ref[0, sub, :] = _post_norm(x_ref[0, sub, :], out, mod_ref, g_ref, b_ref)

    h = spatial_norm(subs[0])
    for t, sub in enumerate(subs):
        gated_mix(sub, h)
        if t + 1 < len(subs):
            h = spatial_norm(subs[t + 1])
        project_out(sub)


def _layer_a(x, mod, j, w_in, a_ln_g, a_ln_b, w_s, b_s_full, w_out, ln_g, ln_b, *, tm):
    bsz, length, _ = x.shape
    kernel = functools.partial(_layer_a_kernel, j=j, tm=tm, ts=min(tm, A_SUB_ROWS))
    return pl.pallas_call(
        kernel,
        out_shape=jax.ShapeDtypeStruct(x.shape, F32),
        grid=(bsz, length // tm),
        in_specs=[
            pl.BlockSpec((1, tm, D_MODEL), lambda b, t: (b, t, 0)),
            pl.BlockSpec((1, 3, D_MODEL), lambda b, t: (b, 0, 0)),
            pl.BlockSpec(memory_space=pl.ANY),
            _const_spec((1, A_WIDTH)),
            _const_spec((1, A_WIDTH)),
            _layer_spec((A_GROUPS, CHUNK, CHUNK), j),
            _const_spec((CHUNK, A_WIDTH)),
            pl.BlockSpec(memory_space=pl.ANY),
            _const_spec((1, D_MODEL)),
            _const_spec((1, D_MODEL)),
        ],
        out_specs=pl.BlockSpec((1, tm, D_MODEL), lambda b, t: (b, t, 0)),
        scratch_shapes=[pltpu.VMEM((tm // 2, A_WIDTH), jnp.uint32),
                        pltpu.VMEM((tm // 2, A_WIDTH), jnp.uint32)]
        + _stage_scratch(D_MODEL, 3 * A_WIDTH) + _stage_scratch(A_WIDTH, D_MODEL),
        compiler_params=pltpu.CompilerParams(
            dimension_semantics=("arbitrary", "arbitrary"), vmem_limit_bytes=VMEM_LIMIT),
        name="layer_a",
    )(x, mod, w_in, a_ln_g, a_ln_b, w_s, b_s_full, w_out, ln_g, ln_b)


def _rope(t, cos, sin_a, sin_b):
    n = t.shape[1] // LANES
    half = ROPE_AXIS_DIM // 2
    outs = []
    for j in range(n):
        blk = t[:, j * LANES:(j + 1) * LANES]
        up = pltpu.roll(blk, LANES - half, 1)
        dn = pltpu.roll(blk, half, 1)
        outs.append(blk * cos + up * sin_a + dn * sin_b)
    return jnp.concatenate(outs, axis=1)


def _project_b_kernel(x_ref, mod_ref, w_ref, cos_ref, sa_ref, sb_ref,
                      q_ref, kk_ref, vt_ref, gz_ref, w_s, w_buf, w_sem, *, j, rope):
    _stage_weight_once(w_ref, j, w_s, w_buf, w_sem)
    x = x_ref[0]
    h = _modulate(x, mod_ref).astype(BF16)
    w = lambda lo, hi: _as_bf16(w_s[:, lo:hi])
    q = _dot(h, w(0, B_WIDTH))
    k = _dot(h, w(B_WIDTH, B_WIDTH + KV_WIDTH))
    v = _dot(h, w(B_WIDTH + KV_WIDTH, B_WIDTH + 2 * KV_WIDTH))
    z = _dot(h, w(B_WIDTH + 2 * KV_WIDTH, 2 * B_WIDTH + 2 * KV_WIDTH))
    if rope:
        cos, sa, sb = cos_ref[...], sa_ref[...], sb_ref[...]
        q = _rope(q, cos, sa, sb)
        k = _rope(k, cos, sa, sb)
    q = (q * Q_SCALE).astype(BF16)
    gz = _silu(z).astype(BF16)
    v_t = v.T.astype(BF16)
    low = lax.broadcasted_iota(jnp.int32, (1, LANES), 1) < HEAD_DIM
    group_w = B_WIDTH // N_KV_HEADS
    for g in range(N_KV_HEADS):
        q_ref[0, g] = q[:, g * group_w:(g + 1) * group_w]
        gz_ref[0, g] = gz[:, g * group_w:(g + 1) * group_w]
        vt_ref[0, g] = v_t[g * HEAD_DIM:(g + 1) * HEAD_DIM, :]
    for j2 in range(KV_WIDTH // LANES):
        blk = k[:, j2 * LANES:(j2 + 1) * LANES]
        swapped = pltpu.roll(blk, HEAD_DIM, 1)
        kk_ref[0, 2 * j2] = jnp.where(low, blk, swapped).astype(BF16)
        kk_ref[0, 2 * j2 + 1] = jnp.where(low, swapped, blk).astype(BF16)


def _project_b(x, mod, j, w_in, tables, *, tm, rope):
    bsz, length, _ = x.shape
    cos, sa, sb = tables
    group_w = B_WIDTH // N_KV_HEADS
    grouped = lambda width: jax.ShapeDtypeStruct((bsz, N_KV_HEADS, length, width), BF16)
    grouped_spec = lambda width: pl.BlockSpec((1, N_KV_HEADS, tm, width),
                                              lambda b, t: (b, 0, t, 0))
    tab_spec = pl.BlockSpec((tm, LANES), lambda b, t: (t, 0))
    kernel = functools.partial(_project_b_kernel, j=j, rope=rope)
    return pl.pallas_call(
        kernel,
        out_shape=(grouped(group_w), grouped(LANES),
                   jax.ShapeDtypeStruct((bsz, N_KV_HEADS, HEAD_DIM, length), BF16),
                   grouped(group_w)),
        grid=(bsz, length // tm),
        in_specs=[
            pl.BlockSpec((1, tm, D_MODEL), lambda b, t: (b, t, 0)),
            pl.BlockSpec((1, 3, D_MODEL), lambda b, t: (b, 0, 0)),
            pl.BlockSpec(memory_space=pl.ANY),
            tab_spec, tab_spec, tab_spec,
        ],
        out_specs=(grouped_spec(group_w), grouped_spec(LANES),
                   pl.BlockSpec((1, N_KV_HEADS, HEAD_DIM, tm), lambda b, t: (b, 0, 0, t)),
                   grouped_spec(group_w)),
        scratch_shapes=_stage_scratch(D_MODEL, 2 * B_WIDTH + 2 * KV_WIDTH),
        compiler_params=pltpu.CompilerParams(
            dimension_semantics=("arbitrary", "arbitrary"), vmem_limit_bytes=VMEM_LIMIT),
        name="project_b_rope" if rope else "project_b",
    )(x, mod, w_in, cos, sa, sb)


def _attend_b_kernel(*refs, local, n_steps, n_sub):
    if local:
        (sink_ref, q_ref, *tile_refs, kx_ref, vtx_ref, gz_ref, x_ref, mod_ref, w_out_ref,
         g_ref, b_ref, mask_ref, o_ref, sx_ref, m_ref, ot_ref, og_ref, sl_ref) = refs
        k_refs, vt_refs = tile_refs[:n_sub + 2], tile_refs[n_sub + 2:]
    else:
        (sink_ref, q_ref, kx_ref, vtx_ref, gz_ref, x_ref, mod_ref, w_out_ref, g_ref, b_ref,
         o_ref, sx_ref, m_ref, ot_ref, og_ref) = refs

    tq = ATT_BLOCK
    low = lax.broadcasted_iota(jnp.int32, (1, LANES), 1) < HEAD_DIM
    left = lax.broadcasted_iota(jnp.int32, (1, 2 * tq), 1) < tq
    if local:
        i = pl.program_id(1)
        first = jnp.where(i == 0, NEG_INF, 0.0).astype(F32)
        last = jnp.where(i == n_steps - 1, NEG_INF, 0.0).astype(F32)
        bias_prev = [mask_ref[0:WINDOW, :] + (first if sub == 0 else 0.0)
                     for sub in range(n_sub)]
        bias_next = [mask_ref[WINDOW:2 * WINDOW, :] + (last if sub == n_sub - 1 else 0.0)
                     for sub in range(n_sub)]

    def sink_row(g, half):
        return jnp.where(left, sink_ref[4 * g + half], sink_ref[4 * g + 2 + half]) * LOG2E

    units = [(sub, half) for sub in range(n_sub) for half in range(2)]

    def scores(g, sub, half):
        rows = slice(sub * tq, (sub + 1) * tq)
        qp = jnp.concatenate([q_ref[0, g, rows, 0:LANES], q_ref[0, g, rows, LANES:2 * LANES]],
                             axis=0)
        qh = jnp.where(low, qp, jnp.zeros_like(qp)) if half == 0 else \
            jnp.where(low, jnp.zeros_like(qp), qp)
        sx = _dot_nt(kx_ref[0, g], qh)
        sx_ref[g, sub, half] = sx
        m = jnp.max(sx, axis=0, keepdims=True)
        if local:
            kl = jnp.concatenate([k_refs[sub + t][0, g] for t in range(3)], axis=0)
            sl = _dot_nt(kl, qh)
            sl = jnp.concatenate(
                [sl[0:WINDOW] + bias_prev[sub], sl[WINDOW:2 * WINDOW],
                 sl[2 * WINDOW:3 * WINDOW] + bias_next[sub]], axis=0)
            sl_ref[g, sub, half] = sl
            m = jnp.maximum(m, jnp.max(sl, axis=0, keepdims=True))
        m_ref[g, sub, half] = jnp.maximum(m, sink_row(g, half))

    def with_ones(vt):
        return jnp.concatenate([vt, jnp.ones((ONES_ROWS, vt.shape[1]), BF16)], axis=0)

    def attend(g, sub, half):
        m = m_ref[g, sub, half]
        px = jnp.exp2(sx_ref[g, sub, half] - m)
        o = _dot(with_ones(vtx_ref[0, g]), px.astype(BF16))
        if local:
            vt = jnp.concatenate([vt_refs[sub + t][0, g] for t in range(3)], axis=1)
            pl_ = jnp.exp2(sl_ref[g, sub, half] - m)
            o = o + _dot(with_ones(vt), pl_.astype(BF16))
        denom = o[HEAD_DIM:HEAD_DIM + 1] + jnp.exp2(sink_row(g, half) - m)
        ot_ref[g, sub, half] = o[0:HEAD_DIM] * (1.0 / denom)

    for sub, half in units:
        scores(0, sub, half)

    def body(g, carry):
        for sub, half in units:
            attend(g, sub, half)
            scores(g + 1, sub, half)
        return carry

    lax.fori_loop(0, N_KV_HEADS - 1, body, 0)

    def project_out(g):
        for sub in range(n_sub):
            rows = slice(sub * tq, (sub + 1) * tq)
            o_t = jnp.concatenate([ot_ref[g, sub, 0], ot_ref[g, sub, 1]], axis=0)
            for r in range(2):
                oc = slice(r * LANES, (r + 1) * LANES)
                o_blk = o_t[:, r * tq:(r + 1) * tq].T
                og_ref[g, _half(rows), oc] = _as_words(
                    (o_blk * gz_ref[0, g, rows, oc].astype(F32)).astype(BF16))
        return _dot(_as_bf16(og_ref[g]), w_out_ref[0, g])

    last_head = N_KV_HEADS - 1
    out = None
    for g in range(last_head):
        for sub, half in units[g * len(units) // last_head:(g + 1) * len(units) // last_head]:
            attend(last_head, sub, half)
        part = project_out(g)
        out = part if out is None else out + part
    out = out + project_out(last_head)
    o_ref[0] = _post_norm(x_ref[0], out, mod_ref, g_ref, b_ref)


def _attend_b(q, kk, vt, kkx, vtx, gz, x, mod, j, w_out, sink, ln_g, ln_b, mask, *, local):
    bsz, _, length, group_w = q.shape
    n_ctx = kkx.shape[2]
    tq = ATT_BLOCK
    n_tiles = length // tq
    n_sub = min(ATT_SUB, n_tiles)
    rows = n_sub * tq
    n_steps = length // rows
    grouped_spec = pl.BlockSpec((1, N_KV_HEADS, rows, group_w), lambda b, i: (b, 0, i, 0))
    x_spec = pl.BlockSpec((1, rows, D_MODEL), lambda b, i: (b, i, 0))
    tile = lambda t: (lambda i: jnp.clip(n_sub * i + t - 1, 0, n_tiles - 1))
    k_specs = [pl.BlockSpec((1, N_KV_HEADS, tq, LANES), lambda b, i, f=tile(t): (b, 0, f(i), 0))
               for t in range(n_sub + 2)]
    vt_specs = [pl.BlockSpec((1, N_KV_HEADS, HEAD_DIM, tq),
                             lambda b, i, f=tile(t): (b, 0, 0, f(i))) for t in range(n_sub + 2)]
    in_specs = [pl.BlockSpec(memory_space=pltpu.SMEM), grouped_spec]
    args = [sink, q]
    if local:
        in_specs += k_specs + vt_specs
        args += [kk] * (n_sub + 2) + [vt] * (n_sub + 2)
    in_specs += [pl.BlockSpec((1, N_KV_HEADS, n_ctx, LANES), lambda b, i: (b, 0, 0, 0)),
                 pl.BlockSpec((1, N_KV_HEADS, HEAD_DIM, n_ctx), lambda b, i: (b, 0, 0, 0))]
    args += [kkx, vtx]
    in_specs += [grouped_spec, x_spec, pl.BlockSpec((1, 3, D_MODEL), lambda b, i: (b, 0, 0)),
                 _layer_spec((N_KV_HEADS, group_w, D_MODEL), j), _const_spec((1, D_MODEL)),
                 _const_spec((1, D_MODEL))]
    args += [gz, x, mod, w_out, ln_g, ln_b]
    scratch = [pltpu.VMEM((N_KV_HEADS, n_sub, 2, n_ctx, 2 * tq), F32),
               pltpu.VMEM((N_KV_HEADS, n_sub, 2, 1, 2 * tq), F32),
               pltpu.VMEM((N_KV_HEADS, n_sub, 2, HEAD_DIM, 2 * tq), F32),
               pltpu.VMEM((N_KV_HEADS, rows // 2, group_w), jnp.uint32)]
    if local:
        in_specs.append(_const_spec((2 * WINDOW, 2 * ATT_BLOCK)))
        args.append(mask)
        scratch.append(pltpu.VMEM((N_KV_HEADS, n_sub, 2, 3 * WINDOW, 2 * tq), F32))
    kernel = functools.partial(_attend_b_kernel, local=local, n_steps=n_steps, n_sub=n_sub)
    return pl.pallas_call(
        kernel,
        out_shape=jax.ShapeDtypeStruct(x.shape, F32),
        grid=(bsz, n_steps),
        in_specs=in_specs,
        out_specs=x_spec,
        scratch_shapes=scratch,
        compiler_params=pltpu.CompilerParams(
            dimension_semantics=("parallel", "parallel"), vmem_limit_bytes=VMEM_LIMIT),
        name="attend_b_local" if local else "attend_b_ctx",
    )(*args)


def _rope_tables(length):
    rows = length // GRID_W
    row = jnp.repeat(jnp.arange(rows), GRID_W).astype(F32)
    col = jnp.tile(jnp.arange(GRID_W), rows).astype(F32)
    n_freq = ROPE_AXIS_DIM // 2
    inv = ROPE_BASE ** (-jnp.arange(n_freq, dtype=F32) / n_freq)
    ang_r, ang_c = row[:, None] * inv[None, :], col[:, None] * inv[None, :]
    cr, sr, cc, sc = jnp.cos(ang_r), jnp.sin(ang_r), jnp.cos(ang_c), jnp.sin(ang_c)
    zero = jnp.zeros_like(sr)
    reps = LANES // HEAD_DIM
    cos = jnp.tile(jnp.concatenate([cr, cr, cc, cc], axis=1), (1, reps))
    sin_a = jnp.tile(jnp.concatenate([-sr, zero, -sc, zero], axis=1), (1, reps))
    sin_b = jnp.tile(jnp.concatenate([zero, sr, zero, sc], axis=1), (1, reps))
    return cos, sin_a, sin_b


def _band_mask():
    kj = jnp.arange(WINDOW)[:, None]
    qi = jnp.arange(ATT_BLOCK)[None, :]
    prev = jnp.where(kj >= qi, 0.0, NEG_INF)
    nxt = jnp.where(kj <= qi, 0.0, NEG_INF)
    return jnp.tile(jnp.concatenate([prev, nxt], axis=0), (1, 2)).astype(F32)


def kernel(x, c, ctx, c_ctx, ada_w, ada_b, ln_g, ln_b, a_w_in, a_ln_g, a_ln_b, a_w_s, a_b_s,
           a_w_out, b_w_in, b_sink, b_w_out):
    bsz, length, _ = x.shape
    n_ctx = ctx.shape[1]
    assert bsz + 1 <= MOD_ROWS

    c_rows = jnp.zeros((MOD_ROWS, D_MODEL), F32).at[:bsz].set(c).at[bsz].set(c_ctx)
    mod_all = _modulation(c_rows, ada_w, ada_b)
    tables = _rope_tables(length)
    mask = _band_mask()
    a_w_s_h = a_w_s.astype(BF16)
    b_w_out_h = b_w_out.astype(BF16).reshape(-1, N_KV_HEADS, B_WIDTH // N_KV_HEADS, D_MODEL)

    for i in range(DEPTH):
        j = i // N_MIXERS
        need_ctx_out = i < DEPTH - 1
        mod = mod_all[i, :bsz].reshape(bsz, 3, D_MODEL)
        mod_c = jnp.broadcast_to(mod_all[i, bsz].reshape(1, 3, D_MODEL), (bsz, 3, D_MODEL))
        g_i, b_i = ln_g[i].reshape(1, D_MODEL), ln_b[i].reshape(1, D_MODEL)
        if i % N_MIXERS == 0:
            b_s_full = jnp.repeat(a_b_s[j].T, A_GW, axis=1)
            weights = (j, a_w_in, a_ln_g[j].reshape(1, A_WIDTH), a_ln_b[j].reshape(1, A_WIDTH),
                       a_w_s_h, b_s_full, a_w_out, g_i, b_i)
            x_new = _layer_a(x, mod, *weights, tm=512)
            if need_ctx_out:
                ctx = _layer_a(ctx, mod_c, *weights, tm=n_ctx)
            x = x_new
        else:
            sink = b_sink[j].astype(F32)
            q, kk, vt, gz = _project_b(x, mod, j, b_w_in, tables, tm=512, rope=True)
            qc, kkc, vtc, gzc = _project_b(ctx, mod_c, j, b_w_in, tables, tm=n_ctx, rope=False)
            x = _attend_b(q, kk, vt, kkc, vtc, gz, x, mod, j, b_w_out_h, sink, g_i, b_i, mask,
                          local=True)
            if need_ctx_out:
                ctx = _attend_b(qc, None, None, kkc, vtc, gzc, ctx, mod_c, j, b_w_out_h, sink,
                                g_i, b_i, None, local=False)
    return x
```

```python
import functools
import math

import jax
import jax.numpy as jnp
from jax import lax
from jax.experimental import pallas as pl
from jax.experimental.pallas import tpu as pltpu

D_MODEL = 1024
DEPTH = 4
GRID_W = 64
N_MIXERS = 2
CHUNK = 128
A_WIDTH = 2 * D_MODEL
A_GROUPS = 8
A_GW = A_WIDTH // A_GROUPS
HEAD_DIM = 64
N_HEADS = D_MODEL // HEAD_DIM
N_KV_HEADS = N_HEADS // 4
B_WIDTH = N_HEADS * HEAD_DIM
KV_WIDTH = N_KV_HEADS * HEAD_DIM
WINDOW = 128
ATT_BLOCK = 128
ATT_SUB = 4
ROPE_AXIS_DIM = HEAD_DIM // 2
ROPE_BASE = 10000.0
LN_EPS = 1e-5
NEG_INF = -1e30
ALPHA = (2.0 * DEPTH) ** 0.25

LANES = 128
MOD_ROWS = 8
VMEM_LIMIT = 48 * 1024 * 1024

BF16 = jnp.bfloat16
F32 = jnp.float32

LOG2E = math.log2(math.e)
Q_SCALE = HEAD_DIM ** -0.5 * LOG2E
ONES_ROWS = 16
A_SUB_ROWS = 256
COPY_ROWS = 128

_GELU_C0 = math.sqrt(2.0 / math.pi)
_GELU_C1 = _GELU_C0 * 0.044715


def _dot(a, b):
    return jnp.dot(a, b, preferred_element_type=F32)


def _dot_nt(a, b):
    return lax.dot_general(a, b, (((1,), (1,)), ((), ())), preferred_element_type=F32)


def _gelu(x):
    inner = x * (_GELU_C1 * (x * x) + _GELU_C0)
    hx = 0.5 * x
    return hx + hx * jnp.tanh(inner)


def _silu(z):
    hz = 0.5 * z
    return hz + hz * jnp.tanh(hz)


def _layer_norm(x, g, b):
    mu = jnp.mean(x, axis=-1, keepdims=True)
    xc = x - mu
    var = jnp.mean(xc * xc, axis=-1, keepdims=True)
    return xc * lax.rsqrt(var + LN_EPS) * g + b


def _const_spec(shape):
    nd = len(shape)
    return pl.BlockSpec(shape, lambda *_: (0,) * nd, pipeline_mode=pl.Buffered(1))


def _as_words(t):
    return pltpu.bitcast(t, jnp.uint32)


def _as_bf16(t):
    return pltpu.bitcast(t, BF16)


def _half(rows):
    return slice(rows.start // 2, rows.stop // 2)


def _stage_weight_once(w_ref, w_s):
    @pl.when((pl.program_id(0) == 0) & (pl.program_id(1) == 0))
    def _():
        def copy_rows(r, carry):
            rows = pl.ds(pl.multiple_of(r * COPY_ROWS, COPY_ROWS), COPY_ROWS)
            half_rows = pl.ds(pl.multiple_of(r * (COPY_ROWS // 2), COPY_ROWS // 2), COPY_ROWS // 2)
            w_s[half_rows, :] = _as_words(w_ref[0, rows, :])
            return carry

        lax.fori_loop(0, w_ref.shape[1] // COPY_ROWS, copy_rows, 0)


def _layer_spec(shape, j):
    nd = len(shape)
    return pl.BlockSpec((1,) + tuple(shape), lambda *_: (j,) + (0,) * nd,
                        pipeline_mode=pl.Buffered(1))


def _modulation_kernel(c_ref, w_ref, b_ref, o_ref):
    c = c_ref[...]
    cond = _silu(c).astype(BF16)
    o_ref[0] = _dot(cond, w_ref[0].astype(BF16)) + b_ref[0]


def _modulation(c_rows, ada_w, ada_b):
    tn = D_MODEL
    return pl.pallas_call(
        _modulation_kernel,
        out_shape=jax.ShapeDtypeStruct((DEPTH, MOD_ROWS, 3 * D_MODEL), F32),
        grid=(DEPTH, 3 * D_MODEL // tn),
        in_specs=[
            pl.BlockSpec((MOD_ROWS, D_MODEL), lambda i, j: (0, 0)),
            pl.BlockSpec((1, D_MODEL, tn), lambda i, j: (i, 0, j)),
            pl.BlockSpec((1, 1, tn), lambda i, j: (i, 0, j)),
        ],
        out_specs=pl.BlockSpec((1, MOD_ROWS, tn), lambda i, j: (i, 0, j)),
        compiler_params=pltpu.CompilerParams(
            dimension_semantics=("parallel", "parallel"), vmem_limit_bytes=VMEM_LIMIT),
        name="modulation",
    )(c_rows, ada_w, ada_b.reshape(DEPTH, 1, 3 * D_MODEL))


def _modulate(x, mod_ref):
    shift = mod_ref[0, 0:1, :]
    scale = mod_ref[0, 1:2, :]
    return x * (1.0 + scale) + shift


def _post_norm(x, out, mod_ref, g_ref, b_ref):
    gate = mod_ref[0, 2:3, :]
    return _layer_norm(ALPHA * x + gate * out, g_ref[...], b_ref[...])


def _layer_a_kernel(x_ref, mod_ref, w_in_ref, lng_ref, lnb_ref, ws_ref, bs_ref, w_out_ref,
                    g_ref, b_ref, o_ref, vn_ref, y_ref, w_in_s, w_out_s, *, tm, ts):
    _stage_weight_once(w_in_ref, w_in_s)
    _stage_weight_once(w_out_ref, w_out_s)

    w_in = lambda lo, hi: _as_bf16(w_in_s[:, lo:hi])
    subs = [slice(t * ts, (t + 1) * ts) for t in range(tm // ts)]

    def spatial_norm(sub):
        h = _modulate(x_ref[0, sub, :], mod_ref).astype(BF16)
        v = _gelu(_dot(h, w_in(A_WIDTH, 2 * A_WIDTH)))
        vn_ref[_half(sub), :] = _as_words(
            _layer_norm(v, lng_ref[...], lnb_ref[...]).astype(BF16))
        return h

    def gated_mix(sub, h):
        for g in range(A_GROUPS):
            cols = slice(g * A_GW, (g + 1) * A_GW)
            u = _gelu(_dot(h, w_in(g * A_GW, (g + 1) * A_GW)))
            z = _dot(h, w_in(2 * A_WIDTH + g * A_GW, 2 * A_WIDTH + (g + 1) * A_GW))
            uz = u * _silu(z)
            for c in range(ts // CHUNK):
                rows = _half(slice(sub.start + c * CHUNK, sub.start + (c + 1) * CHUNK))
                s = _dot(ws_ref[0, g], _as_bf16(vn_ref[rows, cols])) + bs_ref[:, cols]
                y_ref[rows, cols] = _as_words((uz[c * CHUNK:(c + 1) * CHUNK] * s).astype(BF16))

    def project_out(sub):
        out = _dot(_as_bf16(y_ref[_half(sub), :]), _as_bf16(w_out_s[...]))
        o_ref[0, sub, :] = _post_norm(x_ref[0, sub, :], out, mod_ref, g_ref, b_ref)

    h = spatial_norm(subs[0])
    for t, sub in enumerate(subs):
        gated_mix(sub, h)
        if t + 1 < len(subs):
            h = spatial_norm(subs[t + 1])
        project_out(sub)


def _layer_a(x, mod, j, w_in, a_ln_g, a_ln_b, w_s, b_s_full, w_out, ln_g, ln_b, *, tm):
    bsz, length, _ = x.shape
    kernel = functools.partial(_layer_a_kernel, tm=tm, ts=min(tm, A_SUB_ROWS))
    return pl.pallas_call(
        kernel,
        out_shape=jax.ShapeDtypeStruct(x.shape, F32),
        grid=(bsz, length // tm),
        in_specs=[
            pl.BlockSpec((1, tm, D_MODEL), lambda b, t: (b, t, 0)),
            pl.BlockSpec((1, 3, D_MODEL), lambda b, t: (b, 0, 0)),
            _layer_spec((D_MODEL, 3 * A_WIDTH), j),
            _const_spec((1, A_WIDTH)),
            _const_spec((1, A_WIDTH)),
            _layer_spec((A_GROUPS, CHUNK, CHUNK), j),
            _const_spec((CHUNK, A_WIDTH)),
            _layer_spec((A_WIDTH, D_MODEL), j),
            _const_spec((1, D_MODEL)),
            _const_spec((1, D_MODEL)),
        ],
        out_specs=pl.BlockSpec((1, tm, D_MODEL), lambda b, t: (b, t, 0)),
        scratch_shapes=[pltpu.VMEM((tm // 2, A_WIDTH), jnp.uint32),
                        pltpu.VMEM((tm // 2, A_WIDTH), jnp.uint32),
                        pltpu.VMEM((D_MODEL // 2, 3 * A_WIDTH), jnp.uint32),
                        pltpu.VMEM((A_WIDTH // 2, D_MODEL), jnp.uint32)],
        compiler_params=pltpu.CompilerParams(
            dimension_semantics=("arbitrary", "arbitrary"), vmem_limit_bytes=VMEM_LIMIT),
        name="layer_a",
    )(x, mod, w_in, a_ln_g, a_ln_b, w_s, b_s_full, w_out, ln_g, ln_b)


def _rope(t, cos, sin_a, sin_b):
    n = t.shape[1] // LANES
    half = ROPE_AXIS_DIM // 2
    outs = []
    for j in range(n):
        blk = t[:, j * LANES:(j + 1) * LANES]
        up = pltpu.roll(blk, LANES - half, 1)
        dn = pltpu.roll(blk, half, 1)
        outs.append(blk * cos + up * sin_a + dn * sin_b)
    return jnp.concatenate(outs, axis=1)


def _project_b_kernel(x_ref, mod_ref, w_ref, cos_ref, sa_ref, sb_ref,
                      q_ref, kk_ref, vt_ref, gz_ref, w_s, *, rope):
    _stage_weight_once(w_ref, w_s)
    x = x_ref[0]
    h = _modulate(x, mod_ref).astype(BF16)
    w = lambda lo, hi: _as_bf16(w_s[:, lo:hi])
    q = _dot(h, w(0, B_WIDTH))
    k = _dot(h, w(B_WIDTH, B_WIDTH + KV_WIDTH))
    v = _dot(h, w(B_WIDTH + KV_WIDTH, B_WIDTH + 2 * KV_WIDTH))
    z = _dot(h, w(B_WIDTH + 2 * KV_WIDTH, 2 * B_WIDTH + 2 * KV_WIDTH))
    if rope:
        cos, sa, sb = cos_ref[...], sa_ref[...], sb_ref[...]
        q = _rope(q, cos, sa, sb)
        k = _rope(k, cos, sa, sb)
    q = (q * Q_SCALE).astype(BF16)
    gz = _silu(z).astype(BF16)
    v_t = v.T.astype(BF16)
    low = lax.broadcasted_iota(jnp.int32, (1, LANES), 1) < HEAD_DIM
    group_w = B_WIDTH // N_KV_HEADS
    for g in range(N_KV_HEADS):
        q_ref[0, g] = q[:, g * group_w:(g + 1) * group_w]
        gz_ref[0, g] = gz[:, g * group_w:(g + 1) * group_w]
        vt_ref[0, g] = v_t[g * HEAD_DIM:(g + 1) * HEAD_DIM, :]
    for j in range(KV_WIDTH // LANES):
        blk = k[:, j * LANES:(j + 1) * LANES]
        swapped = pltpu.roll(blk, HEAD_DIM, 1)
        kk_ref[0, 2 * j] = jnp.where(low, blk, swapped).astype(BF16)
        kk_ref[0, 2 * j + 1] = jnp.where(low, swapped, blk).astype(BF16)


def _project_b(x, mod, j, w_in, tables, *, tm, rope):
    bsz, length, _ = x.shape
    cos, sa, sb = tables
    group_w = B_WIDTH // N_KV_HEADS
    grouped = lambda width: jax.ShapeDtypeStruct((bsz, N_KV_HEADS, length, width), BF16)
    grouped_spec = lambda width: pl.BlockSpec((1, N_KV_HEADS, tm, width),
                                              lambda b, t: (b, 0, t, 0))
    tab_spec = pl.BlockSpec((tm, LANES), lambda b, t: (t, 0))
    kernel = functools.partial(_project_b_kernel, rope=rope)
    return pl.pallas_call(
        kernel,
        out_shape=(grouped(group_w), grouped(LANES),
                   jax.ShapeDtypeStruct((bsz, N_KV_HEADS, HEAD_DIM, length), BF16),
                   grouped(group_w)),
        grid=(bsz, length // tm),
        in_specs=[
            pl.BlockSpec((1, tm, D_MODEL), lambda b, t: (b, t, 0)),
            pl.BlockSpec((1, 3, D_MODEL), lambda b, t: (b, 0, 0)),
            _layer_spec((D_MODEL, 2 * B_WIDTH + 2 * KV_WIDTH), j),
            tab_spec, tab_spec, tab_spec,
        ],
        out_specs=(grouped_spec(group_w), grouped_spec(LANES),
                   pl.BlockSpec((1, N_KV_HEADS, HEAD_DIM, tm), lambda b, t: (b, 0, 0, t)),
                   grouped_spec(group_w)),
        scratch_shapes=[pltpu.VMEM((D_MODEL // 2, 2 * B_WIDTH + 2 * KV_WIDTH), jnp.uint32)],
        compiler_params=pltpu.CompilerParams(
            dimension_semantics=("arbitrary", "arbitrary"), vmem_limit_bytes=VMEM_LIMIT),
        name="project_b_rope" if rope else "project_b",
    )(x, mod, w_in, cos, sa, sb)


def _attend_b_kernel(*refs, local, n_steps, n_sub):
    if local:
        (sink_ref, q_ref, *tile_refs, kx_ref, vtx_ref, gz_ref, x_ref, mod_ref, w_out_ref,
         g_ref, b_ref, mask_ref, o_ref, sx_ref, m_ref, ot_ref, og_ref, sl_ref) = refs
        k_refs, vt_refs = tile_refs[:n_sub + 2], tile_refs[n_sub + 2:]
    else:
        (sink_ref, q_ref, kx_ref, vtx_ref, gz_ref, x_ref, mod_ref, w_out_ref, g_ref, b_ref,
         o_ref, sx_ref, m_ref, ot_ref, og_ref) = refs

    tq = ATT_BLOCK
    low = lax.broadcasted_iota(jnp.int32, (1, LANES), 1) < HEAD_DIM
    left = lax.broadcasted_iota(jnp.int32, (1, 2 * tq), 1) < tq
    if local:
        i = pl.program_id(1)
        first = jnp.where(i == 0, NEG_INF, 0.0).astype(F32)
        last = jnp.where(i == n_steps - 1, NEG_INF, 0.0).astype(F32)
        bias_prev = [mask_ref[0:WINDOW, :] + (first if sub == 0 else 0.0)
                     for sub in range(n_sub)]
        bias_next = [mask_ref[WINDOW:2 * WINDOW, :] + (last if sub == n_sub - 1 else 0.0)
                     for sub in range(n_sub)]

    def sink_row(g, half):
        return jnp.where(left, sink_ref[4 * g + half], sink_ref[4 * g + 2 + half]) * LOG2E

    units = [(sub, half) for sub in range(n_sub) for half in range(2)]

    def scores(g, sub, half):
        rows = slice(sub * tq, (sub + 1) * tq)
        qp = jnp.concatenate([q_ref[0, g, rows, 0:LANES], q_ref[0, g, rows, LANES:2 * LANES]],
                             axis=0)
        qh = jnp.where(low, qp, jnp.zeros_like(qp)) if half == 0 else \
            jnp.where(low, jnp.zeros_like(qp), qp)
        sx = _dot_nt(kx_ref[0, g], qh)
        sx_ref[g, sub, half] = sx
        m = jnp.max(sx, axis=0, keepdims=True)
        if local:
            kl = jnp.concatenate([k_refs[sub + t][0, g] for t in range(3)], axis=0)
            sl = _dot_nt(kl, qh)
            sl = jnp.concatenate(
                [sl[0:WINDOW] + bias_prev[sub], sl[WINDOW:2 * WINDOW],
                 sl[2 * WINDOW:3 * WINDOW] + bias_next[sub]], axis=0)
            sl_ref[g, sub, half] = sl
            m = jnp.maximum(m, jnp.max(sl, axis=0, keepdims=True))
        m_ref[g, sub, half] = jnp.maximum(m, sink_row(g, half))

    def with_ones(vt):
        return jnp.concatenate([vt, jnp.ones((ONES_ROWS, vt.shape[1]), BF16)], axis=0)

    def attend(g, sub, half):
        m = m_ref[g, sub, half]
        px = jnp.exp2(sx_ref[g, sub, half] - m)
        o = _dot(with_ones(vtx_ref[0, g]), px.astype(BF16))
        if local:
            vt = jnp.concatenate([vt_refs[sub + t][0, g] for t in range(3)], axis=1)
            pl_ = jnp.exp2(sl_ref[g, sub, half] - m)
            o = o + _dot(with_ones(vt), pl_.astype(BF16))
        denom = o[HEAD_DIM:HEAD_DIM + 1] + jnp.exp2(sink_row(g, half) - m)
        ot_ref[g, sub, half] = o[0:HEAD_DIM] * (1.0 / denom)

    for sub, half in units:
        scores(0, sub, half)

    def body(g, carry):
        for sub, half in units:
            attend(g, sub, half)
            scores(g + 1, sub, half)
        return carry

    lax.fori_loop(0, N_KV_HEADS - 1, body, 0)

    def project_out(g):
        for sub in range(n_sub):
            rows = slice(sub * tq, (sub + 1) * tq)
            o_t = jnp.concatenate([ot_ref[g, sub, 0], ot_ref[g, sub, 1]], axis=0)
            for r in range(2):
                oc = slice(r * LANES, (r + 1) * LANES)
                o_blk = o_t[:, r * tq:(r + 1) * tq].T
                og_ref[g, _half(rows), oc] = _as_words(
                    (o_blk * gz_ref[0, g, rows, oc].astype(F32)).astype(BF16))
        return _dot(_as_bf16(og_ref[g]), w_out_ref[0, g])

    last = N_KV_HEADS - 1
    out = None
    for g in range(last):
        for sub, half in units[g * len(units) // last:(g + 1) * len(units) // last]:
            attend(last, sub, half)
        part = project_out(g)
        out = part if out is None else out + part
    out = out + project_out(last)
    o_ref[0] = _post_norm(x_ref[0], out, mod_ref, g_ref, b_ref)


def _attend_b(q, kk, vt, kkx, vtx, gz, x, mod, j, w_out, sink, ln_g, ln_b, mask, *, local):
    bsz, _, length, group_w = q.shape
    n_ctx = kkx.shape[2]
    tq = ATT_BLOCK
    n_tiles = length // tq
    n_sub = min(ATT_SUB, n_tiles)
    rows = n_sub * tq
    n_steps = length // rows
    grouped_spec = pl.BlockSpec((1, N_KV_HEADS, rows, group_w), lambda b, i: (b, 0, i, 0))
    x_spec = pl.BlockSpec((1, rows, D_MODEL), lambda b, i: (b, i, 0))
    tile = lambda t: (lambda i: jnp.clip(n_sub * i + t - 1, 0, n_tiles - 1))
    k_specs = [pl.BlockSpec((1, N_KV_HEADS, tq, LANES), lambda b, i, f=tile(t): (b, 0, f(i), 0))
               for t in range(n_sub + 2)]
    vt_specs = [pl.BlockSpec((1, N_KV_HEADS, HEAD_DIM, tq),
                             lambda b, i, f=tile(t): (b, 0, 0, f(i))) for t in range(n_sub + 2)]
    in_specs = [pl.BlockSpec(memory_space=pltpu.SMEM), grouped_spec]
    args = [sink, q]
    if local:
        in_specs += k_specs + vt_specs
        args += [kk] * (n_sub + 2) + [vt] * (n_sub + 2)
    in_specs += [pl.BlockSpec((1, N_KV_HEADS, n_ctx, LANES), lambda b, i: (b, 0, 0, 0)),
                 pl.BlockSpec((1, N_KV_HEADS, HEAD_DIM, n_ctx), lambda b, i: (b, 0, 0, 0))]
    args += [kkx, vtx]
    in_specs += [grouped_spec, x_spec, pl.BlockSpec((1, 3, D_MODEL), lambda b, i: (b, 0, 0)),
                 _layer_spec((N_KV_HEADS, group_w, D_MODEL), j), _const_spec((1, D_MODEL)),
                 _const_spec((1, D_MODEL))]
    args += [gz, x, mod, w_out, ln_g, ln_b]
    scratch = [pltpu.VMEM((N_KV_HEADS, n_sub, 2, n_ctx, 2 * tq), F32),
               pltpu.VMEM((N_KV_HEADS, n_sub, 2, 1, 2 * tq), F32),
               pltpu.VMEM((N_KV_HEADS, n_sub, 2, HEAD_DIM, 2 * tq), F32),
               pltpu.VMEM((N_KV_HEADS, rows // 2, group_w), jnp.uint32)]
    if local:
        in_specs.append(_const_spec((2 * WINDOW, 2 * ATT_BLOCK)))
        args.append(mask)
        scratch.append(pltpu.VMEM((N_KV_HEADS, n_sub, 2, 3 * WINDOW, 2 * tq), F32))
    kernel = functools.partial(_attend_b_kernel, local=local, n_steps=n_steps, n_sub=n_sub)
    return pl.pallas_call(
        kernel,
        out_shape=jax.ShapeDtypeStruct(x.shape, F32),
        grid=(bsz, n_steps),
        in_specs=in_specs,
        out_specs=x_spec,
        scratch_shapes=scratch,
        compiler_params=pltpu.CompilerParams(
            dimension_semantics=("parallel", "parallel"), vmem_limit_bytes=VMEM_LIMIT),
        name="attend_b_local" if local else "attend_b_ctx",
    )(*args)


def _rope_tables(length):
    rows = length // GRID_W
    row = jnp.repeat(jnp.arange(rows), GRID_W).astype(F32)
    col = jnp.tile(jnp.arange(GRID_W), rows).astype(F32)
    n_freq = ROPE_AXIS_DIM // 2
    inv = ROPE_BASE ** (-jnp.arange(n_freq, dtype=F32) / n_freq)
    ang_r, ang_c = row[:, None] * inv[None, :], col[:, None] * inv[None, :]
    cr, sr, cc, sc = jnp.cos(ang_r), jnp.sin(ang_r), jnp.cos(ang_c), jnp.sin(ang_c)
    zero = jnp.zeros_like(sr)
    reps = LANES // HEAD_DIM
    cos = jnp.tile(jnp.concatenate([cr, cr, cc, cc], axis=1), (1, reps))
    sin_a = jnp.tile(jnp.concatenate([-sr, zero, -sc, zero], axis=1), (1, reps))
    sin_b = jnp.tile(jnp.concatenate([zero, sr, zero, sc], axis=1), (1, reps))
    return cos, sin_a, sin_b


def _band_mask():
    kj = jnp.arange(WINDOW)[:, None]
    qi = jnp.arange(ATT_BLOCK)[None, :]
    prev = jnp.where(kj >= qi, 0.0, NEG_INF)
    nxt = jnp.where(kj <= qi, 0.0, NEG_INF)
    return jnp.tile(jnp.concatenate([prev, nxt], axis=0), (1, 2)).astype(F32)


def kernel(x, c, ctx, c_ctx, ada_w, ada_b, ln_g, ln_b, a_w_in, a_ln_g, a_ln_b, a_w_s, a_b_s,
           a_w_out, b_w_in, b_sink, b_w_out):
    bsz, length, _ = x.shape
    n_ctx = ctx.shape[1]
    assert bsz + 1 <= MOD_ROWS

    c_rows = jnp.zeros((MOD_ROWS, D_MODEL), F32).at[:bsz].set(c).at[bsz].set(c_ctx)
    mod_all = _modulation(c_rows, ada_w, ada_b)
    tables = _rope_tables(length)
    mask = _band_mask()
    a_w_in_h, a_w_s_h, a_w_out_h = (w.astype(BF16) for w in (a_w_in, a_w_s, a_w_out))
    b_w_in_h = b_w_in.astype(BF16)
    b_w_out_h = b_w_out.astype(BF16).reshape(-1, N_KV_HEADS, B_WIDTH // N_KV_HEADS, D_MODEL)

    for i in range(DEPTH):
        j = i // N_MIXERS
        need_ctx_out = i < DEPTH - 1
        mod = mod_all[i, :bsz].reshape(bsz, 3, D_MODEL)
        mod_c = jnp.broadcast_to(mod_all[i, bsz].reshape(1, 3, D_MODEL), (bsz, 3, D_MODEL))
        g_i, b_i = ln_g[i].reshape(1, D_MODEL), ln_b[i].reshape(1, D_MODEL)
        if i % N_MIXERS == 0:
            b_s_full = jnp.repeat(a_b_s[j].T, A_GW, axis=1)
            weights = (j, a_w_in_h, a_ln_g[j].reshape(1, A_WIDTH), a_ln_b[j].reshape(1, A_WIDTH),
                       a_w_s_h, b_s_full, a_w_out_h, g_i, b_i)
            x_new = _layer_a(x, mod, *weights, tm=512)
            if need_ctx_out:
                ctx = _layer_a(ctx, mod_c, *weights, tm=n_ctx)
            x = x_new
        else:
            sink = b_sink[j].astype(F32)
            q, kk, vt, gz = _project_b(x, mod, j, b_w_in_h, tables, tm=1024, rope=True)
            qc, kkc, vtc, gzc = _project_b(ctx, mod_c, j, b_w_in_h, tables, tm=n_ctx, rope=False)
            x = _attend_b(q, kk, vt, kkc, vtc, gz, x, mod, j, b_w_out_h, sink, g_i, b_i, mask,
                          local=True)
            if need_ctx_out:
                ctx = _attend_b(qc, None, None, kkc, vtc, gzc, ctx, mod_c, j, b_w_out_h, sink,
                                g_i, b_i, None, local=False)
    return x
```

```python
import functools
import math

import jax
import jax.numpy as jnp
from jax import lax
from jax.experimental import pallas as pl
from jax.experimental.pallas import tpu as pltpu

D_MODEL = 1024
DEPTH = 4
GRID_W = 64
N_MIXERS = 2
CHUNK = 128
A_WIDTH = 2 * D_MODEL
A_GROUPS = 8
A_GW = A_WIDTH // A_GROUPS
HEAD_DIM = 64
N_HEADS = D_MODEL // HEAD_DIM
N_KV_HEADS = N_HEADS // 4
B_WIDTH = N_HEADS * HEAD_DIM
KV_WIDTH = N_KV_HEADS * HEAD_DIM
WINDOW = 128
ATT_BLOCK = 128
ROPE_AXIS_DIM = HEAD_DIM // 2
ROPE_BASE = 10000.0
LN_EPS = 1e-5
NEG_INF = -1e30
ALPHA = (2.0 * DEPTH) ** 0.25

LANES = 128
MOD_ROWS = 8
VMEM_LIMIT = 48 * 1024 * 1024

BF16 = jnp.bfloat16
F32 = jnp.float32

LOG2E = math.log2(math.e)
Q_SCALE = HEAD_DIM ** -0.5 * LOG2E
ONES_ROWS = 16
A_TILE_ROWS = 512
A_SUB_ROWS = 256
B_TILE_ROWS = 1024
ATT_SUB = 4
COPY_ROWS = 128

_GELU_C0 = math.sqrt(2.0 / math.pi)
_GELU_C1 = _GELU_C0 * 0.044715


def _dot(a, b):
    return jnp.dot(a, b, preferred_element_type=F32)


def _dot_nt(a, b):
    return lax.dot_general(a, b, (((1,), (1,)), ((), ())), preferred_element_type=F32)


def _gelu(x):
    inner = x * (_GELU_C1 * (x * x) + _GELU_C0)
    hx = 0.5 * x
    return hx + hx * jnp.tanh(inner)


def _silu(z):
    hz = 0.5 * z
    return hz + hz * jnp.tanh(hz)


def _layer_norm(x, g, b):
    mu = jnp.mean(x, axis=-1, keepdims=True)
    xc = x - mu
    var = jnp.mean(xc * xc, axis=-1, keepdims=True)
    return xc * lax.rsqrt(var + LN_EPS) * g + b


def _const_spec(shape):
    nd = len(shape)
    return pl.BlockSpec(shape, lambda *_: (0,) * nd, pipeline_mode=pl.Buffered(1))


def _as_words(t):
    return pltpu.bitcast(t, jnp.uint32)


def _as_bf16(t):
    return pltpu.bitcast(t, BF16)


def _half(rows):
    return slice(rows.start // 2, rows.stop // 2)


def _stage_weight_once(w_ref, w_s):
    @pl.when((pl.program_id(0) == 0) & (pl.program_id(1) == 0))
    def _():
        def copy_rows(r, carry):
            rows = pl.ds(pl.multiple_of(r * COPY_ROWS, COPY_ROWS), COPY_ROWS)
            half_rows = pl.ds(pl.multiple_of(r * (COPY_ROWS // 2), COPY_ROWS // 2), COPY_ROWS // 2)
            w_s[half_rows, :] = _as_words(w_ref[0, rows, :].astype(BF16))
            return carry

        lax.fori_loop(0, w_ref.shape[1] // COPY_ROWS, copy_rows, 0)


def _layer_spec(shape, j):
    nd = len(shape)
    return pl.BlockSpec((1,) + tuple(shape), lambda *_: (j,) + (0,) * nd,
                        pipeline_mode=pl.Buffered(1))


def _modulation_kernel(c_ref, w_ref, b_ref, o_ref):
    c = c_ref[...]
    cond = _silu(c).astype(BF16)
    o_ref[0] = _dot(cond, w_ref[0].astype(BF16)) + b_ref[0]


def _modulation(c_rows, ada_w, ada_b):
    tn = D_MODEL
    return pl.pallas_call(
        _modulation_kernel,
        out_shape=jax.ShapeDtypeStruct((DEPTH, MOD_ROWS, 3 * D_MODEL), F32),
        grid=(DEPTH, 3 * D_MODEL // tn),
        in_specs=[
            pl.BlockSpec((MOD_ROWS, D_MODEL), lambda i, j: (0, 0)),
            pl.BlockSpec((1, D_MODEL, tn), lambda i, j: (i, 0, j)),
            pl.BlockSpec((1, 1, tn), lambda i, j: (i, 0, j)),
        ],
        out_specs=pl.BlockSpec((1, MOD_ROWS, tn), lambda i, j: (i, 0, j)),
        compiler_params=pltpu.CompilerParams(
            dimension_semantics=("parallel", "parallel"), vmem_limit_bytes=VMEM_LIMIT),
        name="modulation",
    )(c_rows, ada_w, ada_b.reshape(DEPTH, 1, 3 * D_MODEL))


def _modulate(x, mod_ref):
    shift = mod_ref[0, 0:1, :]
    scale = mod_ref[0, 1:2, :]
    return x * (1.0 + scale) + shift


def _post_norm(x, out, mod_ref, g_ref, b_ref):
    gate = mod_ref[0, 2:3, :]
    return _layer_norm(ALPHA * x + gate * out, g_ref[...], b_ref[...])


def _layer_a_kernel(x_ref, mod_ref, w_in_ref, lng_ref, lnb_ref, ws_ref, bs_ref, w_out_ref,
                    g_ref, b_ref, o_ref, vn_ref, y_ref, w_in_s, w_out_s, *, tm, ts):
    _stage_weight_once(w_in_ref, w_in_s)
    _stage_weight_once(w_out_ref, w_out_s)

    w_in = lambda lo, hi: _as_bf16(w_in_s[:, lo:hi])
    subs = [slice(t * ts, (t + 1) * ts) for t in range(tm // ts)]

    def spatial_norm(sub):
        h = _modulate(x_ref[0, sub, :], mod_ref).astype(BF16)
        v = _gelu(_dot(h, w_in(A_WIDTH, 2 * A_WIDTH)))
        vn_ref[_half(sub), :] = _as_words(
            _layer_norm(v, lng_ref[...], lnb_ref[...]).astype(BF16))
        return h

    def gated_mix(sub, h):
        for g in range(A_GROUPS):
            cols = slice(g * A_GW, (g + 1) * A_GW)
            u = _gelu(_dot(h, w_in(g * A_GW, (g + 1) * A_GW)))
            z = _dot(h, w_in(2 * A_WIDTH + g * A_GW, 2 * A_WIDTH + (g + 1) * A_GW))
            uz = u * _silu(z)
            for c in range(ts // CHUNK):
                rows = _half(slice(sub.start + c * CHUNK, sub.start + (c + 1) * CHUNK))
                s = _dot(ws_ref[0, g], _as_bf16(vn_ref[rows, cols])) + bs_ref[:, cols]
                y_ref[rows, cols] = _as_words((uz[c * CHUNK:(c + 1) * CHUNK] * s).astype(BF16))

    def project_out(sub):
        out = _dot(_as_bf16(y_ref[_half(sub), :]), _as_bf16(w_out_s[...]))
        o_ref[0, sub, :] = _post_norm(x_ref[0, sub, :], out, mod_ref, g_ref, b_ref)

    h = spatial_norm(subs[0])
    for t, sub in enumerate(subs):
        gated_mix(sub, h)
        if t + 1 < len(subs):
            h = spatial_norm(subs[t + 1])
        project_out(sub)


def _layer_a(x, mod, j, w_in, a_ln_g, a_ln_b, w_s, b_s_full, w_out, ln_g, ln_b, *, tm):
    bsz, length, _ = x.shape
    kernel = functools.partial(_layer_a_kernel, tm=tm, ts=min(tm, A_SUB_ROWS))
    return pl.pallas_call(
        kernel,
        out_shape=jax.ShapeDtypeStruct(x.shape, F32),
        grid=(bsz, length // tm),
        in_specs=[
            pl.BlockSpec((1, tm, D_MODEL), lambda b, t: (b, t, 0)),
            pl.BlockSpec((1, 3, D_MODEL), lambda b, t: (b, 0, 0)),
            _layer_spec((D_MODEL, 3 * A_WIDTH), j),
            _const_spec((1, A_WIDTH)),
            _const_spec((1, A_WIDTH)),
            _layer_spec((A_GROUPS, CHUNK, CHUNK), j),
            _const_spec((CHUNK, A_WIDTH)),
            _layer_spec((A_WIDTH, D_MODEL), j),
            _const_spec((1, D_MODEL)),
            _const_spec((1, D_MODEL)),
        ],
        out_specs=pl.BlockSpec((1, tm, D_MODEL), lambda b, t: (b, t, 0)),
        scratch_shapes=[pltpu.VMEM((tm // 2, A_WIDTH), jnp.uint32),
                        pltpu.VMEM((tm // 2, A_WIDTH), jnp.uint32),
                        pltpu.VMEM((D_MODEL // 2, 3 * A_WIDTH), jnp.uint32),
                        pltpu.VMEM((A_WIDTH // 2, D_MODEL), jnp.uint32)],
        compiler_params=pltpu.CompilerParams(
            dimension_semantics=("arbitrary", "arbitrary"), vmem_limit_bytes=VMEM_LIMIT),
        name="layer_a",
    )(x, mod, w_in, a_ln_g, a_ln_b, w_s, b_s_full, w_out, ln_g, ln_b)


def _rope(t, cos, sin_a, sin_b):
    n = t.shape[1] // LANES
    half = ROPE_AXIS_DIM // 2
    outs = []
    for j in range(n):
        blk = t[:, j * LANES:(j + 1) * LANES]
        up = pltpu.roll(blk, LANES - half, 1)
        dn = pltpu.roll(blk, half, 1)
        outs.append(blk * cos + up * sin_a + dn * sin_b)
    return jnp.concatenate(outs, axis=1)


def _project_b_kernel(x_ref, mod_ref, w_ref, cos_ref, sa_ref, sb_ref,
                      q_ref, kk_ref, vt_ref, gz_ref, w_s, *, rope):
    _stage_weight_once(w_ref, w_s)
    x = x_ref[0]
    h = _modulate(x, mod_ref).astype(BF16)
    w = lambda lo, hi: _as_bf16(w_s[:, lo:hi])
    q = _dot(h, w(0, B_WIDTH))
    k = _dot(h, w(B_WIDTH, B_WIDTH + KV_WIDTH))
    v = _dot(h, w(B_WIDTH + KV_WIDTH, B_WIDTH + 2 * KV_WIDTH))
    z = _dot(h, w(B_WIDTH + 2 * KV_WIDTH, 2 * B_WIDTH + 2 * KV_WIDTH))
    if rope:
        cos, sa, sb = cos_ref[...], sa_ref[...], sb_ref[...]
        q = _rope(q, cos, sa, sb)
        k = _rope(k, cos, sa, sb)
    q = (q * Q_SCALE).astype(BF16)
    gz = _silu(z).astype(BF16)
    v_t = v.T.astype(BF16)
    low = lax.broadcasted_iota(jnp.int32, (1, LANES), 1) < HEAD_DIM
    group_w = B_WIDTH // N_KV_HEADS
    for g in range(N_KV_HEADS):
        q_ref[0, g] = q[:, g * group_w:(g + 1) * group_w]
        gz_ref[0, g] = gz[:, g * group_w:(g + 1) * group_w]
        vt_ref[0, g] = v_t[g * HEAD_DIM:(g + 1) * HEAD_DIM, :]
    for j in range(KV_WIDTH // LANES):
        blk = k[:, j * LANES:(j + 1) * LANES]
        swapped = pltpu.roll(blk, HEAD_DIM, 1)
        kk_ref[0, 2 * j] = jnp.where(low, blk, swapped).astype(BF16)
        kk_ref[0, 2 * j + 1] = jnp.where(low, swapped, blk).astype(BF16)


def _project_b(x, mod, j, w_in, tables, *, tm, rope):
    bsz, length, _ = x.shape
    cos, sa, sb = tables
    group_w = B_WIDTH // N_KV_HEADS
    grouped = lambda width: jax.ShapeDtypeStruct((bsz, N_KV_HEADS, length, width), BF16)
    grouped_spec = lambda width: pl.BlockSpec((1, N_KV_HEADS, tm, width),
                                              lambda b, t: (b, 0, t, 0))
    tab_spec = pl.BlockSpec((tm, LANES), lambda b, t: (t, 0))
    kernel = functools.partial(_project_b_kernel, rope=rope)
    return pl.pallas_call(
        kernel,
        out_shape=(grouped(group_w), grouped(LANES),
                   jax.ShapeDtypeStruct((bsz, N_KV_HEADS, HEAD_DIM, length), BF16),
                   grouped(group_w)),
        grid=(bsz, length // tm),
        in_specs=[
            pl.BlockSpec((1, tm, D_MODEL), lambda b, t: (b, t, 0)),
            pl.BlockSpec((1, 3, D_MODEL), lambda b, t: (b, 0, 0)),
            _layer_spec((D_MODEL, 2 * B_WIDTH + 2 * KV_WIDTH), j),
            tab_spec, tab_spec, tab_spec,
        ],
        out_specs=(grouped_spec(group_w), grouped_spec(LANES),
                   pl.BlockSpec((1, N_KV_HEADS, HEAD_DIM, tm), lambda b, t: (b, 0, 0, t)),
                   grouped_spec(group_w)),
        scratch_shapes=[pltpu.VMEM((D_MODEL // 2, 2 * B_WIDTH + 2 * KV_WIDTH), jnp.uint32)],
        compiler_params=pltpu.CompilerParams(
            dimension_semantics=("arbitrary", "arbitrary"), vmem_limit_bytes=VMEM_LIMIT),
        name="project_b_rope" if rope else "project_b",
    )(x, mod, w_in, cos, sa, sb)


def _attend_b_kernel(*refs, local, n_steps, n_sub):
    if local:
        (sink_ref, q_ref, *tile_refs, kx_ref, vtx_ref, gz_ref, x_ref, mod_ref, w_out_ref,
         g_ref, b_ref, mask_ref, o_ref, sx_ref, m_ref, ot_ref, og_ref, w_out_s, sl_ref) = refs
        k_refs, vt_refs = tile_refs[:n_sub + 2], tile_refs[n_sub + 2:]
    else:
        (sink_ref, q_ref, kx_ref, vtx_ref, gz_ref, x_ref, mod_ref, w_out_ref, g_ref, b_ref,
         o_ref, sx_ref, m_ref, ot_ref, og_ref, w_out_s) = refs

    _stage_weight_once(w_out_ref, w_out_s)
    tq = ATT_BLOCK
    low = lax.broadcasted_iota(jnp.int32, (1, LANES), 1) < HEAD_DIM
    left = lax.broadcasted_iota(jnp.int32, (1, 2 * tq), 1) < tq
    if local:
        i = pl.program_id(1)
        first = jnp.where(i == 0, NEG_INF, 0.0).astype(F32)
        last = jnp.where(i == n_steps - 1, NEG_INF, 0.0).astype(F32)
        bias_prev = [mask_ref[0:WINDOW, :] + (first if sub == 0 else 0.0)
                     for sub in range(n_sub)]
        bias_next = [mask_ref[WINDOW:2 * WINDOW, :] + (last if sub == n_sub - 1 else 0.0)
                     for sub in range(n_sub)]

    def sink_row(g, half):
        return jnp.where(left, sink_ref[4 * g + half], sink_ref[4 * g + 2 + half]) * LOG2E

    units = [(sub, half) for sub in range(n_sub) for half in range(2)]

    def scores(g, sub, half):
        rows = slice(sub * tq, (sub + 1) * tq)
        qp = jnp.concatenate([q_ref[0, g, rows, 0:LANES], q_ref[0, g, rows, LANES:2 * LANES]],
                             axis=0)
        qh = jnp.where(low, qp, jnp.zeros_like(qp)) if half == 0 else \
            jnp.where(low, jnp.zeros_like(qp), qp)
        sx = _dot_nt(kx_ref[0, g], qh)
        sx_ref[g, sub, half] = sx
        m = jnp.max(sx, axis=0, keepdims=True)
        if local:
            kl = jnp.concatenate([k_refs[sub + t][0, g] for t in range(3)], axis=0)
            sl = _dot_nt(kl, qh)
            sl = jnp.concatenate(
                [sl[0:WINDOW] + bias_prev[sub], sl[WINDOW:2 * WINDOW],
                 sl[2 * WINDOW:3 * WINDOW] + bias_next[sub]], axis=0)
            sl_ref[g, sub, half] = sl
            m = jnp.maximum(m, jnp.max(sl, axis=0, keepdims=True))
        m_ref[g, sub, half] = jnp.maximum(m, sink_row(g, half))

    def with_ones(vt):
        return jnp.concatenate([vt, jnp.ones((ONES_ROWS, vt.shape[1]), BF16)], axis=0)

    def attend(g, sub, half):
        m = m_ref[g, sub, half]
        px = jnp.exp2(sx_ref[g, sub, half] - m)
        o = _dot(with_ones(vtx_ref[0, g]), px.astype(BF16))
        if local:
            vt = jnp.concatenate([vt_refs[sub + t][0, g] for t in range(3)], axis=1)
            pl_ = jnp.exp2(sl_ref[g, sub, half] - m)
            o = o + _dot(with_ones(vt), pl_.astype(BF16))
        denom = o[HEAD_DIM:HEAD_DIM + 1] + jnp.exp2(sink_row(g, half) - m)
        ot_ref[g, sub, half] = o[0:HEAD_DIM] * (1.0 / denom)

    for sub, half in units:
        scores(0, sub, half)

    def body(g, carry):
        for sub, half in units:
            scores(g + 1, sub, half)
            attend(g, sub, half)
        return carry

    lax.fori_loop(0, N_KV_HEADS - 1, body, 0)

    def project_out(g):
        for sub in range(n_sub):
            rows = slice(sub * tq, (sub + 1) * tq)
            o_t = jnp.concatenate([ot_ref[g, sub, 0], ot_ref[g, sub, 1]], axis=0)
            for r in range(2):
                oc = slice(r * LANES, (r + 1) * LANES)
                o_blk = o_t[:, r * tq:(r + 1) * tq].T
                og_ref[g, _half(rows), oc] = _as_words(
                    (o_blk * gz_ref[0, g, rows, oc].astype(F32)).astype(BF16))
        group_rows = slice(g * og_ref.shape[2], (g + 1) * og_ref.shape[2])
        return _dot(_as_bf16(og_ref[g]), _as_bf16(w_out_s[_half(group_rows), :]))

    last = N_KV_HEADS - 1
    out = None
    for g in range(last):
        part = project_out(g)
        out = part if out is None else out + part
        for sub, half in units[g * len(units) // last:(g + 1) * len(units) // last]:
            attend(last, sub, half)
    out = out + project_out(last)
    o_ref[0] = _post_norm(x_ref[0], out, mod_ref, g_ref, b_ref)


def _attend_b(q, kk, vt, kkx, vtx, gz, x, mod, j, w_out, sink, ln_g, ln_b, mask, *, local):
    bsz, _, length, group_w = q.shape
    n_ctx = kkx.shape[2]
    tq = ATT_BLOCK
    n_tiles = length // tq
    n_sub = min(ATT_SUB, n_tiles)
    rows = n_sub * tq
    n_steps = length // rows
    grouped_spec = pl.BlockSpec((1, N_KV_HEADS, rows, group_w), lambda b, i: (b, 0, i, 0))
    x_spec = pl.BlockSpec((1, rows, D_MODEL), lambda b, i: (b, i, 0))
    tile = lambda t: (lambda i: jnp.clip(n_sub * i + t - 1, 0, n_tiles - 1))
    k_specs = [pl.BlockSpec((1, N_KV_HEADS, tq, LANES), lambda b, i, f=tile(t): (b, 0, f(i), 0))
               for t in range(n_sub + 2)]
    vt_specs = [pl.BlockSpec((1, N_KV_HEADS, HEAD_DIM, tq),
                             lambda b, i, f=tile(t): (b, 0, 0, f(i))) for t in range(n_sub + 2)]
    in_specs = [pl.BlockSpec(memory_space=pltpu.SMEM), grouped_spec]
    args = [sink, q]
    if local:
        in_specs += k_specs + vt_specs
        args += [kk] * (n_sub + 2) + [vt] * (n_sub + 2)
    in_specs += [pl.BlockSpec((1, N_KV_HEADS, n_ctx, LANES), lambda b, i: (b, 0, 0, 0)),
                 pl.BlockSpec((1, N_KV_HEADS, HEAD_DIM, n_ctx), lambda b, i: (b, 0, 0, 0))]
    args += [kkx, vtx]
    in_specs += [grouped_spec, x_spec, pl.BlockSpec((1, 3, D_MODEL), lambda b, i: (b, 0, 0)),
                 _layer_spec((B_WIDTH, D_MODEL), j), _const_spec((1, D_MODEL)),
                 _const_spec((1, D_MODEL))]
    args += [gz, x, mod, w_out, ln_g, ln_b]
    scratch = [pltpu.VMEM((N_KV_HEADS, n_sub, 2, n_ctx, 2 * tq), F32),
               pltpu.VMEM((N_KV_HEADS, n_sub, 2, 1, 2 * tq), F32),
               pltpu.VMEM((N_KV_HEADS, n_sub, 2, HEAD_DIM, 2 * tq), F32),
               pltpu.VMEM((N_KV_HEADS, rows // 2, group_w), jnp.uint32),
               pltpu.VMEM((B_WIDTH // 2, D_MODEL), jnp.uint32)]
    if local:
        in_specs.append(_const_spec((2 * WINDOW, 2 * ATT_BLOCK)))
        args.append(mask)
        scratch.append(pltpu.VMEM((N_KV_HEADS, n_sub, 2, 3 * WINDOW, 2 * tq), F32))
    kernel = functools.partial(_attend_b_kernel, local=local, n_steps=n_steps, n_sub=n_sub)
    return pl.pallas_call(
        kernel,
        out_shape=jax.ShapeDtypeStruct(x.shape, F32),
        grid=(bsz, n_steps),
        in_specs=in_specs,
        out_specs=x_spec,
        scratch_shapes=scratch,
        compiler_params=pltpu.CompilerParams(
            dimension_semantics=("arbitrary", "arbitrary"), vmem_limit_bytes=VMEM_LIMIT),
        name="attend_b_local" if local else "attend_b_ctx",
    )(*args)


def _rope_tables(length):
    rows = length // GRID_W
    row = jnp.repeat(jnp.arange(rows), GRID_W).astype(F32)
    col = jnp.tile(jnp.arange(GRID_W), rows).astype(F32)
    n_freq = ROPE_AXIS_DIM // 2
    inv = ROPE_BASE ** (-jnp.arange(n_freq, dtype=F32) / n_freq)
    ang_r, ang_c = row[:, None] * inv[None, :], col[:, None] * inv[None, :]
    cr, sr, cc, sc = jnp.cos(ang_r), jnp.sin(ang_r), jnp.cos(ang_c), jnp.sin(ang_c)
    zero = jnp.zeros_like(sr)
    reps = LANES // HEAD_DIM
    cos = jnp.tile(jnp.concatenate([cr, cr, cc, cc], axis=1), (1, reps))
    sin_a = jnp.tile(jnp.concatenate([-sr, zero, -sc, zero], axis=1), (1, reps))
    sin_b = jnp.tile(jnp.concatenate([zero, sr, zero, sc], axis=1), (1, reps))
    return cos, sin_a, sin_b


def _band_mask():
    kj = jnp.arange(WINDOW)[:, None]
    qi = jnp.arange(ATT_BLOCK)[None, :]
    prev = jnp.where(kj >= qi, 0.0, NEG_INF)
    nxt = jnp.where(kj <= qi, 0.0, NEG_INF)
    return jnp.tile(jnp.concatenate([prev, nxt], axis=0), (1, 2)).astype(F32)


def kernel(x, c, ctx, c_ctx, ada_w, ada_b, ln_g, ln_b, a_w_in, a_ln_g, a_ln_b, a_w_s, a_b_s,
           a_w_out, b_w_in, b_sink, b_w_out):
    bsz, length, _ = x.shape
    n_ctx = ctx.shape[1]
    assert bsz + 1 <= MOD_ROWS

    c_rows = jnp.zeros((MOD_ROWS, D_MODEL), F32).at[:bsz].set(c).at[bsz].set(c_ctx)
    mod_all = _modulation(c_rows, ada_w, ada_b)
    tables = _rope_tables(length)
    mask = _band_mask()
    a_w_in_h, a_w_s_h, a_w_out_h = (w.astype(BF16) for w in (a_w_in, a_w_s, a_w_out))
    b_w_in_h, b_w_out_h = b_w_in, b_w_out

    for i in range(DEPTH):
        j = i // N_MIXERS
        need_ctx_out = i < DEPTH - 1
        mod = mod_all[i, :bsz].reshape(bsz, 3, D_MODEL)
        mod_c = jnp.broadcast_to(mod_all[i, bsz].reshape(1, 3, D_MODEL), (bsz, 3, D_MODEL))
        g_i, b_i = ln_g[i].reshape(1, D_MODEL), ln_b[i].reshape(1, D_MODEL)
        if i % N_MIXERS == 0:
            b_s_full = jnp.repeat(a_b_s[j].T, A_GW, axis=1)
            weights = (j, a_w_in_h, a_ln_g[j].reshape(1, A_WIDTH), a_ln_b[j].reshape(1, A_WIDTH),
                       a_w_s_h, b_s_full, a_w_out_h, g_i, b_i)
            x_new = _layer_a(x, mod, *weights, tm=A_TILE_ROWS)
            if need_ctx_out:
                ctx = _layer_a(ctx, mod_c, *weights, tm=n_ctx)
            x = x_new
        else:
            sink = b_sink[j].astype(F32)
            q, kk, vt, gz = _project_b(x, mod, j, b_w_in_h, tables, tm=B_TILE_ROWS, rope=True)
            qc, kkc, vtc, gzc = _project_b(ctx, mod_c, j, b_w_in_h, tables, tm=n_ctx, rope=False)
            x = _attend_b(q, kk, vt, kkc, vtc, gz, x, mod, j, b_w_out_h, sink, g_i, b_i, mask,
                          local=True)
            if need_ctx_out:
                ctx = _attend_b(qc, None, None, kkc, vtc, gzc, ctx, mod_c, j, b_w_out_h, sink,
                                g_i, b_i, None, local=False)
    return x
```

```python
import functools
import math

import jax
import jax.numpy as jnp
from jax import lax
from jax.experimental import pallas as pl
from jax.experimental.pallas import tpu as pltpu

D_MODEL = 1024
DEPTH = 4
GRID_W = 64
N_MIXERS = 2
CHUNK = 128
A_WIDTH = 2 * D_MODEL
A_GROUPS = 8
A_GW = A_WIDTH // A_GROUPS
HEAD_DIM = 64
N_HEADS = D_MODEL // HEAD_DIM
N_KV_HEADS = N_HEADS // 4
B_WIDTH = N_HEADS * HEAD_DIM
KV_WIDTH = N_KV_HEADS * HEAD_DIM
WINDOW = 128
ATT_BLOCK = 128
ROPE_AXIS_DIM = HEAD_DIM // 2
ROPE_BASE = 10000.0
LN_EPS = 1e-5
NEG_INF = -1e30
ALPHA = (2.0 * DEPTH) ** 0.25

LANES = 128
MOD_ROWS = 8
VMEM_LIMIT = 48 * 1024 * 1024

BF16 = jnp.bfloat16
F32 = jnp.float32

LOG2E = math.log2(math.e)
Q_SCALE = HEAD_DIM ** -0.5 * LOG2E
ONES_ROWS = 16
A_TILE_ROWS = 512
A_SUB_ROWS = 256
B_TILE_ROWS = 1024
ATT_SUB = 4
COPY_ROWS = 128

_GELU_C0 = math.sqrt(2.0 / math.pi)
_GELU_C1 = _GELU_C0 * 0.044715


def _dot(a, b):
    return jnp.dot(a, b, preferred_element_type=F32)


def _dot_nt(a, b):
    return lax.dot_general(a, b, (((1,), (1,)), ((), ())), preferred_element_type=F32)


def _gelu(x):
    inner = x * (_GELU_C1 * (x * x) + _GELU_C0)
    hx = 0.5 * x
    return hx + hx * jnp.tanh(inner)


def _silu(z):
    hz = 0.5 * z
    return hz + hz * jnp.tanh(hz)


def _layer_norm(x, g, b):
    mu = jnp.mean(x, axis=-1, keepdims=True)
    xc = x - mu
    var = jnp.mean(xc * xc, axis=-1, keepdims=True)
    return xc * lax.rsqrt(var + LN_EPS) * g + b


def _const_spec(shape):
    nd = len(shape)
    return pl.BlockSpec(shape, lambda *_: (0,) * nd, pipeline_mode=pl.Buffered(1))


def _as_words(t):
    return pltpu.bitcast(t, jnp.uint32)


def _as_bf16(t):
    return pltpu.bitcast(t, BF16)


def _half(rows):
    return slice(rows.start // 2, rows.stop // 2)


def _stage_weight_once(w_ref, w_s):
    @pl.when((pl.program_id(0) == 0) & (pl.program_id(1) == 0))
    def _():
        def copy_rows(r, carry):
            rows = pl.ds(pl.multiple_of(r * COPY_ROWS, COPY_ROWS), COPY_ROWS)
            half_rows = pl.ds(pl.multiple_of(r * (COPY_ROWS // 2), COPY_ROWS // 2), COPY_ROWS // 2)
            w_s[half_rows, :] = _as_words(w_ref[0, rows, :])
            return carry

        lax.fori_loop(0, w_ref.shape[1] // COPY_ROWS, copy_rows, 0)


def _layer_spec(shape, j):
    nd = len(shape)
    return pl.BlockSpec((1,) + tuple(shape), lambda *_: (j,) + (0,) * nd,
                        pipeline_mode=pl.Buffered(1))


def _modulation_kernel(c_ref, w_ref, b_ref, o_ref):
    c = c_ref[...]
    cond = _silu(c).astype(BF16)
    o_ref[0] = _dot(cond, w_ref[0].astype(BF16)) + b_ref[0]


def _modulation(c_rows, ada_w, ada_b):
    tn = D_MODEL
    return pl.pallas_call(
        _modulation_kernel,
        out_shape=jax.ShapeDtypeStruct((DEPTH, MOD_ROWS, 3 * D_MODEL), F32),
        grid=(DEPTH, 3 * D_MODEL // tn),
        in_specs=[
            pl.BlockSpec((MOD_ROWS, D_MODEL), lambda i, j: (0, 0)),
            pl.BlockSpec((1, D_MODEL, tn), lambda i, j: (i, 0, j)),
            pl.BlockSpec((1, 1, tn), lambda i, j: (i, 0, j)),
        ],
        out_specs=pl.BlockSpec((1, MOD_ROWS, tn), lambda i, j: (i, 0, j)),
        compiler_params=pltpu.CompilerParams(
            dimension_semantics=("parallel", "parallel"), vmem_limit_bytes=VMEM_LIMIT),
        name="modulation",
    )(c_rows, ada_w, ada_b.reshape(DEPTH, 1, 3 * D_MODEL))


def _modulate(x, mod_ref):
    shift = mod_ref[0, 0:1, :]
    scale = mod_ref[0, 1:2, :]
    return x * (1.0 + scale) + shift


def _post_norm(x, out, mod_ref, g_ref, b_ref):
    gate = mod_ref[0, 2:3, :]
    return _layer_norm(ALPHA * x + gate * out, g_ref[...], b_ref[...])


def _layer_a_kernel(x_ref, mod_ref, w_in_ref, lng_ref, lnb_ref, ws_ref, bs_ref, w_out_ref,
                    g_ref, b_ref, o_ref, vn_ref, y_ref, w_in_s, w_out_s, *, tm, ts):
    _stage_weight_once(w_in_ref, w_in_s)
    _stage_weight_once(w_out_ref, w_out_s)

    w_in = lambda lo, hi: _as_bf16(w_in_s[:, lo:hi])
    subs = [slice(t * ts, (t + 1) * ts) for t in range(tm // ts)]

    def spatial_norm(sub):
        h = _modulate(x_ref[0, sub, :], mod_ref).astype(BF16)
        v = _gelu(_dot(h, w_in(A_WIDTH, 2 * A_WIDTH)))
        vn_ref[_half(sub), :] = _as_words(
            _layer_norm(v, lng_ref[...], lnb_ref[...]).astype(BF16))
        return h

    def gated_mix(sub, h):
        for g in range(A_GROUPS):
            cols = slice(g * A_GW, (g + 1) * A_GW)
            u = _gelu(_dot(h, w_in(g * A_GW, (g + 1) * A_GW)))
            z = _dot(h, w_in(2 * A_WIDTH + g * A_GW, 2 * A_WIDTH + (g + 1) * A_GW))
            uz = u * _silu(z)
            for c in range(ts // CHUNK):
                rows = _half(slice(sub.start + c * CHUNK, sub.start + (c + 1) * CHUNK))
                s = _dot(ws_ref[0, g], _as_bf16(vn_ref[rows, cols])) + bs_ref[:, cols]
                y_ref[rows, cols] = _as_words((uz[c * CHUNK:(c + 1) * CHUNK] * s).astype(BF16))

    def project_out(sub):
        out = _dot(_as_bf16(y_ref[_half(sub), :]), _as_bf16(w_out_s[...]))
        o_ref[0, sub, :] = _post_norm(x_ref[0, sub, :], out, mod_ref, g_ref, b_ref)

    h = spatial_norm(subs[0])
    for t, sub in enumerate(subs):
        gated_mix(sub, h)
        if t + 1 < len(subs):
            h = spatial_norm(subs[t + 1])
        project_out(sub)


def _layer_a(x, mod, j, w_in, a_ln_g, a_ln_b, w_s, b_s_full, w_out, ln_g, ln_b, *, tm):
    bsz, length, _ = x.shape
    kernel = functools.partial(_layer_a_kernel, tm=tm, ts=min(tm, A_SUB_ROWS))
    return pl.pallas_call(
        kernel,
        out_shape=jax.ShapeDtypeStruct(x.shape, F32),
        grid=(bsz, length // tm),
        in_specs=[
            pl.BlockSpec((1, tm, D_MODEL), lambda b, t: (b, t, 0)),
            pl.BlockSpec((1, 3, D_MODEL), lambda b, t: (b, 0, 0)),
            _layer_spec((D_MODEL, 3 * A_WIDTH), j),
            _const_spec((1, A_WIDTH)),
            _const_spec((1, A_WIDTH)),
            _layer_spec((A_GROUPS, CHUNK, CHUNK), j),
            _const_spec((CHUNK, A_WIDTH)),
            _layer_spec((A_WIDTH, D_MODEL), j),
            _const_spec((1, D_MODEL)),
            _const_spec((1, D_MODEL)),
        ],
        out_specs=pl.BlockSpec((1, tm, D_MODEL), lambda b, t: (b, t, 0)),
        scratch_shapes=[pltpu.VMEM((tm // 2, A_WIDTH), jnp.uint32),
                        pltpu.VMEM((tm // 2, A_WIDTH), jnp.uint32),
                        pltpu.VMEM((D_MODEL // 2, 3 * A_WIDTH), jnp.uint32),
                        pltpu.VMEM((A_WIDTH // 2, D_MODEL), jnp.uint32)],
        compiler_params=pltpu.CompilerParams(
            dimension_semantics=("arbitrary", "arbitrary"), vmem_limit_bytes=VMEM_LIMIT),
        name="layer_a",
    )(x, mod, w_in, a_ln_g, a_ln_b, w_s, b_s_full, w_out, ln_g, ln_b)


def _rope(t, cos, sin_a, sin_b):
    n = t.shape[1] // LANES
    half = ROPE_AXIS_DIM // 2
    outs = []
    for j in range(n):
        blk = t[:, j * LANES:(j + 1) * LANES]
        up = pltpu.roll(blk, LANES - half, 1)
        dn = pltpu.roll(blk, half, 1)
        outs.append(blk * cos + up * sin_a + dn * sin_b)
    return jnp.concatenate(outs, axis=1)


def _project_b_kernel(x_ref, mod_ref, w_ref, cos_ref, sa_ref, sb_ref,
                      q_ref, kk_ref, vt_ref, gz_ref, w_s, *, rope):
    _stage_weight_once(w_ref, w_s)
    x = x_ref[0]
    h = _modulate(x, mod_ref).astype(BF16)
    w = lambda lo, hi: _as_bf16(w_s[:, lo:hi])
    q = _dot(h, w(0, B_WIDTH))
    k = _dot(h, w(B_WIDTH, B_WIDTH + KV_WIDTH))
    v = _dot(h, w(B_WIDTH + KV_WIDTH, B_WIDTH + 2 * KV_WIDTH))
    z = _dot(h, w(B_WIDTH + 2 * KV_WIDTH, 2 * B_WIDTH + 2 * KV_WIDTH))
    if rope:
        cos, sa, sb = cos_ref[...], sa_ref[...], sb_ref[...]
        q = _rope(q, cos, sa, sb)
        k = _rope(k, cos, sa, sb)
    q = (q * Q_SCALE).astype(BF16)
    gz = _silu(z).astype(BF16)
    v_t = v.T.astype(BF16)
    low = lax.broadcasted_iota(jnp.int32, (1, LANES), 1) < HEAD_DIM
    group_w = B_WIDTH // N_KV_HEADS
    for g in range(N_KV_HEADS):
        q_ref[0, g] = q[:, g * group_w:(g + 1) * group_w]
        gz_ref[0, g] = gz[:, g * group_w:(g + 1) * group_w]
        vt_ref[0, g] = v_t[g * HEAD_DIM:(g + 1) * HEAD_DIM, :]
    for j in range(KV_WIDTH // LANES):
        blk = k[:, j * LANES:(j + 1) * LANES]
        swapped = pltpu.roll(blk, HEAD_DIM, 1)
        kk_ref[0, 2 * j] = jnp.where(low, blk, swapped).astype(BF16)
        kk_ref[0, 2 * j + 1] = jnp.where(low, swapped, blk).astype(BF16)


def _project_b(x, mod, j, w_in, tables, *, tm, rope):
    bsz, length, _ = x.shape
    cos, sa, sb = tables
    group_w = B_WIDTH // N_KV_HEADS
    grouped = lambda width: jax.ShapeDtypeStruct((bsz, N_KV_HEADS, length, width), BF16)
    grouped_spec = lambda width: pl.BlockSpec((1, N_KV_HEADS, tm, width),
                                              lambda b, t: (b, 0, t, 0))
    tab_spec = pl.BlockSpec((tm, LANES), lambda b, t: (t, 0))
    kernel = functools.partial(_project_b_kernel, rope=rope)
    return pl.pallas_call(
        kernel,
        out_shape=(grouped(group_w), grouped(LANES),
                   jax.ShapeDtypeStruct((bsz, N_KV_HEADS, HEAD_DIM, length), BF16),
                   grouped(group_w)),
        grid=(bsz, length // tm),
        in_specs=[
            pl.BlockSpec((1, tm, D_MODEL), lambda b, t: (b, t, 0)),
            pl.BlockSpec((1, 3, D_MODEL), lambda b, t: (b, 0, 0)),
            _layer_spec((D_MODEL, 2 * B_WIDTH + 2 * KV_WIDTH), j),
            tab_spec, tab_spec, tab_spec,
        ],
        out_specs=(grouped_spec(group_w), grouped_spec(LANES),
                   pl.BlockSpec((1, N_KV_HEADS, HEAD_DIM, tm), lambda b, t: (b, 0, 0, t)),
                   grouped_spec(group_w)),
        scratch_shapes=[pltpu.VMEM((D_MODEL // 2, 2 * B_WIDTH + 2 * KV_WIDTH), jnp.uint32)],
        compiler_params=pltpu.CompilerParams(
            dimension_semantics=("arbitrary", "arbitrary"), vmem_limit_bytes=VMEM_LIMIT),
        name="project_b_rope" if rope else "project_b",
    )(x, mod, w_in, cos, sa, sb)


def _attend_b_kernel(*refs, local, n_steps, n_sub):
    if local:
        (sink_ref, q_ref, *tile_refs, kx_ref, vtx_ref, gz_ref, x_ref, mod_ref, w_out_ref,
         g_ref, b_ref, mask_ref, o_ref, sx_ref, m_ref, ot_ref, og_ref, sl_ref) = refs
        k_refs, vt_refs = tile_refs[:n_sub + 2], tile_refs[n_sub + 2:]
    else:
        (sink_ref, q_ref, kx_ref, vtx_ref, gz_ref, x_ref, mod_ref, w_out_ref, g_ref, b_ref,
         o_ref, sx_ref, m_ref, ot_ref, og_ref) = refs

    tq = ATT_BLOCK
    low = lax.broadcasted_iota(jnp.int32, (1, LANES), 1) < HEAD_DIM
    left = lax.broadcasted_iota(jnp.int32, (1, 2 * tq), 1) < tq
    if local:
        i = pl.program_id(1)
        first = jnp.where(i == 0, NEG_INF, 0.0).astype(F32)
        last = jnp.where(i == n_steps - 1, NEG_INF, 0.0).astype(F32)
        bias_prev = [mask_ref[0:WINDOW, :] + (first if sub == 0 else 0.0)
                     for sub in range(n_sub)]
        bias_next = [mask_ref[WINDOW:2 * WINDOW, :] + (last if sub == n_sub - 1 else 0.0)
                     for sub in range(n_sub)]

    def sink_row(g, half):
        return jnp.where(left, sink_ref[4 * g + half], sink_ref[4 * g + 2 + half]) * LOG2E

    units = [(sub, half) for sub in range(n_sub) for half in range(2)]

    def slot(g):
        return g % 2 if isinstance(g, int) else jnp.bitwise_and(g, 1)

    def scores(g, sub, half):
        rows = slice(sub * tq, (sub + 1) * tq)
        qp = jnp.concatenate([q_ref[0, g, rows, 0:LANES], q_ref[0, g, rows, LANES:2 * LANES]],
                             axis=0)
        qh = jnp.where(low, qp, jnp.zeros_like(qp)) if half == 0 else \
            jnp.where(low, jnp.zeros_like(qp), qp)
        sx = _dot_nt(kx_ref[0, g], qh)
        sx_ref[slot(g), sub, half] = sx
        m = jnp.max(sx, axis=0, keepdims=True)
        if local:
            kl = jnp.concatenate([k_refs[sub + t][0, g] for t in range(3)], axis=0)
            sl = _dot_nt(kl, qh)
            sl = jnp.concatenate(
                [sl[0:WINDOW] + bias_prev[sub], sl[WINDOW:2 * WINDOW],
                 sl[2 * WINDOW:3 * WINDOW] + bias_next[sub]], axis=0)
            sl_ref[slot(g), sub, half] = sl
            m = jnp.maximum(m, jnp.max(sl, axis=0, keepdims=True))
        m_ref[slot(g), sub, half] = jnp.maximum(m, sink_row(g, half))

    def with_ones(vt):
        return jnp.concatenate([vt, jnp.ones((ONES_ROWS, vt.shape[1]), BF16)], axis=0)

    def attend(g, sub, half):
        m = m_ref[slot(g), sub, half]
        px = jnp.exp2(sx_ref[slot(g), sub, half] - m)
        o = _dot(with_ones(vtx_ref[0, g]), px.astype(BF16))
        if local:
            vt = jnp.concatenate([vt_refs[sub + t][0, g] for t in range(3)], axis=1)
            pl_ = jnp.exp2(sl_ref[slot(g), sub, half] - m)
            o = o + _dot(with_ones(vt), pl_.astype(BF16))
        denom = o[HEAD_DIM:HEAD_DIM + 1] + jnp.exp2(sink_row(g, half) - m)
        ot_ref[g, sub, half] = o[0:HEAD_DIM] * (1.0 / denom)

    for sub, half in units:
        scores(0, sub, half)

    def body(g, carry):
        for sub, half in units:
            scores(g + 1, sub, half)
            attend(g, sub, half)
        return carry

    lax.fori_loop(0, N_KV_HEADS - 1, body, 0)

    def project_out(g):
        for sub in range(n_sub):
            rows = slice(sub * tq, (sub + 1) * tq)
            o_t = jnp.concatenate([ot_ref[g, sub, 0], ot_ref[g, sub, 1]], axis=0)
            for r in range(2):
                oc = slice(r * LANES, (r + 1) * LANES)
                o_blk = o_t[:, r * tq:(r + 1) * tq].T
                og_ref[g, _half(rows), oc] = _as_words(
                    (o_blk * gz_ref[0, g, rows, oc].astype(F32)).astype(BF16))
        return _dot(_as_bf16(og_ref[g]), w_out_ref[0, g])

    last = N_KV_HEADS - 1
    out = None
    for g in range(last):
        part = project_out(g)
        out = part if out is None else out + part
        for sub, half in units[g * len(units) // last:(g + 1) * len(units) // last]:
            attend(last, sub, half)
    out = out + project_out(last)
    o_ref[0] = _post_norm(x_ref[0], out, mod_ref, g_ref, b_ref)


def _attend_b(q, kk, vt, kkx, vtx, gz, x, mod, j, w_out, sink, ln_g, ln_b, mask, *, local):
    bsz, _, length, group_w = q.shape
    n_ctx = kkx.shape[2]
    tq = ATT_BLOCK
    n_tiles = length // tq
    n_sub = min(ATT_SUB, n_tiles)
    rows = n_sub * tq
    n_steps = length // rows
    grouped_spec = pl.BlockSpec((1, N_KV_HEADS, rows, group_w), lambda b, i: (b, 0, i, 0))
    x_spec = pl.BlockSpec((1, rows, D_MODEL), lambda b, i: (b, i, 0))
    tile = lambda t: (lambda i: jnp.clip(n_sub * i + t - 1, 0, n_tiles - 1))
    k_specs = [pl.BlockSpec((1, N_KV_HEADS, tq, LANES), lambda b, i, f=tile(t): (b, 0, f(i), 0))
               for t in range(n_sub + 2)]
    vt_specs = [pl.BlockSpec((1, N_KV_HEADS, HEAD_DIM, tq),
                             lambda b, i, f=tile(t): (b, 0, 0, f(i))) for t in range(n_sub + 2)]
    in_specs = [pl.BlockSpec(memory_space=pltpu.SMEM), grouped_spec]
    args = [sink, q]
    if local:
        in_specs += k_specs + vt_specs
        args += [kk] * (n_sub + 2) + [vt] * (n_sub + 2)
    in_specs += [pl.BlockSpec((1, N_KV_HEADS, n_ctx, LANES), lambda b, i: (b, 0, 0, 0)),
                 pl.BlockSpec((1, N_KV_HEADS, HEAD_DIM, n_ctx), lambda b, i: (b, 0, 0, 0))]
    args += [kkx, vtx]
    in_specs += [grouped_spec, x_spec, pl.BlockSpec((1, 3, D_MODEL), lambda b, i: (b, 0, 0)),
                 _layer_spec((N_KV_HEADS, group_w, D_MODEL), j), _const_spec((1, D_MODEL)),
                 _const_spec((1, D_MODEL))]
    args += [gz, x, mod, w_out, ln_g, ln_b]
    scratch = [pltpu.VMEM((2, n_sub, 2, n_ctx, 2 * tq), F32),
               pltpu.VMEM((2, n_sub, 2, 1, 2 * tq), F32),
               pltpu.VMEM((N_KV_HEADS, n_sub, 2, HEAD_DIM, 2 * tq), F32),
               pltpu.VMEM((N_KV_HEADS, rows // 2, group_w), jnp.uint32)]
    if local:
        in_specs.append(_const_spec((2 * WINDOW, 2 * ATT_BLOCK)))
        args.append(mask)
        scratch.append(pltpu.VMEM((2, n_sub, 2, 3 * WINDOW, 2 * tq), F32))
    kernel = functools.partial(_attend_b_kernel, local=local, n_steps=n_steps, n_sub=n_sub)
    return pl.pallas_call(
        kernel,
        out_shape=jax.ShapeDtypeStruct(x.shape, F32),
        grid=(bsz, n_steps),
        in_specs=in_specs,
        out_specs=x_spec,
        scratch_shapes=scratch,
        compiler_params=pltpu.CompilerParams(
            dimension_semantics=("parallel", "parallel"), vmem_limit_bytes=VMEM_LIMIT),
        name="attend_b_local" if local else "attend_b_ctx",
    )(*args)


def _rope_tables(length):
    rows = length // GRID_W
    row = jnp.repeat(jnp.arange(rows), GRID_W).astype(F32)
    col = jnp.tile(jnp.arange(GRID_W), rows).astype(F32)
    n_freq = ROPE_AXIS_DIM // 2
    inv = ROPE_BASE ** (-jnp.arange(n_freq, dtype=F32) / n_freq)
    ang_r, ang_c = row[:, None] * inv[None, :], col[:, None] * inv[None, :]
    cr, sr, cc, sc = jnp.cos(ang_r), jnp.sin(ang_r), jnp.cos(ang_c), jnp.sin(ang_c)
    zero = jnp.zeros_like(sr)
    reps = LANES // HEAD_DIM
    cos = jnp.tile(jnp.concatenate([cr, cr, cc, cc], axis=1), (1, reps))
    sin_a = jnp.tile(jnp.concatenate([-sr, zero, -sc, zero], axis=1), (1, reps))
    sin_b = jnp.tile(jnp.concatenate([zero, sr, zero, sc], axis=1), (1, reps))
    return cos, sin_a, sin_b


def _band_mask():
    kj = jnp.arange(WINDOW)[:, None]
    qi = jnp.arange(ATT_BLOCK)[None, :]
    prev = jnp.where(kj >= qi, 0.0, NEG_INF)
    nxt = jnp.where(kj <= qi, 0.0, NEG_INF)
    return jnp.tile(jnp.concatenate([prev, nxt], axis=0), (1, 2)).astype(F32)


def kernel(x, c, ctx, c_ctx, ada_w, ada_b, ln_g, ln_b, a_w_in, a_ln_g, a_ln_b, a_w_s, a_b_s,
           a_w_out, b_w_in, b_sink, b_w_out):
    bsz, length, _ = x.shape
    n_ctx = ctx.shape[1]
    assert bsz + 1 <= MOD_ROWS

    c_rows = jnp.zeros((MOD_ROWS, D_MODEL), F32).at[:bsz].set(c).at[bsz].set(c_ctx)
    mod_all = _modulation(c_rows, ada_w, ada_b)
    tables = _rope_tables(length)
    mask = _band_mask()
    a_w_in_h, a_w_s_h, a_w_out_h = (w.astype(BF16) for w in (a_w_in, a_w_s, a_w_out))
    b_w_in_h = b_w_in.astype(BF16)
    b_w_out_h = b_w_out.astype(BF16).reshape(-1, N_KV_HEADS, B_WIDTH // N_KV_HEADS, D_MODEL)

    for i in range(DEPTH):
        j = i // N_MIXERS
        need_ctx_out = i < DEPTH - 1
        mod = mod_all[i, :bsz].reshape(bsz, 3, D_MODEL)
        mod_c = jnp.broadcast_to(mod_all[i, bsz].reshape(1, 3, D_MODEL), (bsz, 3, D_MODEL))
        g_i, b_i = ln_g[i].reshape(1, D_MODEL), ln_b[i].reshape(1, D_MODEL)
        if i % N_MIXERS == 0:
            b_s_full = jnp.repeat(a_b_s[j].T, A_GW, axis=1)
            weights = (j, a_w_in_h, a_ln_g[j].reshape(1, A_WIDTH), a_ln_b[j].reshape(1, A_WIDTH),
                       a_w_s_h, b_s_full, a_w_out_h, g_i, b_i)
            x_new = _layer_a(x, mod, *weights, tm=A_TILE_ROWS)
            if need_ctx_out:
                ctx = _layer_a(ctx, mod_c, *weights, tm=n_ctx)
            x = x_new
        else:
            sink = b_sink[j].astype(F32)
            q, kk, vt, gz = _project_b(x, mod, j, b_w_in_h, tables, tm=B_TILE_ROWS, rope=True)
            qc, kkc, vtc, gzc = _project_b(ctx, mod_c, j, b_w_in_h, tables, tm=n_ctx, rope=False)
            x = _attend_b(q, kk, vt, kkc, vtc, gz, x, mod, j, b_w_out_h, sink, g_i, b_i, mask,
                          local=True)
            if need_ctx_out:
                ctx = _attend_b(qc, None, None, kkc, vtc, gzc, ctx, mod_c, j, b_w_out_h, sink,
                                g_i, b_i, None, local=False)
    return x
```

```python
import functools
import math

import jax
import jax.numpy as jnp
from jax import lax
from jax.experimental import pallas as pl
from jax.experimental.pallas import tpu as pltpu

D_MODEL = 1024
DEPTH = 4
GRID_W = 64
N_MIXERS = 2
CHUNK = 128
A_WIDTH = 2 * D_MODEL
A_GROUPS = 8
A_GW = A_WIDTH // A_GROUPS
HEAD_DIM = 64
N_HEADS = D_MODEL // HEAD_DIM
N_KV_HEADS = N_HEADS // 4
B_WIDTH = N_HEADS * HEAD_DIM
KV_WIDTH = N_KV_HEADS * HEAD_DIM
WINDOW = 128
ATT_BLOCK = 128
ROPE_AXIS_DIM = HEAD_DIM // 2
ROPE_BASE = 10000.0
LN_EPS = 1e-5
NEG_INF = -1e30
ALPHA = (2.0 * DEPTH) ** 0.25

LANES = 128
MOD_ROWS = 8
MOD_COL_BLOCKS = 4
VMEM_LIMIT = 48 * 1024 * 1024

BF16 = jnp.bfloat16
F32 = jnp.float32

LOG2E = math.log2(math.e)
Q_SCALE = HEAD_DIM ** -0.5 * LOG2E
ONES_ROWS = 16
A_TILE_ROWS = 512
A_SUB_ROWS = 256
B_TILE_ROWS = 1024
ATT_SUB = 4
COPY_ROWS = 128

_GELU_C0 = math.sqrt(2.0 / math.pi)
_GELU_C1 = _GELU_C0 * 0.044715


def _dot(a, b):
    return jnp.dot(a, b, preferred_element_type=F32)


def _dot_nt(a, b):
    return lax.dot_general(a, b, (((1,), (1,)), ((), ())), preferred_element_type=F32)


def _gelu(x):
    inner = x * (_GELU_C1 * (x * x) + _GELU_C0)
    hx = 0.5 * x
    return hx + hx * jnp.tanh(inner)


def _silu(z):
    hz = 0.5 * z
    return hz + hz * jnp.tanh(hz)


def _layer_norm(x, g, b):
    mu = jnp.mean(x, axis=-1, keepdims=True)
    xc = x - mu
    var = jnp.mean(xc * xc, axis=-1, keepdims=True)
    return xc * lax.rsqrt(var + LN_EPS) * g + b


def _const_spec(shape):
    nd = len(shape)
    return pl.BlockSpec(shape, lambda *_: (0,) * nd, pipeline_mode=pl.Buffered(1))


def _as_words(t):
    return pltpu.bitcast(t, jnp.uint32)


def _as_bf16(t):
    return pltpu.bitcast(t, BF16)


def _half(rows):
    return slice(rows.start // 2, rows.stop // 2)


def _stage_weight_once(w_ref, w_s):
    @pl.when((pl.program_id(0) == 0) & (pl.program_id(1) == 0))
    def _():
        def copy_rows(r, carry):
            rows = pl.ds(pl.multiple_of(r * COPY_ROWS, COPY_ROWS), COPY_ROWS)
            half_rows = pl.ds(pl.multiple_of(r * (COPY_ROWS // 2), COPY_ROWS // 2), COPY_ROWS // 2)
            w_s[half_rows, :] = _as_words(w_ref[0, rows, :])
            return carry

        lax.fori_loop(0, w_ref.shape[1] // COPY_ROWS, copy_rows, 0)


def _cast_rows(src_refs, dst_refs):
    for src, dst in zip(src_refs, dst_refs):
        dst[0] = src[0].astype(BF16)


def _cast_specs(weights, layer, n_steps, step):
    in_specs, out_specs, out_shapes = [], [], []
    for w in weights:
        _, k, n = w.shape
        rows = k // n_steps
        in_specs.append(pl.BlockSpec((1, rows, n), lambda b, t: (layer, step(b, t), 0)))
        out_specs.append(pl.BlockSpec((1, rows, n), lambda b, t: (0, step(b, t), 0)))
        out_shapes.append(jax.ShapeDtypeStruct((1, k, n), BF16))
    return in_specs, out_specs, out_shapes


def _layer_spec(shape, j):
    nd = len(shape)
    return pl.BlockSpec((1,) + tuple(shape), lambda *_: (j,) + (0,) * nd,
                        pipeline_mode=pl.Buffered(1))


def _modulation_kernel(*refs, n_cast):
    c_ref, w_ref, b_ref = refs[:3]
    cast_in, o_ref, cast_out = refs[3:3 + n_cast], refs[3 + n_cast], refs[4 + n_cast:]
    _cast_rows(cast_in, cast_out)
    c = c_ref[...]
    cond = _silu(c).astype(BF16)
    o_ref[0] = _dot(cond, w_ref[0].astype(BF16)) + b_ref[0]


def _modulation(c_rows, ada_w, ada_b, cast=((), 0)):
    n_col = MOD_COL_BLOCKS
    tn = 3 * D_MODEL // n_col
    cast_w, cast_layer = cast
    cast_in, cast_out, cast_shapes = _cast_specs(cast_w, cast_layer, DEPTH * n_col,
                                                 lambda i, j: i * n_col + j)
    outs = pl.pallas_call(
        functools.partial(_modulation_kernel, n_cast=len(cast_w)),
        out_shape=[jax.ShapeDtypeStruct((DEPTH, MOD_ROWS, 3 * D_MODEL), F32)] + cast_shapes,
        grid=(DEPTH, n_col),
        in_specs=[
            pl.BlockSpec((MOD_ROWS, D_MODEL), lambda i, j: (0, 0)),
            pl.BlockSpec((1, D_MODEL, tn), lambda i, j: (i, 0, j)),
            pl.BlockSpec((1, 1, tn), lambda i, j: (i, 0, j)),
        ] + cast_in,
        out_specs=[pl.BlockSpec((1, MOD_ROWS, tn), lambda i, j: (i, 0, j))] + cast_out,
        compiler_params=pltpu.CompilerParams(
            dimension_semantics=("parallel", "parallel"), vmem_limit_bytes=VMEM_LIMIT),
        name="modulation",
    )(c_rows, ada_w, ada_b.reshape(DEPTH, 1, 3 * D_MODEL), *cast_w)
    return outs[0], outs[1:]


def _modulate(x, mod_ref):
    shift = mod_ref[0, 0:1, :]
    scale = mod_ref[0, 1:2, :]
    return x * (1.0 + scale) + shift


def _post_norm(x, out, mod_ref, g_ref, b_ref):
    gate = mod_ref[0, 2:3, :]
    return _layer_norm(ALPHA * x + gate * out, g_ref[...], b_ref[...])


def _layer_a_kernel(*refs, tm, ts, n_cast):
    (x_ref, mod_ref, w_in_ref, lng_ref, lnb_ref, ws_ref, bs_ref, w_out_ref,
     g_ref, b_ref) = refs[:10]
    cast_in, o_ref = refs[10:10 + n_cast], refs[10 + n_cast]
    cast_out = refs[11 + n_cast:11 + 2 * n_cast]
    vn_ref, y_ref, w_in_s, w_out_s = refs[11 + 2 * n_cast:]
    _cast_rows(cast_in, cast_out)
    _stage_weight_once(w_in_ref, w_in_s)
    _stage_weight_once(w_out_ref, w_out_s)

    w_in = lambda lo, hi: _as_bf16(w_in_s[:, lo:hi])
    subs = [slice(t * ts, (t + 1) * ts) for t in range(tm // ts)]

    def spatial_norm(sub):
        h = _modulate(x_ref[0, sub, :], mod_ref).astype(BF16)
        v = _gelu(_dot(h, w_in(A_WIDTH, 2 * A_WIDTH)))
        vn_ref[_half(sub), :] = _as_words(
            _layer_norm(v, lng_ref[...], lnb_ref[...]).astype(BF16))
        return h

    def gated_mix(sub, h):
        for g in range(A_GROUPS):
            cols = slice(g * A_GW, (g + 1) * A_GW)
            u = _gelu(_dot(h, w_in(g * A_GW, (g + 1) * A_GW)))
            z = _dot(h, w_in(2 * A_WIDTH + g * A_GW, 2 * A_WIDTH + (g + 1) * A_GW))
            uz = u * _silu(z)
            for c in range(ts // CHUNK):
                rows = _half(slice(sub.start + c * CHUNK, sub.start + (c + 1) * CHUNK))
                s = _dot(ws_ref[0, g], _as_bf16(vn_ref[rows, cols])) + bs_ref[:, cols]
                y_ref[rows, cols] = _as_words((uz[c * CHUNK:(c + 1) * CHUNK] * s).astype(BF16))

    def project_out(sub):
        out = _dot(_as_bf16(y_ref[_half(sub), :]), _as_bf16(w_out_s[...]))
        o_ref[0, sub, :] = _post_norm(x_ref[0, sub, :], out, mod_ref, g_ref, b_ref)

    h = spatial_norm(subs[0])
    for t, sub in enumerate(subs):
        gated_mix(sub, h)
        if t + 1 < len(subs):
            h = spatial_norm(subs[t + 1])
        project_out(sub)


def _layer_a(x, mod, j, w_in, a_ln_g, a_ln_b, w_s, b_s_full, w_out, ln_g, ln_b, *, tm,
             cast=((), 0)):
    bsz, length, _ = x.shape
    n_t = length // tm
    cast_w, cast_layer = cast
    cast_in, cast_out, cast_shapes = _cast_specs(cast_w, cast_layer, bsz * n_t,
                                                 lambda b, t: b * n_t + t)
    kernel = functools.partial(_layer_a_kernel, tm=tm, ts=min(tm, A_SUB_ROWS),
                               n_cast=len(cast_w))
    outs = pl.pallas_call(
        kernel,
        out_shape=[jax.ShapeDtypeStruct(x.shape, F32)] + cast_shapes,
        grid=(bsz, n_t),
        in_specs=[
            pl.BlockSpec((1, tm, D_MODEL), lambda b, t: (b, t, 0)),
            pl.BlockSpec((1, 3, D_MODEL), lambda b, t: (b, 0, 0)),
            _layer_spec((D_MODEL, 3 * A_WIDTH), j),
            _const_spec((1, A_WIDTH)),
            _const_spec((1, A_WIDTH)),
            _layer_spec((A_GROUPS, CHUNK, CHUNK), j),
            _const_spec((CHUNK, A_WIDTH)),
            _layer_spec((A_WIDTH, D_MODEL), j),
            _const_spec((1, D_MODEL)),
            _const_spec((1, D_MODEL)),
        ] + cast_in,
        out_specs=[pl.BlockSpec((1, tm, D_MODEL), lambda b, t: (b, t, 0))] + cast_out,
        scratch_shapes=[pltpu.VMEM((tm // 2, A_WIDTH), jnp.uint32),
                        pltpu.VMEM((tm // 2, A_WIDTH), jnp.uint32),
                        pltpu.VMEM((D_MODEL // 2, 3 * A_WIDTH), jnp.uint32),
                        pltpu.VMEM((A_WIDTH // 2, D_MODEL), jnp.uint32)],
        compiler_params=pltpu.CompilerParams(
            dimension_semantics=("arbitrary", "arbitrary"), vmem_limit_bytes=VMEM_LIMIT),
        name="layer_a",
    )(x, mod, w_in, a_ln_g, a_ln_b, w_s, b_s_full, w_out, ln_g, ln_b, *cast_w)
    return outs[0], outs[1:]


def _rope(t, cos, sin_a, sin_b):
    n = t.shape[1] // LANES
    half = ROPE_AXIS_DIM // 2
    outs = []
    for j in range(n):
        blk = t[:, j * LANES:(j + 1) * LANES]
        up = pltpu.roll(blk, LANES - half, 1)
        dn = pltpu.roll(blk, half, 1)
        outs.append(blk * cos + up * sin_a + dn * sin_b)
    return jnp.concatenate(outs, axis=1)


def _project_b_kernel(x_ref, mod_ref, w_ref, cos_ref, sa_ref, sb_ref,
                      q_ref, kk_ref, vt_ref, gz_ref, w_s, *, rope):
    _stage_weight_once(w_ref, w_s)
    x = x_ref[0]
    h = _modulate(x, mod_ref).astype(BF16)
    w = lambda lo, hi: _as_bf16(w_s[:, lo:hi])
    q = _dot(h, w(0, B_WIDTH))
    k = _dot(h, w(B_WIDTH, B_WIDTH + KV_WIDTH))
    v = _dot(h, w(B_WIDTH + KV_WIDTH, B_WIDTH + 2 * KV_WIDTH))
    z = _dot(h, w(B_WIDTH + 2 * KV_WIDTH, 2 * B_WIDTH + 2 * KV_WIDTH))
    if rope:
        cos, sa, sb = cos_ref[...], sa_ref[...], sb_ref[...]
        q = _rope(q, cos, sa, sb)
        k = _rope(k, cos, sa, sb)
    q = (q * Q_SCALE).astype(BF16)
    gz = _silu(z).astype(BF16)
    v_t = v.T.astype(BF16)
    low = lax.broadcasted_iota(jnp.int32, (1, LANES), 1) < HEAD_DIM
    group_w = B_WIDTH // N_KV_HEADS
    for g in range(N_KV_HEADS):
        q_ref[0, g] = q[:, g * group_w:(g + 1) * group_w]
        gz_ref[0, g] = gz[:, g * group_w:(g + 1) * group_w]
        vt_ref[0, g] = v_t[g * HEAD_DIM:(g + 1) * HEAD_DIM, :]
    for j in range(KV_WIDTH // LANES):
        blk = k[:, j * LANES:(j + 1) * LANES]
        swapped = pltpu.roll(blk, HEAD_DIM, 1)
        kk_ref[0, 2 * j] = jnp.where(low, blk, swapped).astype(BF16)
        kk_ref[0, 2 * j + 1] = jnp.where(low, swapped, blk).astype(BF16)


def _project_b(x, mod, j, w_in, tables, *, tm, rope):
    bsz, length, _ = x.shape
    cos, sa, sb = tables
    group_w = B_WIDTH // N_KV_HEADS
    grouped = lambda width: jax.ShapeDtypeStruct((bsz, N_KV_HEADS, length, width), BF16)
    grouped_spec = lambda width: pl.BlockSpec((1, N_KV_HEADS, tm, width),
                                              lambda b, t: (b, 0, t, 0))
    tab_spec = pl.BlockSpec((tm, LANES), lambda b, t: (t, 0))
    kernel = functools.partial(_project_b_kernel, rope=rope)
    return pl.pallas_call(
        kernel,
        out_shape=(grouped(group_w), grouped(LANES),
                   jax.ShapeDtypeStruct((bsz, N_KV_HEADS, HEAD_DIM, length), BF16),
                   grouped(group_w)),
        grid=(bsz, length // tm),
        in_specs=[
            pl.BlockSpec((1, tm, D_MODEL), lambda b, t: (b, t, 0)),
            pl.BlockSpec((1, 3, D_MODEL), lambda b, t: (b, 0, 0)),
            _layer_spec((D_MODEL, 2 * B_WIDTH + 2 * KV_WIDTH), j),
            tab_spec, tab_spec, tab_spec,
        ],
        out_specs=(grouped_spec(group_w), grouped_spec(LANES),
                   pl.BlockSpec((1, N_KV_HEADS, HEAD_DIM, tm), lambda b, t: (b, 0, 0, t)),
                   grouped_spec(group_w)),
        scratch_shapes=[pltpu.VMEM((D_MODEL // 2, 2 * B_WIDTH + 2 * KV_WIDTH), jnp.uint32)],
        compiler_params=pltpu.CompilerParams(
            dimension_semantics=("arbitrary", "arbitrary"), vmem_limit_bytes=VMEM_LIMIT),
        name="project_b_rope" if rope else "project_b",
    )(x, mod, w_in, cos, sa, sb)


def _attend_b_kernel(*refs, local, n_steps, n_sub, n_cast):
    n_in = 10 + (2 * (n_sub + 2) + 1 if local else 0)
    cast_in, cast_out = refs[n_in:n_in + n_cast], refs[n_in + n_cast + 1:n_in + 2 * n_cast + 1]
    refs = refs[:n_in] + (refs[n_in + n_cast],) + refs[n_in + 2 * n_cast + 1:]
    _cast_rows(cast_in, cast_out)
    if local:
        (sink_ref, q_ref, *tile_refs, kx_ref, vtx_ref, gz_ref, x_ref, mod_ref, w_out_ref,
         g_ref, b_ref, mask_ref, o_ref, sx_ref, m_ref, ot_ref, og_ref, sl_ref) = refs
        k_refs, vt_refs = tile_refs[:n_sub + 2], tile_refs[n_sub + 2:]
    else:
        (sink_ref, q_ref, kx_ref, vtx_ref, gz_ref, x_ref, mod_ref, w_out_ref, g_ref, b_ref,
         o_ref, sx_ref, m_ref, ot_ref, og_ref) = refs

    tq = ATT_BLOCK
    low = lax.broadcasted_iota(jnp.int32, (1, LANES), 1) < HEAD_DIM
    left = lax.broadcasted_iota(jnp.int32, (1, 2 * tq), 1) < tq
    if local:
        i = pl.program_id(1)
        first = jnp.where(i == 0, NEG_INF, 0.0).astype(F32)
        last = jnp.where(i == n_steps - 1, NEG_INF, 0.0).astype(F32)
        bias_prev = [mask_ref[0:WINDOW, :] + (first if sub == 0 else 0.0)
                     for sub in range(n_sub)]
        bias_next = [mask_ref[WINDOW:2 * WINDOW, :] + (last if sub == n_sub - 1 else 0.0)
                     for sub in range(n_sub)]

    def sink_row(g, half):
        return jnp.where(left, sink_ref[4 * g + half], sink_ref[4 * g + 2 + half]) * LOG2E

    units = [(sub, half) for sub in range(n_sub) for half in range(2)]

    def scores(g, sub, half):
        rows = slice(sub * tq, (sub + 1) * tq)
        qp = jnp.concatenate([q_ref[0, g, rows, 0:LANES], q_ref[0, g, rows, LANES:2 * LANES]],
                             axis=0)
        qh = jnp.where(low, qp, jnp.zeros_like(qp)) if half == 0 else \
            jnp.where(low, jnp.zeros_like(qp), qp)
        sx = _dot_nt(kx_ref[0, g], qh)
        sx_ref[g, sub, half] = sx
        m = jnp.max(sx, axis=0, keepdims=True)
        if local:
            kl = jnp.concatenate([k_refs[sub + t][0, g] for t in range(3)], axis=0)
            sl = _dot_nt(kl, qh)
            sl = jnp.concatenate(
                [sl[0:WINDOW] + bias_prev[sub], sl[WINDOW:2 * WINDOW],
                 sl[2 * WINDOW:3 * WINDOW] + bias_next[sub]], axis=0)
            sl_ref[g, sub, half] = sl
            m = jnp.maximum(m, jnp.max(sl, axis=0, keepdims=True))
        m_ref[g, sub, half] = jnp.maximum(m, sink_row(g, half))

    def with_ones(vt):
        return jnp.concatenate([vt, jnp.ones((ONES_ROWS, vt.shape[1]), BF16)], axis=0)

    def attend(g, sub, half):
        m = m_ref[g, sub, half]
        px = jnp.exp2(sx_ref[g, sub, half] - m)
        o = _dot(with_ones(vtx_ref[0, g]), px.astype(BF16))
        if local:
            vt = jnp.concatenate([vt_refs[sub + t][0, g] for t in range(3)], axis=1)
            pl_ = jnp.exp2(sl_ref[g, sub, half] - m)
            o = o + _dot(with_ones(vt), pl_.astype(BF16))
        denom = o[HEAD_DIM:HEAD_DIM + 1] + jnp.exp2(sink_row(g, half) - m)
        ot_ref[g, sub, half] = o[0:HEAD_DIM] * (1.0 / denom)

    for sub, half in units:
        scores(0, sub, half)

    def body(g, carry):
        for sub, half in units:
            scores(g + 1, sub, half)
            attend(g, sub, half)
        return carry

    lax.fori_loop(0, N_KV_HEADS - 1, body, 0)

    def project_out(g):
        for sub in range(n_sub):
            rows = slice(sub * tq, (sub + 1) * tq)
            o_t = jnp.concatenate([ot_ref[g, sub, 0], ot_ref[g, sub, 1]], axis=0)
            for r in range(2):
                oc = slice(r * LANES, (r + 1) * LANES)
                o_blk = o_t[:, r * tq:(r + 1) * tq].T
                og_ref[g, _half(rows), oc] = _as_words(
                    (o_blk * gz_ref[0, g, rows, oc].astype(F32)).astype(BF16))
        return _dot(_as_bf16(og_ref[g]), w_out_ref[0, g])

    last = N_KV_HEADS - 1
    out = None
    for g in range(last):
        part = project_out(g)
        out = part if out is None else out + part
        for sub, half in units[g * len(units) // last:(g + 1) * len(units) // last]:
            attend(last, sub, half)
    out = out + project_out(last)
    o_ref[0] = _post_norm(x_ref[0], out, mod_ref, g_ref, b_ref)


def _attend_b(q, kk, vt, kkx, vtx, gz, x, mod, j, w_out, sink, ln_g, ln_b, mask, *, local,
              cast=((), 0)):
    bsz, _, length, group_w = q.shape
    n_ctx = kkx.shape[2]
    tq = ATT_BLOCK
    n_tiles = length // tq
    n_sub = min(ATT_SUB, n_tiles)
    rows = n_sub * tq
    n_steps = length // rows
    grouped_spec = pl.BlockSpec((1, N_KV_HEADS, rows, group_w), lambda b, i: (b, 0, i, 0))
    x_spec = pl.BlockSpec((1, rows, D_MODEL), lambda b, i: (b, i, 0))
    tile = lambda t: (lambda i: jnp.clip(n_sub * i + t - 1, 0, n_tiles - 1))
    k_specs = [pl.BlockSpec((1, N_KV_HEADS, tq, LANES), lambda b, i, f=tile(t): (b, 0, f(i), 0))
               for t in range(n_sub + 2)]
    vt_specs = [pl.BlockSpec((1, N_KV_HEADS, HEAD_DIM, tq),
                             lambda b, i, f=tile(t): (b, 0, 0, f(i))) for t in range(n_sub + 2)]
    in_specs = [pl.BlockSpec(memory_space=pltpu.SMEM), grouped_spec]
    args = [sink, q]
    if local:
        in_specs += k_specs + vt_specs
        args += [kk] * (n_sub + 2) + [vt] * (n_sub + 2)
    in_specs += [pl.BlockSpec((1, N_KV_HEADS, n_ctx, LANES), lambda b, i: (b, 0, 0, 0)),
                 pl.BlockSpec((1, N_KV_HEADS, HEAD_DIM, n_ctx), lambda b, i: (b, 0, 0, 0))]
    args += [kkx, vtx]
    in_specs += [grouped_spec, x_spec, pl.BlockSpec((1, 3, D_MODEL), lambda b, i: (b, 0, 0)),
                 _layer_spec((N_KV_HEADS, group_w, D_MODEL), j), _const_spec((1, D_MODEL)),
                 _const_spec((1, D_MODEL))]
    args += [gz, x, mod, w_out, ln_g, ln_b]
    scratch = [pltpu.VMEM((N_KV_HEADS, n_sub, 2, n_ctx, 2 * tq), F32),
               pltpu.VMEM((N_KV_HEADS, n_sub, 2, 1, 2 * tq), F32),
               pltpu.VMEM((N_KV_HEADS, n_sub, 2, HEAD_DIM, 2 * tq), F32),
               pltpu.VMEM((N_KV_HEADS, rows // 2, group_w), jnp.uint32)]
    if local:
        in_specs.append(_const_spec((2 * WINDOW, 2 * ATT_BLOCK)))
        args.append(mask)
        scratch.append(pltpu.VMEM((N_KV_HEADS, n_sub, 2, 3 * WINDOW, 2 * tq), F32))
    cast_w, cast_layer = cast
    cast_in, cast_out, cast_shapes = _cast_specs(cast_w, cast_layer, bsz * n_steps,
                                                 lambda b, i: b * n_steps + i)
    kernel = functools.partial(_attend_b_kernel, local=local, n_steps=n_steps, n_sub=n_sub,
                               n_cast=len(cast_w))
    outs = pl.pallas_call(
        kernel,
        out_shape=[jax.ShapeDtypeStruct(x.shape, F32)] + cast_shapes,
        grid=(bsz, n_steps),
        in_specs=in_specs + cast_in,
        out_specs=[x_spec] + cast_out,
        scratch_shapes=scratch,
        compiler_params=pltpu.CompilerParams(
            dimension_semantics=("parallel", "parallel"), vmem_limit_bytes=VMEM_LIMIT),
        name="attend_b_local" if local else "attend_b_ctx",
    )(*args, *cast_w)
    return outs[0], outs[1:]


def _rope_tables(length):
    rows = length // GRID_W
    row = jnp.repeat(jnp.arange(rows), GRID_W).astype(F32)
    col = jnp.tile(jnp.arange(GRID_W), rows).astype(F32)
    n_freq = ROPE_AXIS_DIM // 2
    inv = ROPE_BASE ** (-jnp.arange(n_freq, dtype=F32) / n_freq)
    ang_r, ang_c = row[:, None] * inv[None, :], col[:, None] * inv[None, :]
    cr, sr, cc, sc = jnp.cos(ang_r), jnp.sin(ang_r), jnp.cos(ang_c), jnp.sin(ang_c)
    zero = jnp.zeros_like(sr)
    reps = LANES // HEAD_DIM
    cos = jnp.tile(jnp.concatenate([cr, cr, cc, cc], axis=1), (1, reps))
    sin_a = jnp.tile(jnp.concatenate([-sr, zero, -sc, zero], axis=1), (1, reps))
    sin_b = jnp.tile(jnp.concatenate([zero, sr, zero, sc], axis=1), (1, reps))
    return cos, sin_a, sin_b


def _band_mask():
    kj = jnp.arange(WINDOW)[:, None]
    qi = jnp.arange(ATT_BLOCK)[None, :]
    prev = jnp.where(kj >= qi, 0.0, NEG_INF)
    nxt = jnp.where(kj <= qi, 0.0, NEG_INF)
    return jnp.tile(jnp.concatenate([prev, nxt], axis=0), (1, 2)).astype(F32)


def kernel(x, c, ctx, c_ctx, ada_w, ada_b, ln_g, ln_b, a_w_in, a_ln_g, a_ln_b, a_w_s, a_b_s,
           a_w_out, b_w_in, b_sink, b_w_out):
    bsz, length, _ = x.shape
    n_ctx = ctx.shape[1]
    assert bsz + 1 <= MOD_ROWS

    c_rows = jnp.zeros((MOD_ROWS, D_MODEL), F32).at[:bsz].set(c).at[bsz].set(c_ctx)
    mod_all, (a_in_h, a_out_h) = _modulation(c_rows, ada_w, ada_b, cast=((a_w_in, a_w_out), 0))
    tables = _rope_tables(length)
    mask = _band_mask()
    assert N_MIXERS == 2
    a_w_s_h = a_w_s.astype(BF16)
    b_in_h = b_out_h = None

    for i in range(DEPTH):
        j = i // N_MIXERS
        need_ctx_out = i < DEPTH - 1
        mod = mod_all[i, :bsz].reshape(bsz, 3, D_MODEL)
        mod_c = jnp.broadcast_to(mod_all[i, bsz].reshape(1, 3, D_MODEL), (bsz, 3, D_MODEL))
        g_i, b_i = ln_g[i].reshape(1, D_MODEL), ln_b[i].reshape(1, D_MODEL)
        if i % N_MIXERS == 0:
            b_s_full = jnp.repeat(a_b_s[j].T, A_GW, axis=1)
            weights = (0, a_in_h, a_ln_g[j].reshape(1, A_WIDTH), a_ln_b[j].reshape(1, A_WIDTH),
                       a_w_s_h[j:j + 1], b_s_full, a_out_h, g_i, b_i)
            cast = ((b_w_in, b_w_out), j) if need_ctx_out else ((), 0)
            x_new, cast_out = _layer_a(x, mod, *weights, tm=A_TILE_ROWS, cast=cast)
            if cast_out:
                b_in_h, b_out_h = cast_out
            if need_ctx_out:
                ctx, _ = _layer_a(ctx, mod_c, *weights, tm=n_ctx)
            x = x_new
        else:
            sink = b_sink[j].astype(F32)
            w_out = b_out_h.reshape(1, N_KV_HEADS, B_WIDTH // N_KV_HEADS, D_MODEL)
            cast = ((a_w_in, a_w_out), j + 1) if need_ctx_out else ((), 0)
            q, kk, vt, gz = _project_b(x, mod, 0, b_in_h, tables, tm=B_TILE_ROWS, rope=True)
            qc, kkc, vtc, gzc = _project_b(ctx, mod_c, 0, b_in_h, tables, tm=n_ctx, rope=False)
            x, cast_out = _attend_b(q, kk, vt, kkc, vtc, gz, x, mod, 0, w_out, sink, g_i, b_i,
                                    mask, local=True, cast=cast)
            if cast_out:
                a_in_h, a_out_h = cast_out
            if need_ctx_out:
                ctx, _ = _attend_b(qc, None, None, kkc, vtc, gzc, ctx, mod_c, 0, w_out, sink,
                                   g_i, b_i, None, local=False)
    return x
```

```python
import functools
import math

import jax
import jax.numpy as jnp
from jax import lax
from jax.experimental import pallas as pl
from jax.experimental.pallas import tpu as pltpu

D_MODEL = 1024
DEPTH = 4
GRID_W = 64
N_MIXERS = 2
CHUNK = 128
A_WIDTH = 2 * D_MODEL
A_GROUPS = 8
A_GW = A_WIDTH // A_GROUPS
HEAD_DIM = 64
N_HEADS = D_MODEL // HEAD_DIM
N_KV_HEADS = N_HEADS // 4
B_WIDTH = N_HEADS * HEAD_DIM
KV_WIDTH = N_KV_HEADS * HEAD_DIM
WINDOW = 128
ATT_BLOCK = 128
ROPE_AXIS_DIM = HEAD_DIM // 2
ROPE_BASE = 10000.0
LN_EPS = 1e-5
NEG_INF = -1e30
ALPHA = (2.0 * DEPTH) ** 0.25

LANES = 128
MOD_ROWS = 8
MOD_COL_BLOCKS = 4
VMEM_LIMIT = 48 * 1024 * 1024

BF16 = jnp.bfloat16
F32 = jnp.float32

LOG2E = math.log2(math.e)
Q_SCALE = HEAD_DIM ** -0.5 * LOG2E
ONES_ROWS = 16
A_TILE_ROWS = 512
A_SUB_ROWS = 256
B_TILE_ROWS = 1024
ATT_SUB = 4
COPY_ROWS = 128

_GELU_C0 = math.sqrt(2.0 / math.pi)
_GELU_C1 = _GELU_C0 * 0.044715


def _dot(a, b):
    return jnp.dot(a, b, preferred_element_type=F32)


def _dot_nt(a, b):
    return lax.dot_general(a, b, (((1,), (1,)), ((), ())), preferred_element_type=F32)


def _gelu(x):
    inner = x * (_GELU_C1 * (x * x) + _GELU_C0)
    hx = 0.5 * x
    return hx + hx * jnp.tanh(inner)


def _silu(z):
    hz = 0.5 * z
    return hz + hz * jnp.tanh(hz)


def _layer_norm(x, g, b):
    mu = jnp.mean(x, axis=-1, keepdims=True)
    xc = x - mu
    var = jnp.mean(xc * xc, axis=-1, keepdims=True)
    return xc * lax.rsqrt(var + LN_EPS) * g + b


def _const_spec(shape):
    nd = len(shape)
    return pl.BlockSpec(shape, lambda *_: (0,) * nd, pipeline_mode=pl.Buffered(1))


def _as_words(t):
    return pltpu.bitcast(t, jnp.uint32)


def _as_bf16(t):
    return pltpu.bitcast(t, BF16)


def _half(rows):
    return slice(rows.start // 2, rows.stop // 2)


def _stage_weight_once(w_ref, w_s):
    @pl.when((pl.program_id(0) == 0) & (pl.program_id(1) == 0))
    def _():
        def copy_rows(r, carry):
            rows = pl.ds(pl.multiple_of(r * COPY_ROWS, COPY_ROWS), COPY_ROWS)
            half_rows = pl.ds(pl.multiple_of(r * (COPY_ROWS // 2), COPY_ROWS // 2), COPY_ROWS // 2)
            w_s[half_rows, :] = _as_words(w_ref[0, rows, :])
            return carry

        lax.fori_loop(0, w_ref.shape[1] // COPY_ROWS, copy_rows, 0)


def _cast_rows(src_refs, dst_refs):
    for src, dst in zip(src_refs, dst_refs):
        dst[0] = src[0].astype(BF16)


def _cast_specs(weights, layer, n_steps, step):
    in_specs, out_specs, out_shapes = [], [], []
    for w in weights:
        _, k, n = w.shape
        rows = k // n_steps
        in_specs.append(pl.BlockSpec((1, rows, n), lambda b, t: (layer, step(b, t), 0)))
        out_specs.append(pl.BlockSpec((1, rows, n), lambda b, t: (0, step(b, t), 0)))
        out_shapes.append(jax.ShapeDtypeStruct((1, k, n), BF16))
    return in_specs, out_specs, out_shapes


def _layer_spec(shape, j):
    nd = len(shape)
    return pl.BlockSpec((1,) + tuple(shape), lambda *_: (j,) + (0,) * nd,
                        pipeline_mode=pl.Buffered(1))


def _modulation_kernel(*refs, n_cast):
    c_ref, w_ref, b_ref = refs[:3]
    cast_in, o_ref, cast_out = refs[3:3 + n_cast], refs[3 + n_cast], refs[4 + n_cast:]
    _cast_rows(cast_in, cast_out)
    c = c_ref[...]
    cond = _silu(c).astype(BF16)
    o_ref[0] = _dot(cond, w_ref[0].astype(BF16)) + b_ref[0]


def _modulation(c_rows, ada_w, ada_b, cast=((), 0)):
    n_col = MOD_COL_BLOCKS
    tn = 3 * D_MODEL // n_col
    cast_w, cast_layer = cast
    cast_in, cast_out, cast_shapes = _cast_specs(cast_w, cast_layer, DEPTH * n_col,
                                                 lambda i, j: i * n_col + j)
    outs = pl.pallas_call(
        functools.partial(_modulation_kernel, n_cast=len(cast_w)),
        out_shape=[jax.ShapeDtypeStruct((DEPTH, MOD_ROWS, 3 * D_MODEL), F32)] + cast_shapes,
        grid=(DEPTH, n_col),
        in_specs=[
            pl.BlockSpec((MOD_ROWS, D_MODEL), lambda i, j: (0, 0)),
            pl.BlockSpec((1, D_MODEL, tn), lambda i, j: (i, 0, j)),
            pl.BlockSpec((1, 1, tn), lambda i, j: (i, 0, j)),
        ] + cast_in,
        out_specs=[pl.BlockSpec((1, MOD_ROWS, tn), lambda i, j: (i, 0, j))] + cast_out,
        compiler_params=pltpu.CompilerParams(
            dimension_semantics=("parallel", "parallel"), vmem_limit_bytes=VMEM_LIMIT),
        name="modulation",
    )(c_rows, ada_w, ada_b.reshape(DEPTH, 1, 3 * D_MODEL), *cast_w)
    return outs[0], outs[1:]


def _mod_spec(mod):
    _, layer, fixed = mod
    row = (lambda b: b) if fixed is None else (lambda b: fixed)
    return pl.BlockSpec((1, 1, 3, D_MODEL), lambda b, t: (layer, row(b), 0, 0))


def _modulate(x, mod_ref):
    shift = mod_ref[0, 0, 0:1, :]
    scale = mod_ref[0, 0, 1:2, :]
    return x * (1.0 + scale) + shift


def _post_norm(x, out, mod_ref, g_ref, b_ref):
    gate = mod_ref[0, 0, 2:3, :]
    return _layer_norm(ALPHA * x + gate * out, g_ref[...], b_ref[...])


def _layer_a_kernel(*refs, tm, ts, n_cast):
    (x_ref, mod_ref, w_in_ref, lng_ref, lnb_ref, ws_ref, bs_ref, w_out_ref,
     g_ref, b_ref) = refs[:10]
    cast_in, o_ref = refs[10:10 + n_cast], refs[10 + n_cast]
    cast_out = refs[11 + n_cast:11 + 2 * n_cast]
    vn_ref, y_ref, w_in_s, w_out_s = refs[11 + 2 * n_cast:]
    _cast_rows(cast_in, cast_out)
    _stage_weight_once(w_in_ref, w_in_s)
    _stage_weight_once(w_out_ref, w_out_s)

    w_in = lambda lo, hi: _as_bf16(w_in_s[:, lo:hi])
    subs = [slice(t * ts, (t + 1) * ts) for t in range(tm // ts)]

    def spatial_norm(sub):
        h = _modulate(x_ref[0, sub, :], mod_ref).astype(BF16)
        v = _gelu(_dot(h, w_in(A_WIDTH, 2 * A_WIDTH)))
        vn_ref[_half(sub), :] = _as_words(
            _layer_norm(v, lng_ref[...], lnb_ref[...]).astype(BF16))
        return h

    def gated_mix(sub, h):
        for g in range(A_GROUPS):
            cols = slice(g * A_GW, (g + 1) * A_GW)
            u = _gelu(_dot(h, w_in(g * A_GW, (g + 1) * A_GW)))
            z = _dot(h, w_in(2 * A_WIDTH + g * A_GW, 2 * A_WIDTH + (g + 1) * A_GW))
            uz = u * _silu(z)
            for c in range(ts // CHUNK):
                rows = _half(slice(sub.start + c * CHUNK, sub.start + (c + 1) * CHUNK))
                s = _dot(ws_ref[0, g], _as_bf16(vn_ref[rows, cols])) + bs_ref[:, cols]
                y_ref[rows, cols] = _as_words((uz[c * CHUNK:(c + 1) * CHUNK] * s).astype(BF16))

    def project_out(sub):
        out = _dot(_as_bf16(y_ref[_half(sub), :]), _as_bf16(w_out_s[...]))
        o_ref[0, sub, :] = _post_norm(x_ref[0, sub, :], out, mod_ref, g_ref, b_ref)

    h = spatial_norm(subs[0])
    for t, sub in enumerate(subs):
        gated_mix(sub, h)
        if t + 1 < len(subs):
            h = spatial_norm(subs[t + 1])
        project_out(sub)


def _layer_a(x, mod, j, w_in, a_ln_g, a_ln_b, w_s, b_s_full, w_out, ln_g, ln_b, *, tm,
             cast=((), 0)):
    bsz, length, _ = x.shape
    n_t = length // tm
    cast_w, cast_layer = cast
    cast_in, cast_out, cast_shapes = _cast_specs(cast_w, cast_layer, bsz * n_t,
                                                 lambda b, t: b * n_t + t)
    kernel = functools.partial(_layer_a_kernel, tm=tm, ts=min(tm, A_SUB_ROWS),
                               n_cast=len(cast_w))
    outs = pl.pallas_call(
        kernel,
        out_shape=[jax.ShapeDtypeStruct(x.shape, F32)] + cast_shapes,
        grid=(bsz, n_t),
        in_specs=[
            pl.BlockSpec((1, tm, D_MODEL), lambda b, t: (b, t, 0)),
            _mod_spec(mod),
            _layer_spec((D_MODEL, 3 * A_WIDTH), j),
            _const_spec((1, A_WIDTH)),
            _const_spec((1, A_WIDTH)),
            _layer_spec((A_GROUPS, CHUNK, CHUNK), j),
            _const_spec((CHUNK, A_WIDTH)),
            _layer_spec((A_WIDTH, D_MODEL), j),
            _const_spec((1, D_MODEL)),
            _const_spec((1, D_MODEL)),
        ] + cast_in,
        out_specs=[pl.BlockSpec((1, tm, D_MODEL), lambda b, t: (b, t, 0))] + cast_out,
        scratch_shapes=[pltpu.VMEM((tm // 2, A_WIDTH), jnp.uint32),
                        pltpu.VMEM((tm // 2, A_WIDTH), jnp.uint32),
                        pltpu.VMEM((D_MODEL // 2, 3 * A_WIDTH), jnp.uint32),
                        pltpu.VMEM((A_WIDTH // 2, D_MODEL), jnp.uint32)],
        compiler_params=pltpu.CompilerParams(
            dimension_semantics=("arbitrary", "arbitrary"), vmem_limit_bytes=VMEM_LIMIT),
        name="layer_a",
    )(x, mod[0], w_in, a_ln_g, a_ln_b, w_s, b_s_full, w_out, ln_g, ln_b, *cast_w)
    return outs[0], outs[1:]


def _rope(t, cos, sin_a, sin_b):
    n = t.shape[1] // LANES
    half = ROPE_AXIS_DIM // 2
    outs = []
    for j in range(n):
        blk = t[:, j * LANES:(j + 1) * LANES]
        up = pltpu.roll(blk, LANES - half, 1)
        dn = pltpu.roll(blk, half, 1)
        outs.append(blk * cos + up * sin_a + dn * sin_b)
    return jnp.concatenate(outs, axis=1)


def _project_b_kernel(x_ref, mod_ref, w_ref, cos_ref, sa_ref, sb_ref,
                      q_ref, kk_ref, vt_ref, gz_ref, w_s, *, rope):
    _stage_weight_once(w_ref, w_s)
    x = x_ref[0]
    h = _modulate(x, mod_ref).astype(BF16)
    w = lambda lo, hi: _as_bf16(w_s[:, lo:hi])
    q = _dot(h, w(0, B_WIDTH))
    k = _dot(h, w(B_WIDTH, B_WIDTH + KV_WIDTH))
    v = _dot(h, w(B_WIDTH + KV_WIDTH, B_WIDTH + 2 * KV_WIDTH))
    z = _dot(h, w(B_WIDTH + 2 * KV_WIDTH, 2 * B_WIDTH + 2 * KV_WIDTH))
    if rope:
        cos, sa, sb = cos_ref[...], sa_ref[...], sb_ref[...]
        q = _rope(q, cos, sa, sb)
        k = _rope(k, cos, sa, sb)
    q = (q * Q_SCALE).astype(BF16)
    gz = _silu(z).astype(BF16)
    v_t = v.T.astype(BF16)
    low = lax.broadcasted_iota(jnp.int32, (1, LANES), 1) < HEAD_DIM
    group_w = B_WIDTH // N_KV_HEADS
    for g in range(N_KV_HEADS):
        q_ref[0, g] = q[:, g * group_w:(g + 1) * group_w]
        gz_ref[0, g] = gz[:, g * group_w:(g + 1) * group_w]
        vt_ref[0, g] = v_t[g * HEAD_DIM:(g + 1) * HEAD_DIM, :]
    for j in range(KV_WIDTH // LANES):
        blk = k[:, j * LANES:(j + 1) * LANES]
        swapped = pltpu.roll(blk, HEAD_DIM, 1)
        kk_ref[0, 2 * j] = jnp.where(low, blk, swapped).astype(BF16)
        kk_ref[0, 2 * j + 1] = jnp.where(low, swapped, blk).astype(BF16)


def _project_b(x, mod, j, w_in, tables, *, tm, rope):
    bsz, length, _ = x.shape
    cos, sa, sb = tables
    group_w = B_WIDTH // N_KV_HEADS
    grouped = lambda width: jax.ShapeDtypeStruct((bsz, N_KV_HEADS, length, width), BF16)
    grouped_spec = lambda width: pl.BlockSpec((1, N_KV_HEADS, tm, width),
                                              lambda b, t: (b, 0, t, 0))
    tab_spec = pl.BlockSpec((tm, LANES), lambda b, t: (t, 0))
    kernel = functools.partial(_project_b_kernel, rope=rope)
    return pl.pallas_call(
        kernel,
        out_shape=(grouped(group_w), grouped(LANES),
                   jax.ShapeDtypeStruct((bsz, N_KV_HEADS, HEAD_DIM, length), BF16),
                   grouped(group_w)),
        grid=(bsz, length // tm),
        in_specs=[
            pl.BlockSpec((1, tm, D_MODEL), lambda b, t: (b, t, 0)),
            _mod_spec(mod),
            _layer_spec((D_MODEL, 2 * B_WIDTH + 2 * KV_WIDTH), j),
            tab_spec, tab_spec, tab_spec,
        ],
        out_specs=(grouped_spec(group_w), grouped_spec(LANES),
                   pl.BlockSpec((1, N_KV_HEADS, HEAD_DIM, tm), lambda b, t: (b, 0, 0, t)),
                   grouped_spec(group_w)),
        scratch_shapes=[pltpu.VMEM((D_MODEL // 2, 2 * B_WIDTH + 2 * KV_WIDTH), jnp.uint32)],
        compiler_params=pltpu.CompilerParams(
            dimension_semantics=("arbitrary", "arbitrary"), vmem_limit_bytes=VMEM_LIMIT),
        name="project_b_rope" if rope else "project_b",
    )(x, mod[0], w_in, cos, sa, sb)


def _attend_b_kernel(*refs, local, n_steps, n_sub, n_cast):
    n_in = 10 + (2 * (n_sub + 2) + 1 if local else 0)
    cast_in, cast_out = refs[n_in:n_in + n_cast], refs[n_in + n_cast + 1:n_in + 2 * n_cast + 1]
    refs = refs[:n_in] + (refs[n_in + n_cast],) + refs[n_in + 2 * n_cast + 1:]
    _cast_rows(cast_in, cast_out)
    if local:
        (sink_ref, q_ref, *tile_refs, kx_ref, vtx_ref, gz_ref, x_ref, mod_ref, w_out_ref,
         g_ref, b_ref, mask_ref, o_ref, sx_ref, m_ref, ot_ref, og_ref, sl_ref) = refs
        k_refs, vt_refs = tile_refs[:n_sub + 2], tile_refs[n_sub + 2:]
    else:
        (sink_ref, q_ref, kx_ref, vtx_ref, gz_ref, x_ref, mod_ref, w_out_ref, g_ref, b_ref,
         o_ref, sx_ref, m_ref, ot_ref, og_ref) = refs

    tq = ATT_BLOCK
    low = lax.broadcasted_iota(jnp.int32, (1, LANES), 1) < HEAD_DIM
    left = lax.broadcasted_iota(jnp.int32, (1, 2 * tq), 1) < tq
    if local:
        i = pl.program_id(1)
        first = jnp.where(i == 0, NEG_INF, 0.0).astype(F32)
        last = jnp.where(i == n_steps - 1, NEG_INF, 0.0).astype(F32)
        bias_prev = [mask_ref[0:WINDOW, :] + (first if sub == 0 else 0.0)
                     for sub in range(n_sub)]
        bias_next = [mask_ref[WINDOW:2 * WINDOW, :] + (last if sub == n_sub - 1 else 0.0)
                     for sub in range(n_sub)]

    def sink_row(g, half):
        return jnp.where(left, sink_ref[4 * g + half], sink_ref[4 * g + 2 + half]) * LOG2E

    units = [(sub, half) for sub in range(n_sub) for half in range(2)]

    def scores(g, sub, half):
        rows = slice(sub * tq, (sub + 1) * tq)
        qp = jnp.concatenate([q_ref[0, g, rows, 0:LANES], q_ref[0, g, rows, LANES:2 * LANES]],
                             axis=0)
        qh = jnp.where(low, qp, jnp.zeros_like(qp)) if half == 0 else \
            jnp.where(low, jnp.zeros_like(qp), qp)
        sx = _dot_nt(kx_ref[0, g], qh)
        sx_ref[g, sub, half] = sx
        m = jnp.max(sx, axis=0, keepdims=True)
        if local:
            kl = jnp.concatenate([k_refs[sub + t][0, g] for t in range(3)], axis=0)
            sl = _dot_nt(kl, qh)
            sl = jnp.concatenate(
                [sl[0:WINDOW] + bias_prev[sub], sl[WINDOW:2 * WINDOW],
                 sl[2 * WINDOW:3 * WINDOW] + bias_next[sub]], axis=0)
            sl_ref[g, sub, half] = sl
            m = jnp.maximum(m, jnp.max(sl, axis=0, keepdims=True))
        m_ref[g, sub, half] = jnp.maximum(m, sink_row(g, half))

    def with_ones(vt):
        return jnp.concatenate([vt, jnp.ones((ONES_ROWS, vt.shape[1]), BF16)], axis=0)

    def attend(g, sub, half):
        m = m_ref[g, sub, half]
        px = jnp.exp2(sx_ref[g, sub, half] - m)
        o = _dot(with_ones(vtx_ref[0, g]), px.astype(BF16))
        if local:
            vt = jnp.concatenate([vt_refs[sub + t][0, g] for t in range(3)], axis=1)
            pl_ = jnp.exp2(sl_ref[g, sub, half] - m)
            o = o + _dot(with_ones(vt), pl_.astype(BF16))
        denom = o[HEAD_DIM:HEAD_DIM + 1] + jnp.exp2(sink_row(g, half) - m)
        ot_ref[g, sub, half] = o[0:HEAD_DIM] * (1.0 / denom)

    for sub, half in units:
        scores(0, sub, half)

    def body(g, carry):
        for sub, half in units:
            scores(g + 1, sub, half)
            attend(g, sub, half)
        return carry

    lax.fori_loop(0, N_KV_HEADS - 1, body, 0)

    def project_out(g):
        for sub in range(n_sub):
            rows = slice(sub * tq, (sub + 1) * tq)
            o_t = jnp.concatenate([ot_ref[g, sub, 0], ot_ref[g, sub, 1]], axis=0)
            for r in range(2):
                oc = slice(r * LANES, (r + 1) * LANES)
                o_blk = o_t[:, r * tq:(r + 1) * tq].T
                og_ref[g, _half(rows), oc] = _as_words(
                    (o_blk * gz_ref[0, g, rows, oc].astype(F32)).astype(BF16))
        return _dot(_as_bf16(og_ref[g]), w_out_ref[0, g])

    last = N_KV_HEADS - 1
    out = None
    for g in range(last):
        part = project_out(g)
        out = part if out is None else out + part
        for sub, half in units[g * len(units) // last:(g + 1) * len(units) // last]:
            attend(last, sub, half)
    out = out + project_out(last)
    o_ref[0] = _post_norm(x_ref[0], out, mod_ref, g_ref, b_ref)


def _attend_b(q, kk, vt, kkx, vtx, gz, x, mod, j, w_out, sink, ln_g, ln_b, mask, *, local,
              cast=((), 0)):
    bsz, _, length, group_w = q.shape
    n_ctx = kkx.shape[2]
    tq = ATT_BLOCK
    n_tiles = length // tq
    n_sub = min(ATT_SUB, n_tiles)
    rows = n_sub * tq
    n_steps = length // rows
    grouped_spec = pl.BlockSpec((1, N_KV_HEADS, rows, group_w), lambda b, i: (b, 0, i, 0))
    x_spec = pl.BlockSpec((1, rows, D_MODEL), lambda b, i: (b, i, 0))
    tile = lambda t: (lambda i: jnp.clip(n_sub * i + t - 1, 0, n_tiles - 1))
    k_specs = [pl.BlockSpec((1, N_KV_HEADS, tq, LANES), lambda b, i, f=tile(t): (b, 0, f(i), 0))
               for t in range(n_sub + 2)]
    vt_specs = [pl.BlockSpec((1, N_KV_HEADS, HEAD_DIM, tq),
                             lambda b, i, f=tile(t): (b, 0, 0, f(i))) for t in range(n_sub + 2)]
    in_specs = [pl.BlockSpec(memory_space=pltpu.SMEM), grouped_spec]
    args = [sink, q]
    if local:
        in_specs += k_specs + vt_specs
        args += [kk] * (n_sub + 2) + [vt] * (n_sub + 2)
    in_specs += [pl.BlockSpec((1, N_KV_HEADS, n_ctx, LANES), lambda b, i: (b, 0, 0, 0)),
                 pl.BlockSpec((1, N_KV_HEADS, HEAD_DIM, n_ctx), lambda b, i: (b, 0, 0, 0))]
    args += [kkx, vtx]
    in_specs += [grouped_spec, x_spec, _mod_spec(mod),
                 _layer_spec((N_KV_HEADS, group_w, D_MODEL), j), _const_spec((1, D_MODEL)),
                 _const_spec((1, D_MODEL))]
    args += [gz, x, mod[0], w_out, ln_g, ln_b]
    scratch = [pltpu.VMEM((N_KV_HEADS, n_sub, 2, n_ctx, 2 * tq), F32),
               pltpu.VMEM((N_KV_HEADS, n_sub, 2, 1, 2 * tq), F32),
               pltpu.VMEM((N_KV_HEADS, n_sub, 2, HEAD_DIM, 2 * tq), F32),
               pltpu.VMEM((N_KV_HEADS, rows // 2, group_w), jnp.uint32)]
    if local:
        in_specs.append(_const_spec((2 * WINDOW, 2 * ATT_BLOCK)))
        args.append(mask)
        scratch.append(pltpu.VMEM((N_KV_HEADS, n_sub, 2, 3 * WINDOW, 2 * tq), F32))
    cast_w, cast_layer = cast
    cast_in, cast_out, cast_shapes = _cast_specs(cast_w, cast_layer, bsz * n_steps,
                                                 lambda b, i: b * n_steps + i)
    kernel = functools.partial(_attend_b_kernel, local=local, n_steps=n_steps, n_sub=n_sub,
                               n_cast=len(cast_w))
    outs = pl.pallas_call(
        kernel,
        out_shape=[jax.ShapeDtypeStruct(x.shape, F32)] + cast_shapes,
        grid=(bsz, n_steps),
        in_specs=in_specs + cast_in,
        out_specs=[x_spec] + cast_out,
        scratch_shapes=scratch,
        compiler_params=pltpu.CompilerParams(
            dimension_semantics=("parallel", "parallel"), vmem_limit_bytes=VMEM_LIMIT),
        name="attend_b_local" if local else "attend_b_ctx",
    )(*args, *cast_w)
    return outs[0], outs[1:]


def _rope_tables(length):
    rows = length // GRID_W
    row = jnp.repeat(jnp.arange(rows), GRID_W).astype(F32)
    col = jnp.tile(jnp.arange(GRID_W), rows).astype(F32)
    n_freq = ROPE_AXIS_DIM // 2
    inv = ROPE_BASE ** (-jnp.arange(n_freq, dtype=F32) / n_freq)
    ang_r, ang_c = row[:, None] * inv[None, :], col[:, None] * inv[None, :]
    cr, sr, cc, sc = jnp.cos(ang_r), jnp.sin(ang_r), jnp.cos(ang_c), jnp.sin(ang_c)
    zero = jnp.zeros_like(sr)
    reps = LANES // HEAD_DIM
    cos = jnp.tile(jnp.concatenate([cr, cr, cc, cc], axis=1), (1, reps))
    sin_a = jnp.tile(jnp.concatenate([-sr, zero, -sc, zero], axis=1), (1, reps))
    sin_b = jnp.tile(jnp.concatenate([zero, sr, zero, sc], axis=1), (1, reps))
    return cos, sin_a, sin_b


def _band_mask():
    kj = jnp.arange(WINDOW)[:, None]
    qi = jnp.arange(ATT_BLOCK)[None, :]
    prev = jnp.where(kj >= qi, 0.0, NEG_INF)
    nxt = jnp.where(kj <= qi, 0.0, NEG_INF)
    return jnp.tile(jnp.concatenate([prev, nxt], axis=0), (1, 2)).astype(F32)


def kernel(x, c, ctx, c_ctx, ada_w, ada_b, ln_g, ln_b, a_w_in, a_ln_g, a_ln_b, a_w_s, a_b_s,
           a_w_out, b_w_in, b_sink, b_w_out):
    bsz, length, _ = x.shape
    n_ctx = ctx.shape[1]
    assert bsz + 1 <= MOD_ROWS

    c_rows = jnp.zeros((MOD_ROWS, D_MODEL), F32).at[:bsz].set(c).at[bsz].set(c_ctx)
    mod_all, (a_in_h, a_out_h) = _modulation(c_rows, ada_w, ada_b, cast=((a_w_in, a_w_out), 0))
    mod_rows = mod_all.reshape(DEPTH, MOD_ROWS, 3, D_MODEL)
    tables = _rope_tables(length)
    mask = _band_mask()
    assert N_MIXERS == 2
    a_w_s_h = a_w_s.astype(BF16)
    b_in_h = b_out_h = None

    for i in range(DEPTH):
        j = i // N_MIXERS
        need_ctx_out = i < DEPTH - 1
        mod, mod_c = (mod_rows, i, None), (mod_rows, i, bsz)
        g_i, b_i = ln_g[i].reshape(1, D_MODEL), ln_b[i].reshape(1, D_MODEL)
        if i % N_MIXERS == 0:
            b_s_full = jnp.repeat(a_b_s[j].T, A_GW, axis=1)
            weights = (0, a_in_h, a_ln_g[j].reshape(1, A_WIDTH), a_ln_b[j].reshape(1, A_WIDTH),
                       a_w_s_h[j:j + 1], b_s_full, a_out_h, g_i, b_i)
            cast = ((b_w_in, b_w_out), j) if need_ctx_out else ((), 0)
            x_new, cast_out = _layer_a(x, mod, *weights, tm=A_TILE_ROWS, cast=cast)
            if cast_out:
                b_in_h, b_out_h = cast_out
            if need_ctx_out:
                ctx, _ = _layer_a(ctx, mod_c, *weights, tm=n_ctx)
            x = x_new
        else:
            sink = b_sink[j].astype(F32)
            w_out = b_out_h.reshape(1, N_KV_HEADS, B_WIDTH // N_KV_HEADS, D_MODEL)
            cast = ((a_w_in, a_w_out), j + 1) if need_ctx_out else ((), 0)
            q, kk, vt, gz = _project_b(x, mod, 0, b_in_h, tables, tm=B_TILE_ROWS, rope=True)
            qc, kkc, vtc, gzc = _project_b(ctx, mod_c, 0, b_in_h, tables, tm=n_ctx, rope=False)
            x, cast_out = _attend_b(q, kk, vt, kkc, vtc, gz, x, mod, 0, w_out, sink, g_i, b_i,
                                    mask, local=True, cast=cast)
            if cast_out:
                a_in_h, a_out_h = cast_out
            if need_ctx_out:
                ctx, _ = _attend_b(qc, None, None, kkc, vtc, gzc, ctx, mod_c, 0, w_out, sink,
                                   g_i, b_i, None, local=False)
    return x
```
